```python
import math
import jax, jax.numpy as jnp
from jax import lax
import numpy as np

D_MODEL = 1024
BATCH = 8
SEQ = 2048
DEPTH = 1
DEC_BATCH = 32
DEC_SEQ = 4
PAST_LEN = 16384
PAGE_SIZE = 128

H_A = 4
DK_A = 128
DV_A = 64
W_A = H_A * DV_A
H_B = 4
DH_B = 64
W_B = H_B * 2 * DH_B
H_M = 4
DH_M = 64
W_M = H_M * DH_M
N_MEM = 256
MIX = W_A + W_B + W_M
COL_SIZES = (H_A * DK_A, H_A * DK_A, W_A, W_B, W_B, W_B, W_M, MIX)
SPLITS = tuple(int(s) for s in np.cumsum(COL_SIZES)[:-1])
N_IN = int(sum(COL_SIZES))
CHUNK = 64
Q_BLOCK = 128
EPS = 1e-6
DIFF_SCALE = DH_B ** -0.5
MEM_SCALE = DH_M ** -0.5

kernel_name = "hymba_hgrn2_diffattn_alibi_memxattn_step"


def rmsnorm(x, g):
    xf = x.astype(jnp.float32)
    y = xf * lax.rsqrt(jnp.mean(xf * xf, axis=-1, keepdims=True) + EPS)
    return (y * g.astype(jnp.float32)).astype(x.dtype)


def alibi_slopes():
    return 2.0 ** (-8.0 * jnp.arange(1, H_B + 1, dtype=jnp.float32) / H_B)


def lambda_init(layer):
    return 0.8 - 0.6 * math.exp(-0.3 * layer)


def diff_combine(s, lam):
    a = jax.nn.softmax(s, axis=-1)
    return a[..., 0, :, :] - lam * a[..., 1, :, :]


def hgrn2(q, k, v, g, S0):
    B, L, H, DK = q.shape
    DV = v.shape[-1]
    C = CHUNK if L % CHUNK == 0 else L
    n = L // C

    def chunks(a):
        return jnp.moveaxis(a.astype(jnp.float32).reshape(B, n, C, H, a.shape[-1]), 1, 0)

    causal = jnp.tril(jnp.ones((C, C), dtype=bool))

    def step(S, inp):
        qc, kc, vc, gc = inp
        b = jnp.cumsum(gc, axis=1)
        inter = jnp.einsum('bthk,bhkv->bthv', qc * jnp.exp(b), S)
        diff = b[:, :, None] - b[:, None, :]
        decay = jnp.exp(jnp.where(causal[None, :, :, None, None], diff, -jnp.inf))
        A = jnp.einsum('bthk,btshk,bshk->bhts', qc, decay, kc)
        intra = jnp.einsum('bhts,bshv->bthv', A, vc)
        bl = b[:, -1]
        S_new = jnp.exp(bl)[..., None] * S + jnp.einsum('bshk,bshv->bhkv', kc * jnp.exp(bl[:, None] - b), vc)
        return S_new, inter + intra

    S_fin, o = lax.scan(step, S0.astype(jnp.float32), (chunks(q), chunks(k), chunks(v), chunks(g)))
    o = jnp.moveaxis(o, 0, 1).reshape(B, L, H, DV)
    return o, S_fin


def diff_attn_prompt(q, k, v, lam):
    B, L = q.shape[:2]
    nb = L // Q_BLOCK
    qb = jnp.moveaxis(q.reshape(B, nb, Q_BLOCK, H_B, 2, DH_B), 1, 0)
    starts = jnp.arange(nb, dtype=jnp.int32) * Q_BLOCK
    spos = jnp.arange(L, dtype=jnp.int32)
    slopes = alibi_slopes()

    def blk(args):
        qc, t0 = args
        tpos = t0 + jnp.arange(Q_BLOCK, dtype=jnp.int32)
        dist = tpos[:, None] - spos[None, :]
        s = jnp.einsum('bthjd,bshjd->bhjts', qc, k).astype(jnp.float32) * DIFF_SCALE
        bias = -slopes[:, None, None] * dist.astype(jnp.float32)
        s = jnp.where(dist >= 0, s + bias[None, :, None], -jnp.inf)
        a = diff_combine(s, lam)
        return jnp.einsum('bhts,bshe->bthe', a.astype(v.dtype), v)

    o = lax.map(blk, (qb, starts))
    return jnp.moveaxis(o, 0, 1).reshape(B, L, H_B, 2 * DH_B)


def diff_attn_sample(q, k_new, v_new, cache_k_l, cache_v_l, page_table, lam):
    T = q.shape[1]
    past = page_table.shape[1] * PAGE_SIZE
    slopes = alibi_slopes()
    tpos = past + jnp.arange(T, dtype=jnp.int32)
    ppos = jnp.arange(past, dtype=jnp.int32)
    bias_past = -slopes[:, None, None] * (tpos[:, None] - ppos[None, :]).astype(jnp.float32)
    dist_new = tpos[:, None] - tpos[None, :]
    bias_new = -slopes[:, None, None] * dist_new.astype(jnp.float32)

    def one(args):
        qs, ks, vs, pages = args
        kp = cache_k_l[pages].reshape(past, H_B, 2, DH_B)
        vp = cache_v_l[pages].reshape(past, H_B, 2 * DH_B)
        s_p = jnp.einsum('thjd,shjd->hjts', qs, kp).astype(jnp.float32) * DIFF_SCALE + bias_past[:, None]
        s_n = jnp.einsum('thjd,shjd->hjts', qs, ks).astype(jnp.float32) * DIFF_SCALE + bias_new[:, None]
        s_n = jnp.where(dist_new >= 0, s_n, -jnp.inf)
        a = diff_combine(jnp.concatenate([s_p, s_n], axis=-1), lam).astype(vs.dtype)
        return (jnp.einsum('hts,she->the', a[..., :past], vp)
                + jnp.einsum('hts,she->the', a[..., past:], vs))

    return lax.map(one, (q, k_new, v_new, page_table))


def mem_kv(mem, g, w):
    B, N, _ = mem.shape
    kv = rmsnorm(mem, g) @ w
    mk, mv = jnp.split(kv, 2, axis=-1)
    return mk.reshape(B, N, H_M, DH_M), mv.reshape(B, N, H_M, DH_M)


def mem_attn(q, mk, mv):
    s = jnp.einsum('blhd,bnhd->bhln', q, mk).astype(jnp.float32) * MEM_SCALE
    a = jax.nn.softmax(s, axis=-1)
    return jnp.einsum('bhln,bnhd->blhd', a.astype(mv.dtype), mv)


def layer_forward(x, l, norm_g, w_in, lb, hgrn_norm_g, diff_norm_g, lam, w_out, S0, diff_fn, mk, mv):
    B, L, _ = x.shape
    h = rmsnorm(x, norm_g[l])
    proj = h @ w_in[l]
    hq, hf, hi, dq, dk, dv, mq, z = jnp.split(proj, SPLITS, axis=-1)
    fgate = lb + (1.0 - lb) * jax.nn.sigmoid(hf.astype(jnp.float32))
    o_a, S_new = hgrn2(hq.reshape(B, L, H_A, DK_A),
                       (1.0 - fgate).reshape(B, L, H_A, DK_A),
                       hi.reshape(B, L, H_A, DV_A),
                       jnp.log(fgate).reshape(B, L, H_A, DK_A), S0)
    o_a = rmsnorm(o_a.astype(x.dtype), hgrn_norm_g[l]).reshape(B, L, W_A)
    kd = dk.reshape(B, L, H_B, 2, DH_B)
    vd = dv.reshape(B, L, H_B, 2 * DH_B)
    o_b = diff_fn(dq.reshape(B, L, H_B, 2, DH_B), kd, vd, lam)
    o_b = (rmsnorm(o_b, diff_norm_g[l]) * (1.0 - lambda_init(l))).reshape(B, L, W_B)
    o_m = mem_attn(mq.reshape(B, L, H_M, DH_M), mk, mv).reshape(B, L, W_M)
    o = jnp.concatenate([o_a, o_b, o_m], axis=-1) * jax.nn.silu(z)
    y = x + o @ w_out[l]
    return y, kd.reshape(B, L, H_B, 2 * DH_B), vd, S_new


def setup_inputs(seed: int = 0) -> dict:
    key = jax.random.key(seed)
    ks = jax.random.split(key, 24)
    n_pages = PAST_LEN // PAGE_SIZE
    n_used = DEC_BATCH * n_pages
    n_pool = n_used + max(1, n_used // 4)
    f32 = jnp.float32
    page_table = jax.random.permutation(ks[0], n_pool)[:n_used].reshape(DEC_BATCH, n_pages).astype(jnp.int32)
    return {
        "x_prompt": jax.random.normal(ks[1], (BATCH, SEQ, D_MODEL), f32),
        "x_sample": jax.random.normal(ks[2], (DEC_BATCH, DEC_SEQ, D_MODEL), f32),
        "mem_prompt": jax.random.normal(ks[3], (BATCH, N_MEM, D_MODEL), f32),
        "cache_k": jax.random.normal(ks[4], (DEPTH, n_pool, PAGE_SIZE, H_B, 2 * DH_B), f32),
        "cache_v": jax.random.normal(ks[5], (DEPTH, n_pool, PAGE_SIZE, H_B, 2 * DH_B), f32),
        "state_hgrn": 0.5 * jax.random.normal(ks[6], (DEPTH, DEC_BATCH, H_A, DK_A, DV_A), f32),
        "cache_mem_k": jax.random.normal(ks[7], (DEPTH, DEC_BATCH, N_MEM, H_M, DH_M), f32),
        "cache_mem_v": jax.random.normal(ks[8], (DEPTH, DEC_BATCH, N_MEM, H_M, DH_M), f32),
        "page_table": page_table,
        "norm_g": 1.0 + 0.02 * jax.random.normal(ks[9], (DEPTH, D_MODEL), f32),
        "w_in": jax.random.normal(ks[10], (DEPTH, D_MODEL, N_IN), f32) * D_MODEL ** -0.5,
        "hgrn_lb_logits": 0.5 * jax.random.normal(ks[11], (DEPTH + 1, H_A * DK_A), f32),
        "hgrn_norm_g": 1.0 + 0.02 * jax.random.normal(ks[12], (DEPTH, DV_A), f32),
        "diff_norm_g": 1.0 + 0.02 * jax.random.normal(ks[13], (DEPTH, 2 * DH_B), f32),
        "lambda_q1": 0.1 * jax.random.normal(ks[14], (DEPTH, DH_B), f32),
        "lambda_k1": 0.1 * jax.random.normal(ks[15], (DEPTH, DH_B), f32),
        "lambda_q2": 0.1 * jax.random.normal(ks[16], (DEPTH, DH_B), f32),
        "lambda_k2": 0.1 * jax.random.normal(ks[17], (DEPTH, DH_B), f32),
        "mem_norm_g": 1.0 + 0.02 * jax.random.normal(ks[18], (DEPTH, D_MODEL), f32),
        "w_mem_kv": jax.random.normal(ks[19], (DEPTH, D_MODEL, 2 * W_M), f32) * D_MODEL ** -0.5,
        "w_out": jax.random.normal(ks[20], (DEPTH, MIX, D_MODEL), f32) * MIX ** -0.5,
        "final_g": 1.0 + 0.02 * jax.random.normal(ks[21], (D_MODEL,), f32),
    }


def reference(x_prompt, x_sample, mem_prompt, cache_k, cache_v, state_hgrn, cache_mem_k, cache_mem_v,
              page_table, norm_g, w_in, hgrn_lb_logits, hgrn_norm_g, diff_norm_g, lambda_q1, lambda_k1,
              lambda_q2, lambda_k2, mem_norm_g, w_mem_kv, w_out, final_g):
    lbs = jnp.cumsum(jax.nn.softmax(hgrn_lb_logits.astype(jnp.float32), axis=0), axis=0)
    yp, ys = x_prompt, x_sample
    kp_l, vp_l, sp_l, mkp_l, mvp_l, ks_l, vs_l, ss_l = [], [], [], [], [], [], [], []
    for l in range(DEPTH):
        lb = lbs[l]
        lam = (jnp.exp(jnp.sum(lambda_q1[l].astype(jnp.float32) * lambda_k1[l].astype(jnp.float32)))
               - jnp.exp(jnp.sum(lambda_q2[l].astype(jnp.float32) * lambda_k2[l].astype(jnp.float32)))
               + lambda_init(l))
        mk, mv = mem_kv(mem_prompt, mem_norm_g[l], w_mem_kv[l])
        S0 = jnp.zeros((yp.shape[0], H_A, DK_A, DV_A), jnp.float32)
        yp, k_p, v_p, S_p = layer_forward(yp, l, norm_g, w_in, lb, hgrn_norm_g, diff_norm_g, lam, w_out,
                                          S0, diff_attn_prompt, mk, mv)
        kp_l.append(k_p); vp_l.append(v_p); sp_l.append(S_p); mkp_l.append(mk); mvp_l.append(mv)
        ck, cv = cache_k[l], cache_v[l]
        sample_fn = lambda q, k, v, lm, ck=ck, cv=cv: diff_attn_sample(q, k, v, ck, cv, page_table, lm)
        ys, k_s, v_s, S_s = layer_forward(ys, l, norm_g, w_in, lb, hgrn_norm_g, diff_norm_g, lam, w_out,
                                          state_hgrn[l], sample_fn, cache_mem_k[l], cache_mem_v[l])
        ks_l.append(k_s); vs_l.append(v_s); ss_l.append(S_s)
    y_prompt = rmsnorm(yp, final_g)
    y_sample = rmsnorm(ys, final_g)
    return (y_prompt, y_sample, jnp.stack(kp_l), jnp.stack(vp_l), jnp.stack(sp_l), jnp.stack(mkp_l),
            jnp.stack(mvp_l), jnp.stack(ks_l), jnp.stack(vs_l), jnp.stack(ss_l))
```

```python
import functools
import math

import jax
import jax.numpy as jnp
from jax import lax
from jax.experimental import pallas as pl
from jax.experimental.pallas import tpu as pltpu

F32 = jnp.float32
BF16 = jnp.bfloat16
EPS = 1e-6
NEG = -1e30
VMEM_LIMIT = 48 * 1024 * 1024
SAMPLE_ROWS = 8

_NT = (((1,), (1,)), ((), ()))
_TN = (((0,), (0,)), ((), ()))


def _dot(a, b):
    return jnp.dot(a, b, preferred_element_type=F32)


def _dot_nt(a, b):
    return lax.dot_general(a, b, _NT, preferred_element_type=F32)


def _dot_tn(a, b):
    return lax.dot_general(a, b, _TN, preferred_element_type=F32)


def _rms(x, g):
    return x * lax.rsqrt(jnp.mean(x * x, axis=-1, keepdims=True) + EPS) * g


def _params(*sem):
    return pltpu.CompilerParams(dimension_semantics=sem, vmem_limit_bytes=VMEM_LIMIT)


def _proj_kernel(x_ref, g_ref, w_ref, *out_refs, col_sizes):
    h = _rms(x_ref[...], g_ref[...]).astype(BF16)
    start = 0
    for ref, size in zip(out_refs, col_sizes):
        ref[...] = _dot(h, w_ref[:, start:start + size]).astype(ref.dtype)
        start += size


def _norm_proj(x2d, g, w_bf16, col_sizes, tm, name):
    m, d = x2d.shape
    n = w_bf16.shape[1]
    assert sum(col_sizes) == n and m % tm == 0
    return pl.pallas_call(
        functools.partial(_proj_kernel, col_sizes=col_sizes),
        grid=(m // tm,),
        in_specs=[pl.BlockSpec((tm, d), lambda i: (i, 0)),
                  pl.BlockSpec((1, d), lambda i: (0, 0)),
                  pl.BlockSpec((d, n), lambda i: (0, 0))],
        out_specs=[pl.BlockSpec((tm, s), lambda i: (i, 0)) for s in col_sizes],
        out_shape=[jax.ShapeDtypeStruct((m, s), F32) for s in col_sizes],
        compiler_params=_params("parallel"),
        name=name,
    )(x2d, g.reshape(1, d), w_bf16)


def _lower_bound(lg, layer, axis=0):
    e = jnp.exp(lg - jnp.max(lg, axis=axis, keepdims=True))
    p = e / jnp.sum(e, axis=axis, keepdims=True)
    head = p[:layer + 1] if axis == 0 else p[:, :layer + 1]
    return jnp.sum(head, axis=axis, keepdims=True)


def _hgrn_chunk_kernel(pa_ref, lg_ref, ng_ref, o_ref, sout_ref, s_scr, b_scr,
                       *, C, layer, H, DK, DV):
    c = pl.program_id(1)

    @pl.when(c == 0)
    def _():
        s_scr[...] = jnp.zeros_like(s_scr)

    W = H * DK
    lb = _lower_bound(lg_ref[...], layer)
    f = lb + (1.0 - lb) * jax.nn.sigmoid(pa_ref[:, W:2 * W])
    g = jnp.log(f)
    kk = 1.0 - f
    q = pa_ref[:, 0:W]

    ti = lax.broadcasted_iota(jnp.int32, (C, C), 0)
    si = lax.broadcasted_iota(jnp.int32, (C, C), 1)
    lmat = (si <= ti).astype(BF16)
    g_hi = g.astype(BF16)
    r1 = g - g_hi.astype(F32)
    g_mid = r1.astype(BF16)
    g_lo = (r1 - g_mid.astype(F32)).astype(BF16)
    b = _dot(lmat, g_hi) + _dot(lmat, g_mid) + _dot(lmat, g_lo)
    b_scr[...] = b
    bl = b_scr[C - 1:C, :]

    row = lax.broadcasted_iota(jnp.int32, (C, 1), 0)
    levels = []
    m = C // 2
    while m >= 1:
        upper = (row & (2 * m - 1)) >= m
        if m >= 4:
            pieces = [jnp.broadcast_to(b_scr[blk * 2 * m + m - 1:blk * 2 * m + m, :], (2 * m, W))
                      for blk in range(C // (2 * m))]
            bref = pieces[0] if len(pieces) == 1 else jnp.concatenate(pieces, axis=0)
            dq = b - bref
            dk = bref - b
        elif m == 2:
            dq = g + jnp.where((row & 3) == 3, pltpu.roll(g, 1, 0), 0.0)
            dk = jnp.where((row & 3) == 0, pltpu.roll(g, C - 1, 0), 0.0)
        else:
            dq = g
            dk = jnp.zeros_like(g)
        ql = jnp.where(upper, q * jnp.exp(dq), 0.0).astype(BF16)
        kl = jnp.where(upper, 0.0, kk * jnp.exp(dk)).astype(BF16)
        sh = (2 * m).bit_length() - 1
        same = None if 2 * m == C else (ti >> sh) == (si >> sh)
        levels.append((ql, kl, same))
        m //= 2
    eye = ti == si
    q_bf = q.astype(BF16)
    k_bf = kk.astype(BF16)
    q_in = (q * jnp.exp(b)).astype(BF16)
    k_out = (kk * jnp.exp(bl - b)).astype(BF16)
    ebl = jnp.exp(bl)

    for h in range(H):
        ks = slice(h * DK, (h + 1) * DK)
        vs = slice(h * DV, (h + 1) * DV)
        v = pa_ref[:, 2 * W + h * DV:2 * W + (h + 1) * DV].astype(BF16)
        a = jnp.where(eye, _dot_nt(q_bf[:, ks], k_bf[:, ks]), 0.0)
        for ql, kl, same in levels:
            r = _dot_nt(ql[:, ks], kl[:, ks])
            a = a + (r if same is None else jnp.where(same, r, 0.0))
        st = s_scr[h]
        o = _dot_nt(q_in[:, ks], st.astype(BF16)) + _dot(a.astype(BF16), v)
        o_ref[:, vs] = _rms(o, ng_ref[...])
        s_scr[h] = ebl[:, ks] * st + _dot_tn(v, k_out[:, ks])

    @pl.when(c == pl.num_programs(1) - 1)
    def _():
        sout_ref[...] = s_scr[...]


def _hgrn_prompt(pa, lg, ng, layer, H, DK, DV, C=128):
    B, L, wp = pa.shape
    W = H * DK
    assert L % C == 0 and wp == 2 * W + H * DV
    return pl.pallas_call(
        functools.partial(_hgrn_chunk_kernel, C=C, layer=layer, H=H, DK=DK, DV=DV),
        grid=(B, L // C),
        in_specs=[pl.BlockSpec((None, C, wp), lambda b, c: (b, c, 0)),
                  pl.BlockSpec(lg.shape, lambda b, c: (0, 0)),
                  pl.BlockSpec((1, DV), lambda b, c: (0, 0))],
        out_specs=[pl.BlockSpec((None, C, H * DV), lambda b, c: (b, c, 0)),
                   pl.BlockSpec((None, H, DV, DK), lambda b, c: (b, 0, 0, 0))],
        out_shape=[jax.ShapeDtypeStruct((B, L, H * DV), F32),
                   jax.ShapeDtypeStruct((B, H, DV, DK), F32)],
        scratch_shapes=[pltpu.VMEM((H, DV, DK), F32), pltpu.VMEM((C, W), F32)],
        compiler_params=_params("parallel", "arbitrary"),
        name="hgrn_prompt",
    )(pa, lg, ng.reshape(1, DV))


def _hgrn_step_kernel(qt_ref, ft_ref, v_ref, lg_ref, ng_ref, s0_ref, o_ref, sout_ref,
                      *, T, layer, H, DK, DV):
    lbt = _lower_bound(lg_ref[...], layer, axis=1)
    f_all = lbt + (1.0 - lbt) * jax.nn.sigmoid(ft_ref[...])
    q_all = qt_ref[...]
    o_ref[...] = jnp.zeros_like(o_ref)
    for h in range(H):
        ks = slice(h * DK, (h + 1) * DK)
        s = s0_ref[h]
        for t in range(T):
            f = f_all[ks, t:t + 1]
            v = v_ref[t:t + 1, h * DV:(h + 1) * DV]
            s = f * s + (1.0 - f) * v
            o = jnp.sum(s * q_all[ks, t:t + 1], axis=0, keepdims=True)
            o_ref[t:t + 1, h * DV:(h + 1) * DV] = _rms(o, ng_ref[...])
        sout_ref[h] = s


def _hgrn_sample(qt, ft, v, lgt, ng, s0, layer, T, H, DK, DV):
    B = qt.shape[0]
    W = H * DK
    rows = v.shape[1]
    return pl.pallas_call(
        functools.partial(_hgrn_step_kernel, T=T, layer=layer, H=H, DK=DK, DV=DV),
        grid=(B,),
        in_specs=[pl.BlockSpec((None, W, T), lambda b: (b, 0, 0)),
                  pl.BlockSpec((None, W, T), lambda b: (b, 0, 0)),
                  pl.BlockSpec((None, rows, H * DV), lambda b: (b, 0, 0)),
                  pl.BlockSpec(lgt.shape, lambda b: (0, 0)),
                  pl.BlockSpec((1, DV), lambda b: (0, 0)),
                  pl.BlockSpec((None, H, DK, DV), lambda b: (b, 0, 0, 0))],
        out_specs=[pl.BlockSpec((None, rows, H * DV), lambda b: (b, 0, 0)),
                   pl.BlockSpec((None, H, DK, DV), lambda b: (b, 0, 0, 0))],
        out_shape=[jax.ShapeDtypeStruct((B, rows, H * DV), F32),
                   jax.ShapeDtypeStruct((B, H, DK, DV), F32)],
        compiler_params=_params("parallel"),
        name="hgrn_sample",
    )(qt, ft, v, lgt, ng.reshape(1, DV), s0)


def _online_update(s, v_bf, m_scr, l_scr, acc_scr):
    m_prev = m_scr[...]
    m_new = jnp.maximum(m_prev, jnp.max(s, axis=-1, keepdims=True))
    alpha = jnp.exp(m_prev - m_new)
    p = jnp.exp(s - m_new)
    l_scr[...] = alpha * l_scr[...] + jnp.sum(p, axis=-1, keepdims=True)
    acc_scr[...] = alpha * acc_scr[...] + _dot(p.astype(BF16), v_bf)
    m_scr[...] = m_new


def _diff_prompt_kernel(slope_ref, lam_ref, q_ref, k_ref, v_ref, ng_ref, o_ref,
                        m_scr, l_scr, acc_scr, *, T, DH, scale, out_scale):
    h = pl.program_id(1)
    qi = pl.program_id(2)
    slope = slope_ref[h]
    q = q_ref[...] * scale
    lane = lax.broadcasted_iota(jnp.int32, q.shape, 1)
    qs = jnp.concatenate([jnp.where(lane < DH, q, 0.0), jnp.where(lane < DH, 0.0, q)],
                         axis=0).astype(BF16)
    ti = lax.broadcasted_iota(jnp.int32, (2 * T, T), 0) & (T - 1)
    si = lax.broadcasted_iota(jnp.int32, (2 * T, T), 1)
    base = slope * (si - ti).astype(F32)

    m_scr[...] = jnp.full_like(m_scr, NEG)
    l_scr[...] = jnp.zeros_like(l_scr)
    acc_scr[...] = jnp.zeros_like(acc_scr)

    def scores(kj):
        k = k_ref[pl.ds(pl.multiple_of(kj * T, T), T), :].astype(BF16)
        v = v_ref[pl.ds(pl.multiple_of(kj * T, T), T), :].astype(BF16)
        off = slope * ((kj - qi) * T).astype(F32)
        return _dot_nt(qs, k) + (base + off), v

    def body(kj, carry):
        s, v = scores(kj)
        _online_update(s, v, m_scr, l_scr, acc_scr)
        return carry

    lax.fori_loop(0, qi, body, 0)
    s, v = scores(qi)
    _online_update(jnp.where(si <= ti, s, NEG), v, m_scr, l_scr, acc_scr)

    o2 = acc_scr[...] / l_scr[...]
    o = o2[:T] - lam_ref[0] * o2[T:]
    o_ref[...] = _rms(o, ng_ref[...]) * out_scale


def _diff_prompt(q, k, v, slopes, lam, ng, H, DH, out_scale, T=256):
    B, L, _ = q.shape
    E = 2 * DH
    assert L % T == 0 and T & (T - 1) == 0
    grid_spec = pltpu.PrefetchScalarGridSpec(
        num_scalar_prefetch=0,
        grid=(B, H, L // T),
        in_specs=[pl.BlockSpec(memory_space=pltpu.SMEM),
                  pl.BlockSpec(memory_space=pltpu.SMEM),
                  pl.BlockSpec((None, T, E), lambda b, h, i: (b, i, h)),
                  pl.BlockSpec((None, L, E), lambda b, h, i: (b, 0, h)),
                  pl.BlockSpec((None, L, E), lambda b, h, i: (b, 0, h)),
                  pl.BlockSpec((1, E), lambda b, h, i: (0, 0))],
        out_specs=pl.BlockSpec((None, T, E), lambda b, h, i: (b, i, h)),
        scratch_shapes=[pltpu.VMEM((2 * T, 1), F32), pltpu.VMEM((2 * T, 1), F32),
                        pltpu.VMEM((2 * T, E), F32)])
    return pl.pallas_call(
        functools.partial(_diff_prompt_kernel, T=T, DH=DH, scale=DH ** -0.5, out_scale=out_scale),
        grid_spec=grid_spec,
        out_shape=jax.ShapeDtypeStruct((B, L, H * E), F32),
        compiler_params=_params("parallel", "parallel", "arbitrary"),
        name="diff_prompt",
    )(slopes, lam, q, k, v, ng.reshape(1, E))


def _diff_sample_kernel(pt_ref, slope_ref, lam_ref, q_ref, kn_ref, vn_ref, ng_ref, *rest,
                        PP, PAGE, T, H, DH, past, scale, out_scale):
    k_refs = rest[:PP]
    v_refs = rest[PP:2 * PP]
    o_ref = rest[2 * PP]
    m_scr, l_scr, acc_scr = rest[2 * PP + 1:]
    step = pl.program_id(1)
    E = 2 * DH
    R = SAMPLE_ROWS
    NR = 2 * H * R

    @pl.when(step == 0)
    def _():
        m_scr[...] = jnp.full_like(m_scr, NEG)
        l_scr[...] = jnp.zeros_like(l_scr)
        acc_scr[...] = jnp.zeros_like(acc_scr)

    q = q_ref[...] * scale
    lane = lax.broadcasted_iota(jnp.int32, q.shape, 1)
    qs = jnp.concatenate([jnp.where((lane // DH) == j, q, 0.0) for j in range(2 * H)],
                         axis=0).astype(BF16)
    rowi = lax.broadcasted_iota(jnp.int32, (NR, 1), 0)
    tq = rowi & (R - 1)
    slope = jnp.zeros((NR, 1), F32)
    for h in range(H):
        slope = jnp.where((rowi // (2 * R)) == h, slope_ref[h], slope)

    col = lax.broadcasted_iota(jnp.int32, (NR, PP * PAGE), 1)
    dist = (past + tq) - (step * (PP * PAGE) + col)
    s = jnp.concatenate([_dot_nt(qs, k_refs[j][...].astype(BF16)) for j in range(PP)], axis=1)
    s = s - slope * dist.astype(F32)
    m_prev = m_scr[...]
    m_new = jnp.maximum(m_prev, jnp.max(s, axis=-1, keepdims=True))
    alpha = jnp.exp(m_prev - m_new)
    p = jnp.exp(s - m_new)
    l_scr[...] = alpha * l_scr[...] + jnp.sum(p, axis=-1, keepdims=True)
    pb = p.astype(BF16)
    pv = _dot(pb[:, 0:PAGE], v_refs[0][...].astype(BF16))
    for j in range(1, PP):
        pv = pv + _dot(pb[:, j * PAGE:(j + 1) * PAGE], v_refs[j][...].astype(BF16))
    acc_scr[...] = alpha * acc_scr[...] + pv
    m_scr[...] = m_new

    @pl.when(step == pl.num_programs(1) - 1)
    def _():
        pad = jnp.zeros((PAGE - R, H * E), F32)
        kn = jnp.concatenate([kn_ref[...], pad], axis=0).astype(BF16)
        vn = jnp.concatenate([vn_ref[...], pad], axis=0).astype(BF16)
        coln = lax.broadcasted_iota(jnp.int32, (NR, PAGE), 1)
        dn = tq - coln
        sn = _dot_nt(qs, kn) - slope * dn.astype(F32)
        sn = jnp.where((dn >= 0) & (coln < T), sn, NEG)
        _online_update(sn, vn, m_scr, l_scr, acc_scr)
        o2 = acc_scr[...] / l_scr[...]
        for h in range(H):
            blk = o2[2 * h * R:2 * (h + 1) * R, h * E:(h + 1) * E]
            o = blk[:R] - lam_ref[0] * blk[R:]
            o_ref[:, h * E:(h + 1) * E] = _rms(o, ng_ref[...]) * out_scale


def _diff_sample(q, kn, vn, cache_k, cache_v, page_ids, slopes, lam, ng, T, H, DH, out_scale, PP=8):
    B = q.shape[0]
    n_pages = page_ids.shape[1]
    PAGE = cache_k.shape[1]
    E = 2 * DH
    R = SAMPLE_ROWS
    NR = 2 * H * R
    assert n_pages % PP == 0 and q.shape[1] == R
    row_spec = pl.BlockSpec((None, R, H * E), lambda b, s, pt: (b, 0, 0))

    def page_spec(j):
        return pl.BlockSpec((None, PAGE, H * E), lambda b, s, pt: (pt[b, s * PP + j], 0, 0))

    grid_spec = pltpu.PrefetchScalarGridSpec(
        num_scalar_prefetch=1,
        grid=(B, n_pages // PP),
        in_specs=[pl.BlockSpec(memory_space=pltpu.SMEM),
                  pl.BlockSpec(memory_space=pltpu.SMEM),
                  row_spec, row_spec, row_spec,
                  pl.BlockSpec((1, E), lambda b, s, pt: (0, 0))]
                 + [page_spec(j) for j in range(PP)] + [page_spec(j) for j in range(PP)],
        out_specs=row_spec,
        scratch_shapes=[pltpu.VMEM((NR, 1), F32), pltpu.VMEM((NR, 1), F32),
                        pltpu.VMEM((NR, H * E), F32)])
    return pl.pallas_call(
        functools.partial(_diff_sample_kernel, PP=PP, PAGE=PAGE, T=T, H=H, DH=DH,
                          past=n_pages * PAGE, scale=DH ** -0.5, out_scale=out_scale),
        grid_spec=grid_spec,
        out_shape=jax.ShapeDtypeStruct((B, R, H * E), F32),
        compiler_params=_params("parallel", "arbitrary"),
        name="diff_sample",
    )(page_ids, slopes, lam, q, kn, vn, ng.reshape(1, E), *([cache_k] * PP), *([cache_v] * PP))


def _mem_attn_kernel(q_ref, k_ref, v_ref, o_ref, *, H, DH, scale):
    q = q_ref[...] * scale
    T = q.shape[0]
    lane_h = lax.broadcasted_iota(jnp.int32, q.shape, 1) // DH
    qs = jnp.concatenate([jnp.where(lane_h == h, q, 0.0) for h in range(H)], axis=0).astype(BF16)
    s = _dot_nt(qs, k_ref[...].astype(BF16))
    p = jnp.exp(s - jnp.max(s, axis=-1, keepdims=True))
    o4 = _dot(p.astype(BF16), v_ref[...].astype(BF16)) / jnp.sum(p, axis=-1, keepdims=True)
    o = jnp.zeros_like(q)
    for h in range(H):
        o = jnp.where(lane_h == h, o4[h * T:(h + 1) * T], o)
    o_ref[...] = o


def _mem_attn(q, mk, mv, H, DH, T):
    B, L, W = q.shape
    N = mk.shape[1]
    assert L % T == 0
    return pl.pallas_call(
        functools.partial(_mem_attn_kernel, H=H, DH=DH, scale=DH ** -0.5),
        grid=(B, L // T),
        in_specs=[pl.BlockSpec((None, T, W), lambda b, i: (b, i, 0)),
                  pl.BlockSpec((None, N, W), lambda b, i: (b, 0, 0)),
                  pl.BlockSpec((None, N, W), lambda b, i: (b, 0, 0))],
        out_specs=pl.BlockSpec((None, T, W), lambda b, i: (b, i, 0)),
        out_shape=jax.ShapeDtypeStruct((B, L, W), F32),
        compiler_params=_params("parallel", "parallel"),
        name="mem_attn",
    )(q, mk, mv)


def _out_kernel(x_ref, oa_ref, ob_ref, om_ref, z_ref, w_ref, g_ref, y_ref, *, final):
    o = jnp.concatenate([oa_ref[...], ob_ref[...], om_ref[...]], axis=-1)
    z = z_ref[...]
    o = (o * (z * jax.nn.sigmoid(z))).astype(BF16)
    y = x_ref[...] + _dot(o, w_ref[...])
    y_ref[...] = _rms(y, g_ref[...]) if final else y


def _out_proj(x2d, oa, ob, om, z, w_bf16, g, tm, final):
    m, d = x2d.shape
    mix = w_bf16.shape[0]

    def rows(a):
        return pl.BlockSpec((tm, a.shape[1]), lambda i: (i, 0))

    return pl.pallas_call(
        functools.partial(_out_kernel, final=final),
        grid=(m // tm,),
        in_specs=[rows(x2d), rows(oa), rows(ob), rows(om), rows(z),
                  pl.BlockSpec((mix, d), lambda i: (0, 0)),
                  pl.BlockSpec((1, d), lambda i: (0, 0))],
        out_specs=rows(x2d),
        out_shape=jax.ShapeDtypeStruct((m, d), F32),
        compiler_params=_params("parallel"),
        name="out_proj",
    )(x2d, oa, ob, om, z, w_bf16, g.reshape(1, d))


def kernel(x_prompt, x_sample, mem_prompt, cache_k, cache_v, state_hgrn, cache_mem_k, cache_mem_v, page_table, norm_g, w_in, hgrn_lb_logits, hgrn_norm_g, diff_norm_g, lambda_q1, lambda_k1, lambda_q2, lambda_k2, mem_norm_g, w_mem_kv, w_out, final_g):
    B, L, D = x_prompt.shape
    SB, T, _ = x_sample.shape
    depth, _, H_A, DK_A, DV_A = state_hgrn.shape
    _, n_pool, PAGE, H_B, E_B = cache_k.shape
    DH_B = E_B // 2
    _, _, N_MEM, H_M, DH_M = cache_mem_k.shape
    W_A, W_K, W_B, W_M = H_A * DV_A, H_A * DK_A, H_B * E_B, H_M * DH_M
    MIX = W_A + W_B + W_M
    R = SAMPLE_ROWS
    assert T <= R
    col_sizes = (2 * W_K + W_A, W_B, W_B, W_B, W_M, MIX)

    slopes = jnp.asarray([2.0 ** (-8.0 * (i + 1) / H_B) for i in range(H_B)], F32)
    cache_k2 = cache_k.reshape(depth * n_pool, PAGE, W_B)
    cache_v2 = cache_v.reshape(depth * n_pool, PAGE, W_B)

    yp = x_prompt.reshape(B * L, D)
    ys = jnp.pad(x_sample, ((0, 0), (0, R - T), (0, 0))).reshape(SB * R, D)
    outs = {k: [] for k in ("kp", "vp", "sp", "mkp", "mvp", "ks", "vs", "ss")}
    for l in range(depth):
        lam_init = 0.8 - 0.6 * math.exp(-0.3 * l)
        lam = (jnp.exp(jnp.sum(lambda_q1[l] * lambda_k1[l])) - jnp.exp(jnp.sum(lambda_q2[l] * lambda_k2[l]))
               + lam_init).reshape(1).astype(F32)
        w_in_b = w_in[l].astype(BF16)
        w_out_b = w_out[l].astype(BF16)
        w_mem_b = w_mem_kv[l].astype(BF16)

        mk, mv = _norm_proj(mem_prompt.reshape(B * N_MEM, D), mem_norm_g[l], w_mem_b, (W_M, W_M), 256, "mem_kv")
        pa, dq, dk, dv, mq, z = _norm_proj(yp, norm_g[l], w_in_b, col_sizes, 256, "in_proj_prompt")
        oa, st = _hgrn_prompt(pa.reshape(B, L, -1), hgrn_lb_logits, hgrn_norm_g[l], l, H_A, DK_A, DV_A)
        ob = _diff_prompt(dq.reshape(B, L, W_B), dk.reshape(B, L, W_B), dv.reshape(B, L, W_B),
                          slopes, lam, diff_norm_g[l], H_B, DH_B, 1.0 - lam_init)
        om = _mem_attn(mq.reshape(B, L, W_M), mk.reshape(B, N_MEM, W_M), mv.reshape(B, N_MEM, W_M),
                       H_M, DH_M, 256)
        yp = _out_proj(yp, oa.reshape(B * L, W_A), ob.reshape(B * L, W_B), om.reshape(B * L, W_M), z,
                       w_out_b, final_g, 256, l == depth - 1)
        outs["kp"].append(dk.reshape(B, L, H_B, E_B))
        outs["vp"].append(dv.reshape(B, L, H_B, E_B))
        outs["sp"].append(jnp.swapaxes(st, -1, -2))
        outs["mkp"].append(mk.reshape(B, N_MEM, H_M, DH_M))
        outs["mvp"].append(mv.reshape(B, N_MEM, H_M, DH_M))

        pa, dq, dk, dv, mq, z = _norm_proj(ys, norm_g[l], w_in_b, col_sizes, SB * R, "in_proj_sample")
        pa3 = pa.reshape(SB, R, -1)
        qt = jnp.swapaxes(pa3[:, :T, 0:W_K], 1, 2)
        ft = jnp.swapaxes(pa3[:, :T, W_K:2 * W_K], 1, 2)
        oa, ss = _hgrn_sample(qt, ft, pa3[:, :, 2 * W_K:], hgrn_lb_logits.T, hgrn_norm_g[l], state_hgrn[l],
                              l, T, H_A, DK_A, DV_A)
        ob = _diff_sample(dq.reshape(SB, R, W_B), dk.reshape(SB, R, W_B), dv.reshape(SB, R, W_B),
                          cache_k2, cache_v2, page_table + l * n_pool, slopes, lam, diff_norm_g[l],
                          T, H_B, DH_B, 1.0 - lam_init)
        om = _mem_attn(mq.reshape(SB, R, W_M), cache_mem_k[l].reshape(SB, N_MEM, W_M),
                       cache_mem_v[l].reshape(SB, N_MEM, W_M), H_M, DH_M, R)
        ys = _out_proj(ys, oa.reshape(SB * R, W_A), ob.reshape(SB * R, W_B), om.reshape(SB * R, W_M), z,
                       w_out_b, final_g, SB * R, l == depth - 1)
        outs["ks"].append(dk.reshape(SB, R, H_B, E_B)[:, :T])
        outs["vs"].append(dv.reshape(SB, R, H_B, E_B)[:, :T])
        outs["ss"].append(ss)

    y_prompt = yp.reshape(B, L, D)
    y_sample = ys.reshape(SB, R, D)[:, :T]
    return (y_prompt, y_sample, jnp.stack(outs["kp"]), jnp.stack(outs["vp"]), jnp.stack(outs["sp"]),
            jnp.stack(outs["mkp"]), jnp.stack(outs["mvp"]), jnp.stack(outs["ks"]), jnp.stack(outs["vs"]),
            jnp.stack(outs["ss"]))
```

```python
import functools
import math

import jax
import jax.numpy as jnp
from jax import lax
from jax.experimental import pallas as pl
from jax.experimental.pallas import tpu as pltpu

F32 = jnp.float32
BF16 = jnp.bfloat16
EPS = 1e-6
NEG = -1e30
VMEM_LIMIT = 48 * 1024 * 1024
SAMPLE_ROWS = 8
ROW_TILE = 256

_NT = (((1,), (1,)), ((), ()))
_TN = (((0,), (0,)), ((), ()))


def _dot(a, b):
    return jnp.dot(a, b, preferred_element_type=F32)


def _dot_nt(a, b):
    return lax.dot_general(a, b, _NT, preferred_element_type=F32)


def _dot_tn(a, b):
    return lax.dot_general(a, b, _TN, preferred_element_type=F32)


def _rms(x, g):
    return x * lax.rsqrt(jnp.mean(x * x, axis=-1, keepdims=True) + EPS) * g


def _params(*sem):
    return pltpu.CompilerParams(dimension_semantics=sem, vmem_limit_bytes=VMEM_LIMIT)


def _split_halves(q, dh):
    lane = lax.broadcasted_iota(jnp.int32, q.shape, 1)
    zero = jnp.zeros_like(q)
    return jnp.concatenate([jnp.where(lane < dh, q, zero), jnp.where(lane < dh, zero, q)], axis=0)


def _proj_kernel(x_ref, g_ref, w_ref, *out_refs, col_sizes):
    h = _rms(x_ref[...], g_ref[...]).astype(BF16)
    start = 0
    for ref, size in zip(out_refs, col_sizes):
        ref[...] = _dot(h, w_ref[:, start:start + size]).astype(ref.dtype)
        start += size


def _norm_proj(x2d, g, w_bf16, col_sizes, tm, name):
    m, d = x2d.shape
    n = w_bf16.shape[1]
    assert sum(col_sizes) == n and m % tm == 0
    return pl.pallas_call(
        functools.partial(_proj_kernel, col_sizes=col_sizes),
        grid=(m // tm,),
        in_specs=[pl.BlockSpec((tm, d), lambda i: (i, 0)),
                  pl.BlockSpec((1, d), lambda i: (0, 0)),
                  pl.BlockSpec((d, n), lambda i: (0, 0))],
        out_specs=[pl.BlockSpec((tm, s), lambda i: (i, 0)) for s in col_sizes],
        out_shape=[jax.ShapeDtypeStruct((m, s), F32) for s in col_sizes],
        compiler_params=_params("parallel"),
        name=name,
    )(x2d, g.reshape(1, d), w_bf16)


def _in_proj_kernel(x_ref, g_ref, w_ref, *rest, sizes, H, prompt, scale):
    w_pa, w_b, w_m, mix = sizes
    if prompt:
        wvt_ref, pa_ref, dq_ref, kb_ref, vt_ref, ko_ref, vo_ref, mq_ref, z_ref = rest
    else:
        pa_ref, dq_ref, ko_ref, vo_ref, mq_ref, z_ref = rest
    e = w_b // H
    h = _rms(x_ref[...], g_ref[...]).astype(BF16)
    c = 0
    pa_ref[...] = _dot(h, w_ref[:, c:c + w_pa])
    c += w_pa
    dq_ref[...] = (_dot(h, w_ref[:, c:c + w_b]) * scale).astype(BF16)
    c += w_b
    k = _dot(h, w_ref[:, c:c + w_b])
    c += w_b
    v = _dot(h, w_ref[:, c:c + w_b])
    c += w_b
    tm = k.shape[0]
    for hh in range(H):
        ko_ref[pl.ds(hh, tm, stride=H), :] = k[:, hh * e:(hh + 1) * e]
        vo_ref[pl.ds(hh, tm, stride=H), :] = v[:, hh * e:(hh + 1) * e]
    if prompt:
        kb_ref[...] = k.astype(BF16)
        vt_ref[...] = _dot_nt(wvt_ref[...], h).astype(BF16)
    mq_ref[...] = _dot(h, w_ref[:, c:c + w_m])
    c += w_m
    z_ref[...] = _dot(h, w_ref[:, c:c + mix])


def _in_proj(x2d, g, w_bf16, wvt_bf16, sizes, H, tm, scale, name):
    m, d = x2d.shape
    n = w_bf16.shape[1]
    w_pa, w_b, w_m, mix = sizes
    e = w_b // H
    prompt = wvt_bf16 is not None
    assert w_pa + 3 * w_b + w_m + mix == n and m % tm == 0

    def rows(width):
        return pl.BlockSpec((tm, width), lambda i: (i, 0))

    const = lambda i: (0, 0)
    in_specs = [rows(d), pl.BlockSpec((1, d), const), pl.BlockSpec((d, n), const)]
    args = [x2d, g.reshape(1, d), w_bf16]
    out_specs = [rows(w_pa), rows(w_b)]
    out_shape = [jax.ShapeDtypeStruct((m, w_pa), F32), jax.ShapeDtypeStruct((m, w_b), BF16)]
    if prompt:
        in_specs.append(pl.BlockSpec((w_b, d), const))
        args.append(wvt_bf16)
        out_specs += [rows(w_b), pl.BlockSpec((None, w_b, tm), lambda i: (i, 0, 0))]
        out_shape += [jax.ShapeDtypeStruct((m, w_b), BF16), jax.ShapeDtypeStruct((m // tm, w_b, tm), BF16)]
    head_rows = pl.BlockSpec((tm * H, e), lambda i: (i, 0))
    out_specs += [head_rows, head_rows, rows(w_m), rows(mix)]
    out_shape += [jax.ShapeDtypeStruct((m * H, e), F32), jax.ShapeDtypeStruct((m * H, e), F32),
                  jax.ShapeDtypeStruct((m, w_m), F32), jax.ShapeDtypeStruct((m, mix), F32)]
    return pl.pallas_call(
        functools.partial(_in_proj_kernel, sizes=sizes, H=H, prompt=prompt, scale=scale),
        grid=(m // tm,),
        in_specs=in_specs,
        out_specs=out_specs,
        out_shape=out_shape,
        compiler_params=_params("parallel"),
        name=name,
    )(*args)


def _lower_bound(lg, layer, axis=0):
    e = jnp.exp(lg - jnp.max(lg, axis=axis, keepdims=True))
    p = e / jnp.sum(e, axis=axis, keepdims=True)
    head = p[:layer + 1] if axis == 0 else p[:, :layer + 1]
    return jnp.sum(head, axis=axis, keepdims=True)


def _hgrn_chunk_kernel(pa_ref, lg_ref, ng_ref, o_ref, sout_ref, s_scr, b_scr,
                       *, C, layer, H, DK, DV):
    c = pl.program_id(1)

    @pl.when(c == 0)
    def _():
        s_scr[...] = jnp.zeros_like(s_scr)

    W = H * DK
    lb = _lower_bound(lg_ref[...], layer)
    f = lb + (1.0 - lb) * jax.nn.sigmoid(pa_ref[:, W:2 * W])
    g = jnp.log(f)
    kk = 1.0 - f
    q = pa_ref[:, 0:W]

    ti = lax.broadcasted_iota(jnp.int32, (C, C), 0)
    si = lax.broadcasted_iota(jnp.int32, (C, C), 1)
    lmat = (si <= ti).astype(BF16)
    g_hi = g.astype(BF16)
    r1 = g - g_hi.astype(F32)
    g_mid = r1.astype(BF16)
    g_lo = (r1 - g_mid.astype(F32)).astype(BF16)
    b = _dot(lmat, g_hi) + _dot(lmat, g_mid) + _dot(lmat, g_lo)
    b_scr[...] = b
    bl = b_scr[C - 1:C, :]

    row = lax.broadcasted_iota(jnp.int32, (C, 1), 0)
    levels = []
    m = C // 2
    while m >= 1:
        upper = (row & (2 * m - 1)) >= m
        if m >= 4:
            pieces = [jnp.broadcast_to(b_scr[blk * 2 * m + m - 1:blk * 2 * m + m, :], (2 * m, W))
                      for blk in range(C // (2 * m))]
            bref = pieces[0] if len(pieces) == 1 else jnp.concatenate(pieces, axis=0)
            dq = b - bref
            dk = bref - b
        elif m == 2:
            dq = g + jnp.where((row & 3) == 3, pltpu.roll(g, 1, 0), 0.0)
            dk = jnp.where((row & 3) == 0, pltpu.roll(g, C - 1, 0), 0.0)
        else:
            dq = g
            dk = jnp.zeros_like(g)
        ql = jnp.where(upper, q * jnp.exp(dq), 0.0).astype(BF16)
        kl = jnp.where(upper, 0.0, kk * jnp.exp(dk)).astype(BF16)
        sh = (2 * m).bit_length() - 1
        same = None if 2 * m == C else (ti >> sh) == (si >> sh)
        levels.append((ql, kl, same))
        m //= 2
    eye = ti == si
    q_bf = q.astype(BF16)
    k_bf = kk.astype(BF16)
    q_in = (q * jnp.exp(b)).astype(BF16)
    k_out = (kk * jnp.exp(bl - b)).astype(BF16)
    ebl = jnp.exp(bl)

    for h in range(H):
        ks = slice(h * DK, (h + 1) * DK)
        vs = slice(h * DV, (h + 1) * DV)
        v = pa_ref[:, 2 * W + h * DV:2 * W + (h + 1) * DV].astype(BF16)
        a = jnp.where(eye, _dot_nt(q_bf[:, ks], k_bf[:, ks]), 0.0)
        for ql, kl, same in levels:
            r = _dot_nt(ql[:, ks], kl[:, ks])
            a = a + (r if same is None else jnp.where(same, r, 0.0))
        st = s_scr[h]
        o = _dot_nt(q_in[:, ks], st.astype(BF16)) + _dot(a.astype(BF16), v)
        o_ref[:, vs] = _rms(o, ng_ref[...])
        s_scr[h] = ebl[:, ks] * st + _dot_tn(v, k_out[:, ks])

    @pl.when(c == pl.num_programs(1) - 1)
    def _():
        sout_ref[...] = s_scr[...]


def _hgrn_prompt(pa, lg, ng, layer, H, DK, DV, C=128):
    B, L, wp = pa.shape
    W = H * DK
    assert L % C == 0 and wp == 2 * W + H * DV
    return pl.pallas_call(
        functools.partial(_hgrn_chunk_kernel, C=C, layer=layer, H=H, DK=DK, DV=DV),
        grid=(B, L // C),
        in_specs=[pl.BlockSpec((None, C, wp), lambda b, c: (b, c, 0)),
                  pl.BlockSpec(lg.shape, lambda b, c: (0, 0)),
                  pl.BlockSpec((1, DV), lambda b, c: (0, 0))],
        out_specs=[pl.BlockSpec((None, C, H * DV), lambda b, c: (b, c, 0)),
                   pl.BlockSpec((None, H, DV, DK), lambda b, c: (b, 0, 0, 0))],
        out_shape=[jax.ShapeDtypeStruct((B, L, H * DV), F32),
                   jax.ShapeDtypeStruct((B, H, DV, DK), F32)],
        scratch_shapes=[pltpu.VMEM((H, DV, DK), F32), pltpu.VMEM((C, W), F32)],
        compiler_params=_params("parallel", "arbitrary"),
        name="hgrn_prompt",
    )(pa, lg, ng.reshape(1, DV))


def _hgrn_step_kernel(qt_ref, ft_ref, v_ref, lg_ref, ng_ref, s0_ref, o_ref, sout_ref,
                      *, T, layer, H, DK, DV):
    lbt = _lower_bound(lg_ref[...], layer, axis=1)
    f_all = lbt + (1.0 - lbt) * jax.nn.sigmoid(ft_ref[...])
    q_all = qt_ref[...]
    o_ref[...] = jnp.zeros_like(o_ref)
    for h in range(H):
        ks = slice(h * DK, (h + 1) * DK)
        s = s0_ref[h]
        for t in range(T):
            f = f_all[ks, t:t + 1]
            v = v_ref[t:t + 1, h * DV:(h + 1) * DV]
            s = f * s + (1.0 - f) * v
            o = jnp.sum(s * q_all[ks, t:t + 1], axis=0, keepdims=True)
            o_ref[t:t + 1, h * DV:(h + 1) * DV] = _rms(o, ng_ref[...])
        sout_ref[h] = s


def _hgrn_sample(qt, ft, v, lgt, ng, s0, layer, T, H, DK, DV):
    B = qt.shape[0]
    W = H * DK
    rows = v.shape[1]
    return pl.pallas_call(
        functools.partial(_hgrn_step_kernel, T=T, layer=layer, H=H, DK=DK, DV=DV),
        grid=(B,),
        in_specs=[pl.BlockSpec((None, W, T), lambda b: (b, 0, 0)),
                  pl.BlockSpec((None, W, T), lambda b: (b, 0, 0)),
                  pl.BlockSpec((None, rows, H * DV), lambda b: (b, 0, 0)),
                  pl.BlockSpec(lgt.shape, lambda b: (0, 0)),
                  pl.BlockSpec((1, DV), lambda b: (0, 0)),
                  pl.BlockSpec((None, H, DK, DV), lambda b: (b, 0, 0, 0))],
        out_specs=[pl.BlockSpec((None, rows, H * DV), lambda b: (b, 0, 0)),
                   pl.BlockSpec((None, H, DK, DV), lambda b: (b, 0, 0, 0))],
        out_shape=[jax.ShapeDtypeStruct((B, rows, H * DV), F32),
                   jax.ShapeDtypeStruct((B, H, DK, DV), F32)],
        compiler_params=_params("parallel"),
        name="hgrn_sample",
    )(qt, ft, v, lgt, ng.reshape(1, DV), s0)


def _diff_prompt_kernel(lam_ref, q_ref, k_ref, vt_ref, g_ref, o_ref, m_scr, l_scr, acc_scr,
                        *, T, H, DH, slopes, out_scale):
    qi = pl.program_id(1)
    E = 2 * DH
    srel = lax.broadcasted_iota(jnp.int32, (T, 2 * T), 0)
    trel = lax.broadcasted_iota(jnp.int32, (T, 2 * T), 1) & (T - 1)
    causal = srel <= trel
    for h in range(H):
        hc = slice(h * E, (h + 1) * E)
        slope = slopes[h]
        qs = _split_halves(q_ref[:, hc], DH)
        bias = slope * srel.astype(F32)
        m_scr[...] = jnp.full_like(m_scr, NEG)
        l_scr[...] = jnp.zeros_like(l_scr)
        acc_scr[...] = jnp.zeros_like(acc_scr)

        def block(kj, masked):
            k = k_ref[pl.ds(pl.multiple_of(kj * T, T), T), hc]
            s = _dot_nt(k, qs) + bias
            if masked:
                s = jnp.where(causal, s, NEG)
            off = slope * ((kj - qi) * T).astype(F32)
            m_prev = m_scr[...]
            m_new = jnp.maximum(m_prev, jnp.max(s, axis=0, keepdims=True) + off)
            alpha = jnp.exp(m_prev - m_new)
            p = jnp.exp(s - (m_new - off))
            l_scr[...] = alpha * l_scr[...] + jnp.sum(p, axis=0, keepdims=True)
            acc_scr[...] = alpha * acc_scr[...] + _dot(vt_ref[kj, hc, :], p.astype(BF16))
            m_scr[...] = m_new

        def body(kj, carry):
            block(kj, False)
            return carry

        lax.fori_loop(0, qi, body, 0)
        block(qi, True)

        o2 = acc_scr[...] / l_scr[...]
        ot = o2[:, :T] - lam_ref[0] * o2[:, T:]
        ms = jnp.mean(ot * ot, axis=0, keepdims=True)
        ot = ot * lax.rsqrt(ms + EPS) * (g_ref[...] * out_scale)
        o_ref[:, hc] = ot.T


def _diff_prompt(q, kb, vt, lam, ng, H, DH, out_scale):
    B, L, W = q.shape
    T = vt.shape[-1]
    E = 2 * DH
    slopes = tuple(2.0 ** (-8.0 * (i + 1) / H) for i in range(H))
    assert L % T == 0 and T & (T - 1) == 0
    return pl.pallas_call(
        functools.partial(_diff_prompt_kernel, T=T, H=H, DH=DH, slopes=slopes, out_scale=out_scale),
        grid=(B, L // T),
        in_specs=[pl.BlockSpec(memory_space=pltpu.SMEM),
                  pl.BlockSpec((None, T, W), lambda b, i: (b, i, 0)),
                  pl.BlockSpec((None, L, W), lambda b, i: (b, 0, 0)),
                  pl.BlockSpec((None, L // T, W, T), lambda b, i: (b, 0, 0, 0)),
                  pl.BlockSpec((E, 1), lambda b, i: (0, 0))],
        out_specs=pl.BlockSpec((None, T, W), lambda b, i: (b, i, 0)),
        out_shape=jax.ShapeDtypeStruct((B, L, W), F32),
        scratch_shapes=[pltpu.VMEM((1, 2 * T), F32), pltpu.VMEM((1, 2 * T), F32),
                        pltpu.VMEM((E, 2 * T), F32)],
        compiler_params=_params("parallel", "arbitrary"),
        name="diff_prompt",
    )(lam, q, kb, vt, ng.reshape(E, 1))


def _diff_sample_kernel(pt_ref, lam_ref, q_ref, kn_ref, vn_ref, ng_ref, *rest,
                        PP, PAGE, T, H, DH, past, slopes, out_scale):
    k_refs = rest[:PP]
    v_refs = rest[PP:2 * PP]
    o_ref = rest[2 * PP]
    m_scr, l_scr, acc_scr = rest[2 * PP + 1:]
    step = pl.program_id(1)
    E = 2 * DH
    R = SAMPLE_ROWS
    NR = 2 * R

    @pl.when(step == 0)
    def _():
        m_scr[...] = jnp.full_like(m_scr, NEG)
        l_scr[...] = jnp.zeros_like(l_scr)
        acc_scr[...] = jnp.zeros_like(acc_scr)

    qs = [_split_halves(q_ref[:, h * E:(h + 1) * E], DH) for h in range(H)]
    head = lax.broadcasted_iota(jnp.int32, (H * NR, 1), 0) // NR
    slope = jnp.zeros((H * NR, 1), F32)
    for h in range(H):
        slope = jnp.where(head == h, slopes[h], slope)

    def update(k_tile, v_tile, n_tiles, kpos, ok):
        s = jnp.concatenate(
            [jnp.concatenate([_dot_nt(qs[h], k_tile(h, j)) for j in range(n_tiles)], axis=1)
             for h in range(H)], axis=0)
        s = s + slope * kpos
        if ok is not None:
            s = jnp.where(ok, s, NEG)
        m_prev = m_scr[...]
        m_new = jnp.maximum(m_prev, jnp.max(s, axis=-1, keepdims=True))
        alpha = jnp.exp(m_prev - m_new)
        p = jnp.exp(s - m_new)
        l_scr[...] = alpha * l_scr[...] + jnp.sum(p, axis=-1, keepdims=True)
        pb = p.astype(BF16)
        pv = []
        for h in range(H):
            out = _dot(pb[h * NR:(h + 1) * NR, 0:PAGE], v_tile(h, 0))
            for j in range(1, n_tiles):
                out = out + _dot(pb[h * NR:(h + 1) * NR, j * PAGE:(j + 1) * PAGE], v_tile(h, j))
            pv.append(out)
        acc_scr[...] = alpha * acc_scr[...] + jnp.concatenate(pv, axis=0)
        m_scr[...] = m_new

    col = lax.broadcasted_iota(jnp.int32, (1, PP * PAGE), 1)
    kpos = (step * (PP * PAGE) + col - past).astype(F32)
    update(lambda h, j: k_refs[j][pl.ds(h, PAGE, stride=H), :].astype(BF16),
           lambda h, j: v_refs[j][pl.ds(h, PAGE, stride=H), :].astype(BF16), PP, kpos, None)

    @pl.when(step == pl.num_programs(1) - 1)
    def _():
        tq = lax.broadcasted_iota(jnp.int32, (H * NR, PAGE), 0) & (R - 1)
        coln = lax.broadcasted_iota(jnp.int32, (H * NR, PAGE), 1)
        pad = jnp.zeros((PAGE - R, E), F32)
        update(lambda h, j: jnp.concatenate([kn_ref[pl.ds(h, R, stride=H), :], pad], axis=0).astype(BF16),
               lambda h, j: jnp.concatenate([vn_ref[pl.ds(h, R, stride=H), :], pad], axis=0).astype(BF16),
               1, coln[0:1].astype(F32), (coln <= tq) & (coln < T))
        o2 = acc_scr[...] / l_scr[...]
        for h in range(H):
            o = o2[h * NR:h * NR + R] - lam_ref[0] * o2[h * NR + R:(h + 1) * NR]
            o_ref[:, h * E:(h + 1) * E] = _rms(o, ng_ref[...]) * out_scale


def _diff_sample(q, kn, vn, cache_k, cache_v, page_ids, lam, ng, T, H, DH, out_scale, PP=8):
    B = q.shape[0]
    n_pages = page_ids.shape[1]
    PAGE = cache_k.shape[1] // H
    E = 2 * DH
    R = SAMPLE_ROWS
    slopes = tuple(2.0 ** (-8.0 * (i + 1) / H) for i in range(H))
    assert n_pages % PP == 0 and q.shape[1] == R
    new_rows = pl.BlockSpec((None, R * H, E), lambda b, s, pt: (b, 0, 0))

    def page_spec(j):
        return pl.BlockSpec((None, PAGE * H, E), lambda b, s, pt: (pt[b, s * PP + j], 0, 0))

    grid_spec = pltpu.PrefetchScalarGridSpec(
        num_scalar_prefetch=1,
        grid=(B, n_pages // PP),
        in_specs=[pl.BlockSpec(memory_space=pltpu.SMEM),
                  pl.BlockSpec((None, R, H * E), lambda b, s, pt: (b, 0, 0)),
                  new_rows, new_rows,
                  pl.BlockSpec((1, E), lambda b, s, pt: (0, 0))]
                 + [page_spec(j) for j in range(PP)] + [page_spec(j) for j in range(PP)],
        out_specs=pl.BlockSpec((None, R, H * E), lambda b, s, pt: (b, 0, 0)),
        scratch_shapes=[pltpu.VMEM((H * 2 * R, 1), F32), pltpu.VMEM((H * 2 * R, 1), F32),
                        pltpu.VMEM((H * 2 * R, E), F32)])
    return pl.pallas_call(
        functools.partial(_diff_sample_kernel, PP=PP, PAGE=PAGE, T=T, H=H, DH=DH,
                          past=n_pages * PAGE, slopes=slopes, out_scale=out_scale),
        grid_spec=grid_spec,
        out_shape=jax.ShapeDtypeStruct((B, R, H * E), F32),
        compiler_params=_params("parallel", "arbitrary"),
        name="diff_sample",
    )(page_ids, lam, q, kn, vn, ng.reshape(1, E), *([cache_k] * PP), *([cache_v] * PP))


def _mem_attn_kernel(q_ref, k_ref, v_ref, o_ref, *, H, DH, scale):
    q = q_ref[...] * scale
    T = q.shape[0]
    lane_h = lax.broadcasted_iota(jnp.int32, q.shape, 1) // DH
    qs = jnp.concatenate([jnp.where(lane_h == h, q, 0.0) for h in range(H)], axis=0).astype(BF16)
    s = _dot_nt(qs, k_ref[...].astype(BF16))
    p = jnp.exp(s - jnp.max(s, axis=-1, keepdims=True))
    o4 = _dot(p.astype(BF16), v_ref[...].astype(BF16)) / jnp.sum(p, axis=-1, keepdims=True)
    o = jnp.zeros_like(q)
    for h in range(H):
        o = jnp.where(lane_h == h, o4[h * T:(h + 1) * T], o)
    o_ref[...] = o


def _mem_attn(q, mk, mv, H, DH, T):
    B, L, W = q.shape
    N = mk.shape[1]
    assert L % T == 0
    return pl.pallas_call(
        functools.partial(_mem_attn_kernel, H=H, DH=DH, scale=DH ** -0.5),
        grid=(B, L // T),
        in_specs=[pl.BlockSpec((None, T, W), lambda b, i: (b, i, 0)),
                  pl.BlockSpec((None, N, W), lambda b, i: (b, 0, 0)),
                  pl.BlockSpec((None, N, W), lambda b, i: (b, 0, 0))],
        out_specs=pl.BlockSpec((None, T, W), lambda b, i: (b, i, 0)),
        out_shape=jax.ShapeDtypeStruct((B, L, W), F32),
        compiler_params=_params("parallel", "parallel"),
        name="mem_attn",
    )(q, mk, mv)


def _out_kernel(x_ref, oa_ref, ob_ref, om_ref, z_ref, w_ref, g_ref, y_ref, *, final):
    o = jnp.concatenate([oa_ref[...], ob_ref[...], om_ref[...]], axis=-1)
    z = z_ref[...]
    o = (o * (z * jax.nn.sigmoid(z))).astype(BF16)
    y = x_ref[...] + _dot(o, w_ref[...])
    y_ref[...] = _rms(y, g_ref[...]) if final else y


def _out_proj(x2d, oa, ob, om, z, w_bf16, g, tm, final):
    m, d = x2d.shape
    mix = w_bf16.shape[0]

    def rows(a):
        return pl.BlockSpec((tm, a.shape[1]), lambda i: (i, 0))

    return pl.pallas_call(
        functools.partial(_out_kernel, final=final),
        grid=(m // tm,),
        in_specs=[rows(x2d), rows(oa), rows(ob), rows(om), rows(z),
                  pl.BlockSpec((mix, d), lambda i: (0, 0)),
                  pl.BlockSpec((1, d), lambda i: (0, 0))],
        out_specs=rows(x2d),
        out_shape=jax.ShapeDtypeStruct((m, d), F32),
        compiler_params=_params("parallel"),
        name="out_proj",
    )(x2d, oa, ob, om, z, w_bf16, g.reshape(1, d))


def kernel(x_prompt, x_sample, mem_prompt, cache_k, cache_v, state_hgrn, cache_mem_k, cache_mem_v, page_table, norm_g, w_in, hgrn_lb_logits, hgrn_norm_g, diff_norm_g, lambda_q1, lambda_k1, lambda_q2, lambda_k2, mem_norm_g, w_mem_kv, w_out, final_g):
    B, L, D = x_prompt.shape
    SB, T, _ = x_sample.shape
    depth, _, H_A, DK_A, DV_A = state_hgrn.shape
    _, n_pool, PAGE, H_B, E_B = cache_k.shape
    DH_B = E_B // 2
    _, _, N_MEM, H_M, DH_M = cache_mem_k.shape
    W_A, W_K, W_B, W_M = H_A * DV_A, H_A * DK_A, H_B * E_B, H_M * DH_M
    MIX = W_A + W_B + W_M
    R = SAMPLE_ROWS
    TM = ROW_TILE
    assert T <= R and L % TM == 0
    sizes = (2 * W_K + W_A, W_B, W_M, MIX)
    v_col = 2 * W_K + W_A + 2 * W_B
    diff_scale = DH_B ** -0.5

    cache_k3 = cache_k.reshape(depth * n_pool, PAGE * H_B, E_B)
    cache_v3 = cache_v.reshape(depth * n_pool, PAGE * H_B, E_B)

    yp = x_prompt.reshape(B * L, D)
    ys = jnp.pad(x_sample, ((0, 0), (0, R - T), (0, 0))).reshape(SB * R, D)
    outs = {k: [] for k in ("kp", "vp", "sp", "mkp", "mvp", "ks", "vs", "ss")}
    for l in range(depth):
        lam_init = 0.8 - 0.6 * math.exp(-0.3 * l)
        lam = (jnp.exp(jnp.sum(lambda_q1[l] * lambda_k1[l])) - jnp.exp(jnp.sum(lambda_q2[l] * lambda_k2[l]))
               + lam_init).reshape(1).astype(F32)
        w_in_b = w_in[l].astype(BF16)
        w_vt_b = w_in[l][:, v_col:v_col + W_B].T.astype(BF16)
        w_out_b = w_out[l].astype(BF16)
        w_mem_b = w_mem_kv[l].astype(BF16)

        mk, mv = _norm_proj(mem_prompt.reshape(B * N_MEM, D), mem_norm_g[l], w_mem_b, (W_M, W_M), 256, "mem_kv")
        pa, dq, kb, vt, ko, vo, mq, z = _in_proj(yp, norm_g[l], w_in_b, w_vt_b, sizes, H_B, TM, diff_scale,
                                                 "in_proj_prompt")
        oa, st = _hgrn_prompt(pa.reshape(B, L, -1), hgrn_lb_logits, hgrn_norm_g[l], l, H_A, DK_A, DV_A)
        ob = _diff_prompt(dq.reshape(B, L, W_B), kb.reshape(B, L, W_B), vt.reshape(B, L // TM, W_B, TM),
                          lam, diff_norm_g[l], H_B, DH_B, 1.0 - lam_init)
        om = _mem_attn(mq.reshape(B, L, W_M), mk.reshape(B, N_MEM, W_M), mv.reshape(B, N_MEM, W_M),
                       H_M, DH_M, 256)
        yp = _out_proj(yp, oa.reshape(B * L, W_A), ob.reshape(B * L, W_B), om.reshape(B * L, W_M), z,
                       w_out_b, final_g, TM, l == depth - 1)
        outs["kp"].append(ko.reshape(B, L, H_B, E_B))
        outs["vp"].append(vo.reshape(B, L, H_B, E_B))
        outs["sp"].append(jnp.swapaxes(st, -1, -2))
        outs["mkp"].append(mk.reshape(B, N_MEM, H_M, DH_M))
        outs["mvp"].append(mv.reshape(B, N_MEM, H_M, DH_M))

        pa, dq, ko, vo, mq, z = _in_proj(ys, norm_g[l], w_in_b, None, sizes, H_B, SB * R, diff_scale,
                                         "in_proj_sample")
        pa3 = pa.reshape(SB, R, -1)
        qt = jnp.swapaxes(pa3[:, :T, 0:W_K], 1, 2)
        ft = jnp.swapaxes(pa3[:, :T, W_K:2 * W_K], 1, 2)
        oa, ss = _hgrn_sample(qt, ft, pa3[:, :, 2 * W_K:], hgrn_lb_logits.T, hgrn_norm_g[l], state_hgrn[l],
                              l, T, H_A, DK_A, DV_A)
        ob = _diff_sample(dq.reshape(SB, R, W_B), ko.reshape(SB, R * H_B, E_B), vo.reshape(SB, R * H_B, E_B),
                          cache_k3, cache_v3, page_table + l * n_pool,
                          lam, diff_norm_g[l], T, H_B, DH_B, 1.0 - lam_init)
        om = _mem_attn(mq.reshape(SB, R, W_M), cache_mem_k[l].reshape(SB, N_MEM, W_M),
                       cache_mem_v[l].reshape(SB, N_MEM, W_M), H_M, DH_M, R)
        ys = _out_proj(ys, oa.reshape(SB * R, W_A), ob.reshape(SB * R, W_B), om.reshape(SB * R, W_M), z,
                       w_out_b, final_g, SB * R, l == depth - 1)
        outs["ks"].append(ko.reshape(SB, R, H_B, E_B)[:, :T])
        outs["vs"].append(vo.reshape(SB, R, H_B, E_B)[:, :T])
        outs["ss"].append(ss)

    y_prompt = yp.reshape(B, L, D)
    y_sample = ys.reshape(SB, R, D)[:, :T]
    return (y_prompt, y_sample, jnp.stack(outs["kp"]), jnp.stack(outs["vp"]), jnp.stack(outs["sp"]),
            jnp.stack(outs["mkp"]), jnp.stack(outs["mvp"]), jnp.stack(outs["ks"]), jnp.stack(outs["vs"]),
            jnp.stack(outs["ss"]))
```

```python
import functools
import math

import jax
import jax.numpy as jnp
from jax import lax
from jax.experimental import pallas as pl
from jax.experimental.pallas import tpu as pltpu

F32 = jnp.float32
BF16 = jnp.bfloat16
EPS = 1e-6
NEG = -1e30
VMEM_LIMIT = 48 * 1024 * 1024
SAMPLE_ROWS = 8
ROW_TILE = 256

_NT = (((1,), (1,)), ((), ()))
_TN = (((0,), (0,)), ((), ()))


def _dot(a, b):
    return jnp.dot(a, b, preferred_element_type=F32)


def _dot_nt(a, b):
    return lax.dot_general(a, b, _NT, preferred_element_type=F32)


def _dot_tn(a, b):
    return lax.dot_general(a, b, _TN, preferred_element_type=F32)


def _rms(x, g):
    return x * lax.rsqrt(jnp.mean(x * x, axis=-1, keepdims=True) + EPS) * g


def _params(*sem):
    return pltpu.CompilerParams(dimension_semantics=sem, vmem_limit_bytes=VMEM_LIMIT)


def _split_halves(q, dh):
    lane = lax.broadcasted_iota(jnp.int32, q.shape, 1)
    zero = jnp.zeros_like(q)
    return jnp.concatenate([jnp.where(lane < dh, q, zero), jnp.where(lane < dh, zero, q)], axis=0)


def _proj_kernel(x_ref, g_ref, w_ref, *out_refs, col_sizes):
    h = _rms(x_ref[...], g_ref[...]).astype(BF16)
    start = 0
    for ref, size in zip(out_refs, col_sizes):
        ref[...] = _dot(h, w_ref[:, start:start + size]).astype(ref.dtype)
        start += size


def _norm_proj(x2d, g, w_bf16, col_sizes, tm, name):
    m, d = x2d.shape
    n = w_bf16.shape[1]
    assert sum(col_sizes) == n and m % tm == 0
    return pl.pallas_call(
        functools.partial(_proj_kernel, col_sizes=col_sizes),
        grid=(m // tm,),
        in_specs=[pl.BlockSpec((tm, d), lambda i: (i, 0)),
                  pl.BlockSpec((1, d), lambda i: (0, 0)),
                  pl.BlockSpec((d, n), lambda i: (0, 0))],
        out_specs=[pl.BlockSpec((tm, s), lambda i: (i, 0)) for s in col_sizes],
        out_shape=[jax.ShapeDtypeStruct((m, s), F32) for s in col_sizes],
        compiler_params=_params("parallel"),
        name=name,
    )(x2d, g.reshape(1, d), w_bf16)


def _in_proj_kernel(x_ref, g_ref, w_ref, *rest, sizes, H, prompt, scale):
    w_pa, w_b, w_m, mix = sizes
    if prompt:
        pa_ref, dq_ref, kb_ref, vb_ref, ko_ref, vo_ref, mq_ref, z_ref = rest
    else:
        pa_ref, dq_ref, ko_ref, vo_ref, mq_ref, z_ref = rest
    e = w_b // H
    h = _rms(x_ref[...], g_ref[...]).astype(BF16)
    c = 0
    pa_ref[...] = _dot(h, w_ref[:, c:c + w_pa])
    c += w_pa
    dq_ref[...] = (_dot(h, w_ref[:, c:c + w_b]) * scale).astype(BF16)
    c += w_b
    k = _dot(h, w_ref[:, c:c + w_b])
    c += w_b
    v = _dot(h, w_ref[:, c:c + w_b])
    c += w_b
    tm = k.shape[0]
    for hh in range(H):
        ko_ref[pl.ds(hh, tm, stride=H), :] = k[:, hh * e:(hh + 1) * e]
        vo_ref[pl.ds(hh, tm, stride=H), :] = v[:, hh * e:(hh + 1) * e]
    if prompt:
        kb_ref[...] = k.astype(BF16)
        vb_ref[...] = v.astype(BF16)
    mq_ref[...] = _dot(h, w_ref[:, c:c + w_m])
    c += w_m
    z_ref[...] = _dot(h, w_ref[:, c:c + mix])


def _in_proj(x2d, g, w_bf16, sizes, H, tm, scale, prompt, name):
    m, d = x2d.shape
    n = w_bf16.shape[1]
    w_pa, w_b, w_m, mix = sizes
    e = w_b // H
    assert w_pa + 3 * w_b + w_m + mix == n and m % tm == 0

    def rows(width):
        return pl.BlockSpec((tm, width), lambda i: (i, 0))

    const = lambda i: (0, 0)
    in_specs = [rows(d), pl.BlockSpec((1, d), const), pl.BlockSpec((d, n), const)]
    args = [x2d, g.reshape(1, d), w_bf16]
    out_specs = [rows(w_pa), rows(w_b)]
    out_shape = [jax.ShapeDtypeStruct((m, w_pa), F32), jax.ShapeDtypeStruct((m, w_b), BF16)]
    if prompt:
        out_specs += [rows(w_b), rows(w_b)]
        out_shape += [jax.ShapeDtypeStruct((m, w_b), BF16), jax.ShapeDtypeStruct((m, w_b), BF16)]
    head_rows = pl.BlockSpec((tm * H, e), lambda i: (i, 0))
    out_specs += [head_rows, head_rows, rows(w_m), rows(mix)]
    out_shape += [jax.ShapeDtypeStruct((m * H, e), F32), jax.ShapeDtypeStruct((m * H, e), F32),
                  jax.ShapeDtypeStruct((m, w_m), F32), jax.ShapeDtypeStruct((m, mix), F32)]
    return pl.pallas_call(
        functools.partial(_in_proj_kernel, sizes=sizes, H=H, prompt=prompt, scale=scale),
        grid=(m // tm,),
        in_specs=in_specs,
        out_specs=out_specs,
        out_shape=out_shape,
        compiler_params=_params("parallel"),
        name=name,
    )(*args)


def _lower_bound(lg, layer, axis=0):
    e = jnp.exp(lg - jnp.max(lg, axis=axis, keepdims=True))
    p = e / jnp.sum(e, axis=axis, keepdims=True)
    head = p[:layer + 1] if axis == 0 else p[:, :layer + 1]
    return jnp.sum(head, axis=axis, keepdims=True)


def _hgrn_chunk_kernel(pa_ref, lg_ref, ng_ref, o_ref, sout_ref, s_scr, b_scr,
                       *, C, layer, H, DK, DV):
    c = pl.program_id(1)

    @pl.when(c == 0)
    def _():
        s_scr[...] = jnp.zeros_like(s_scr)

    W = H * DK
    lb = _lower_bound(lg_ref[...], layer)
    f = lb + (1.0 - lb) * jax.nn.sigmoid(pa_ref[:, W:2 * W])
    g = jnp.log(f)
    kk = 1.0 - f
    q = pa_ref[:, 0:W]

    ti = lax.broadcasted_iota(jnp.int32, (C, C), 0)
    si = lax.broadcasted_iota(jnp.int32, (C, C), 1)
    lmat = (si <= ti).astype(BF16)
    g_hi = g.astype(BF16)
    r1 = g - g_hi.astype(F32)
    g_mid = r1.astype(BF16)
    g_lo = (r1 - g_mid.astype(F32)).astype(BF16)
    b = _dot(lmat, g_hi) + _dot(lmat, g_mid) + _dot(lmat, g_lo)
    b_scr[...] = b
    bl = b_scr[C - 1:C, :]

    row = lax.broadcasted_iota(jnp.int32, (C, 1), 0)
    levels = []
    m = C // 2
    while m >= 1:
        upper = (row & (2 * m - 1)) >= m
        if m >= 4:
            pieces = [jnp.broadcast_to(b_scr[blk * 2 * m + m - 1:blk * 2 * m + m, :], (2 * m, W))
                      for blk in range(C // (2 * m))]
            bref = pieces[0] if len(pieces) == 1 else jnp.concatenate(pieces, axis=0)
            dq = b - bref
            dk = bref - b
        elif m == 2:
            dq = g + jnp.where((row & 3) == 3, pltpu.roll(g, 1, 0), 0.0)
            dk = jnp.where((row & 3) == 0, pltpu.roll(g, C - 1, 0), 0.0)
        else:
            dq = g
            dk = jnp.zeros_like(g)
        ql = jnp.where(upper, q * jnp.exp(dq), 0.0).astype(BF16)
        kl = jnp.where(upper, 0.0, kk * jnp.exp(dk)).astype(BF16)
        sh = (2 * m).bit_length() - 1
        same = None if 2 * m == C else (ti >> sh) == (si >> sh)
        levels.append((ql, kl, same))
        m //= 2
    eye = ti == si
    q_bf = q.astype(BF16)
    k_bf = kk.astype(BF16)
    q_in = (q * jnp.exp(b)).astype(BF16)
    k_out = (kk * jnp.exp(bl - b)).astype(BF16)
    ebl = jnp.exp(bl)

    for h in range(H):
        ks = slice(h * DK, (h + 1) * DK)
        vs = slice(h * DV, (h + 1) * DV)
        v = pa_ref[:, 2 * W + h * DV:2 * W + (h + 1) * DV].astype(BF16)
        a = jnp.where(eye, _dot_nt(q_bf[:, ks], k_bf[:, ks]), 0.0)
        for ql, kl, same in levels:
            r = _dot_nt(ql[:, ks], kl[:, ks])
            a = a + (r if same is None else jnp.where(same, r, 0.0))
        st = s_scr[h]
        o = _dot_nt(q_in[:, ks], st.astype(BF16)) + _dot(a.astype(BF16), v)
        o_ref[:, vs] = _rms(o, ng_ref[...])
        s_scr[h] = ebl[:, ks] * st + _dot_tn(v, k_out[:, ks])

    @pl.when(c == pl.num_programs(1) - 1)
    def _():
        sout_ref[...] = s_scr[...]


def _hgrn_prompt(pa, lg, ng, layer, H, DK, DV, C=128):
    B, L, wp = pa.shape
    W = H * DK
    assert L % C == 0 and wp == 2 * W + H * DV
    return pl.pallas_call(
        functools.partial(_hgrn_chunk_kernel, C=C, layer=layer, H=H, DK=DK, DV=DV),
        grid=(B, L // C),
        in_specs=[pl.BlockSpec((None, C, wp), lambda b, c: (b, c, 0)),
                  pl.BlockSpec(lg.shape, lambda b, c: (0, 0)),
                  pl.BlockSpec((1, DV), lambda b, c: (0, 0))],
        out_specs=[pl.BlockSpec((None, C, H * DV), lambda b, c: (b, c, 0)),
                   pl.BlockSpec((None, H, DV, DK), lambda b, c: (b, 0, 0, 0))],
        out_shape=[jax.ShapeDtypeStruct((B, L, H * DV), F32),
                   jax.ShapeDtypeStruct((B, H, DV, DK), F32)],
        scratch_shapes=[pltpu.VMEM((H, DV, DK), F32), pltpu.VMEM((C, W), F32)],
        compiler_params=_params("parallel", "arbitrary"),
        name="hgrn_prompt",
    )(pa, lg, ng.reshape(1, DV))


def _hgrn_step_kernel(qt_ref, ft_ref, v_ref, lg_ref, ng_ref, s0_ref, o_ref, sout_ref,
                      *, T, layer, H, DK, DV):
    lbt = _lower_bound(lg_ref[...], layer, axis=1)
    f_all = lbt + (1.0 - lbt) * jax.nn.sigmoid(ft_ref[...])
    q_all = qt_ref[...]
    o_ref[...] = jnp.zeros_like(o_ref)
    for h in range(H):
        ks = slice(h * DK, (h + 1) * DK)
        s = s0_ref[h]
        for t in range(T):
            f = f_all[ks, t:t + 1]
            v = v_ref[t:t + 1, h * DV:(h + 1) * DV]
            s = f * s + (1.0 - f) * v
            o = jnp.sum(s * q_all[ks, t:t + 1], axis=0, keepdims=True)
            o_ref[t:t + 1, h * DV:(h + 1) * DV] = _rms(o, ng_ref[...])
        sout_ref[h] = s


def _hgrn_sample(qt, ft, v, lgt, ng, s0, layer, T, H, DK, DV):
    B = qt.shape[0]
    W = H * DK
    rows = v.shape[1]
    return pl.pallas_call(
        functools.partial(_hgrn_step_kernel, T=T, layer=layer, H=H, DK=DK, DV=DV),
        grid=(B,),
        in_specs=[pl.BlockSpec((None, W, T), lambda b: (b, 0, 0)),
                  pl.BlockSpec((None, W, T), lambda b: (b, 0, 0)),
                  pl.BlockSpec((None, rows, H * DV), lambda b: (b, 0, 0)),
                  pl.BlockSpec(lgt.shape, lambda b: (0, 0)),
                  pl.BlockSpec((1, DV), lambda b: (0, 0)),
                  pl.BlockSpec((None, H, DK, DV), lambda b: (b, 0, 0, 0))],
        out_specs=[pl.BlockSpec((None, rows, H * DV), lambda b: (b, 0, 0)),
                   pl.BlockSpec((None, H, DK, DV), lambda b: (b, 0, 0, 0))],
        out_shape=[jax.ShapeDtypeStruct((B, rows, H * DV), F32),
                   jax.ShapeDtypeStruct((B, H, DK, DV), F32)],
        compiler_params=_params("parallel"),
        name="hgrn_sample",
    )(qt, ft, v, lgt, ng.reshape(1, DV), s0)


def _diff_prompt_kernel(slope_ref, lam_ref, q_ref, k_ref, v_ref, g_ref, o_ref, m_scr, l_scr, acc_scr,
                        *, TQ, DH, out_scale):
    h = pl.program_id(1)
    qi = pl.program_id(2)
    LANES = m_scr.shape[1]
    slope = slope_ref[h]
    qs = _split_halves(q_ref[...], DH)
    trel = lax.broadcasted_iota(jnp.int32, (2 * TQ, TQ), 0) & (TQ - 1)
    srel = lax.broadcasted_iota(jnp.int32, (2 * TQ, TQ), 1)
    key_bias = slope * lax.broadcasted_iota(jnp.int32, (1, TQ), 1).astype(F32)

    m_scr[...] = jnp.full_like(m_scr, NEG)
    l_scr[...] = jnp.zeros_like(l_scr)
    acc_scr[...] = jnp.zeros_like(acc_scr)

    def block(kj, masked):
        rows = pl.ds(pl.multiple_of(kj * TQ, TQ), TQ)
        s = _dot_nt(qs, k_ref[rows, :])
        s = s + (key_bias + slope * ((kj - qi) * TQ).astype(F32))
        if masked:
            s = jnp.where(srel <= trel, s, NEG)
        m_prev = m_scr[...]
        m_new = jnp.maximum(m_prev, jnp.max(s, axis=1, keepdims=True))
        alpha = jnp.exp(m_prev - m_new)
        p = jnp.exp(s - jnp.tile(m_new, (1, TQ // LANES)))
        l_scr[...] = alpha * l_scr[...] + jnp.sum(p, axis=1, keepdims=True)
        acc_scr[...] = alpha * acc_scr[...] + _dot(p.astype(BF16), v_ref[rows, :])
        m_scr[...] = m_new

    def body(kj, carry):
        block(kj, False)
        return carry

    lax.fori_loop(0, qi, body, 0)
    block(qi, True)

    o2 = acc_scr[...] / l_scr[...]
    o = o2[:TQ] - lam_ref[0] * o2[TQ:]
    o_ref[...] = _rms(o, g_ref[...]) * out_scale


def _diff_prompt(q, kb, vb, lam, ng, H, DH, out_scale, TQ=512):
    B, L, W = q.shape
    E = 2 * DH
    slopes = jnp.asarray([2.0 ** (-8.0 * (i + 1) / H) for i in range(H)], F32)
    assert L % TQ == 0 and TQ & (TQ - 1) == 0
    return pl.pallas_call(
        functools.partial(_diff_prompt_kernel, TQ=TQ, DH=DH, out_scale=out_scale),
        grid=(B, H, L // TQ),
        in_specs=[pl.BlockSpec(memory_space=pltpu.SMEM),
                  pl.BlockSpec(memory_space=pltpu.SMEM),
                  pl.BlockSpec((None, TQ, E), lambda b, h, i: (b, i, h)),
                  pl.BlockSpec((None, L, E), lambda b, h, i: (b, 0, h)),
                  pl.BlockSpec((None, L, E), lambda b, h, i: (b, 0, h)),
                  pl.BlockSpec((1, E), lambda b, h, i: (0, 0))],
        out_specs=pl.BlockSpec((None, TQ, E), lambda b, h, i: (b, i, h)),
        out_shape=jax.ShapeDtypeStruct((B, L, W), F32),
        scratch_shapes=[pltpu.VMEM((2 * TQ, E), F32), pltpu.VMEM((2 * TQ, E), F32),
                        pltpu.VMEM((2 * TQ, E), F32)],
        compiler_params=_params("parallel", "parallel", "arbitrary"),
        name="diff_prompt",
    )(slopes, lam, q, kb, vb, ng.reshape(1, E))


def _diff_sample_kernel(pt_ref, lam_ref, q_ref, kn_ref, vn_ref, ng_ref, *rest,
                        PP, PAGE, T, H, DH, past, slopes, out_scale):
    k_refs = rest[:PP]
    v_refs = rest[PP:2 * PP]
    o_ref = rest[2 * PP]
    m_scr, l_scr, acc_scr = rest[2 * PP + 1:]
    step = pl.program_id(1)
    E = 2 * DH
    R = SAMPLE_ROWS
    NR = 2 * R

    @pl.when(step == 0)
    def _():
        m_scr[...] = jnp.full_like(m_scr, NEG)
        l_scr[...] = jnp.zeros_like(l_scr)
        acc_scr[...] = jnp.zeros_like(acc_scr)

    qs = [_split_halves(q_ref[:, h * E:(h + 1) * E], DH) for h in range(H)]
    head = lax.broadcasted_iota(jnp.int32, (H * NR, 1), 0) // NR
    slope = jnp.zeros((H * NR, 1), F32)
    for h in range(H):
        slope = jnp.where(head == h, slopes[h], slope)

    def update(k_tile, v_tile, n_tiles, kpos, ok):
        s = jnp.concatenate(
            [jnp.concatenate([_dot_nt(qs[h], k_tile(h, j)) for j in range(n_tiles)], axis=1)
             for h in range(H)], axis=0)
        s = s + slope * kpos
        if ok is not None:
            s = jnp.where(ok, s, NEG)
        m_prev = m_scr[...]
        m_new = jnp.maximum(m_prev, jnp.max(s, axis=-1, keepdims=True))
        alpha = jnp.exp(m_prev - m_new)
        p = jnp.exp(s - m_new)
        l_scr[...] = alpha * l_scr[...] + jnp.sum(p, axis=-1, keepdims=True)
        pb = p.astype(BF16)
        pv = []
        for h in range(H):
            out = _dot(pb[h * NR:(h + 1) * NR, 0:PAGE], v_tile(h, 0))
            for j in range(1, n_tiles):
                out = out + _dot(pb[h * NR:(h + 1) * NR, j * PAGE:(j + 1) * PAGE], v_tile(h, j))
            pv.append(out)
        acc_scr[...] = alpha * acc_scr[...] + jnp.concatenate(pv, axis=0)
        m_scr[...] = m_new

    col = lax.broadcasted_iota(jnp.int32, (1, PP * PAGE), 1)
    kpos = (step * (PP * PAGE) + col - past).astype(F32)
    update(lambda h, j: k_refs[j][pl.ds(h, PAGE, stride=H), :].astype(BF16),
           lambda h, j: v_refs[j][pl.ds(h, PAGE, stride=H), :].astype(BF16), PP, kpos, None)

    @pl.when(step == pl.num_programs(1) - 1)
    def _():
        tq = lax.broadcasted_iota(jnp.int32, (H * NR, PAGE), 0) & (R - 1)
        coln = lax.broadcasted_iota(jnp.int32, (H * NR, PAGE), 1)
        pad = jnp.zeros((PAGE - R, E), F32)
        update(lambda h, j: jnp.concatenate([kn_ref[pl.ds(h, R, stride=H), :], pad], axis=0).astype(BF16),
               lambda h, j: jnp.concatenate([vn_ref[pl.ds(h, R, stride=H), :], pad], axis=0).astype(BF16),
               1, coln[0:1].astype(F32), (coln <= tq) & (coln < T))
        o2 = acc_scr[...] / l_scr[...]
        for h in range(H):
            o = o2[h * NR:h * NR + R] - lam_ref[0] * o2[h * NR + R:(h + 1) * NR]
            o_ref[:, h * E:(h + 1) * E] = _rms(o, ng_ref[...]) * out_scale


def _diff_sample(q, kn, vn, cache_k, cache_v, page_ids, lam, ng, T, H, DH, out_scale, PP=16):
    B = q.shape[0]
    n_pages = page_ids.shape[1]
    PAGE = cache_k.shape[1] // H
    E = 2 * DH
    R = SAMPLE_ROWS
    slopes = tuple(2.0 ** (-8.0 * (i + 1) / H) for i in range(H))
    assert n_pages % PP == 0 and q.shape[1] == R
    new_rows = pl.BlockSpec((None, R * H, E), lambda b, s, pt: (b, 0, 0))

    def page_spec(j):
        return pl.BlockSpec((None, PAGE * H, E), lambda b, s, pt: (pt[b, s * PP + j], 0, 0))

    grid_spec = pltpu.PrefetchScalarGridSpec(
        num_scalar_prefetch=1,
        grid=(B, n_pages // PP),
        in_specs=[pl.BlockSpec(memory_space=pltpu.SMEM),
                  pl.BlockSpec((None, R, H * E), lambda b, s, pt: (b, 0, 0)),
                  new_rows, new_rows,
                  pl.BlockSpec((1, E), lambda b, s, pt: (0, 0))]
                 + [page_spec(j) for j in range(PP)] + [page_spec(j) for j in range(PP)],
        out_specs=pl.BlockSpec((None, R, H * E), lambda b, s, pt: (b, 0, 0)),
        scratch_shapes=[pltpu.VMEM((H * 2 * R, 1), F32), pltpu.VMEM((H * 2 * R, 1), F32),
                        pltpu.VMEM((H * 2 * R, E), F32)])
    return pl.pallas_call(
        functools.partial(_diff_sample_kernel, PP=PP, PAGE=PAGE, T=T, H=H, DH=DH,
                          past=n_pages * PAGE, slopes=slopes, out_scale=out_scale),
        grid_spec=grid_spec,
        out_shape=jax.ShapeDtypeStruct((B, R, H * E), F32),
        compiler_params=_params("parallel", "arbitrary"),
        name="diff_sample",
    )(page_ids, lam, q, kn, vn, ng.reshape(1, E), *([cache_k] * PP), *([cache_v] * PP))


def _mem_attn_kernel(q_ref, k_ref, v_ref, o_ref, *, H, DH, scale):
    q = q_ref[...] * scale
    T = q.shape[0]
    lane_h = lax.broadcasted_iota(jnp.int32, q.shape, 1) // DH
    qs = jnp.concatenate([jnp.where(lane_h == h, q, 0.0) for h in range(H)], axis=0).astype(BF16)
    s = _dot_nt(qs, k_ref[...].astype(BF16))
    p = jnp.exp(s - jnp.max(s, axis=-1, keepdims=True))
    o4 = _dot(p.astype(BF16), v_ref[...].astype(BF16)) / jnp.sum(p, axis=-1, keepdims=True)
    o = jnp.zeros_like(q)
    for h in range(H):
        o = jnp.where(lane_h == h, o4[h * T:(h + 1) * T], o)
    o_ref[...] = o


def _mem_attn(q, mk, mv, H, DH, T):
    B, L, W = q.shape
    N = mk.shape[1]
    assert L % T == 0
    return pl.pallas_call(
        functools.partial(_mem_attn_kernel, H=H, DH=DH, scale=DH ** -0.5),
        grid=(B, L // T),
        in_specs=[pl.BlockSpec((None, T, W), lambda b, i: (b, i, 0)),
                  pl.BlockSpec((None, N, W), lambda b, i: (b, 0, 0)),
                  pl.BlockSpec((None, N, W), lambda b, i: (b, 0, 0))],
        out_specs=pl.BlockSpec((None, T, W), lambda b, i: (b, i, 0)),
        out_shape=jax.ShapeDtypeStruct((B, L, W), F32),
        compiler_params=_params("parallel", "parallel"),
        name="mem_attn",
    )(q, mk, mv)


def _out_kernel(x_ref, oa_ref, ob_ref, om_ref, z_ref, w_ref, g_ref, y_ref, *, final):
    o = jnp.concatenate([oa_ref[...], ob_ref[...], om_ref[...]], axis=-1)
    z = z_ref[...]
    o = (o * (z * jax.nn.sigmoid(z))).astype(BF16)
    y = x_ref[...] + _dot(o, w_ref[...])
    y_ref[...] = _rms(y, g_ref[...]) if final else y


def _out_proj(x2d, oa, ob, om, z, w_bf16, g, tm, final):
    m, d = x2d.shape
    mix = w_bf16.shape[0]

    def rows(a):
        return pl.BlockSpec((tm, a.shape[1]), lambda i: (i, 0))

    return pl.pallas_call(
        functools.partial(_out_kernel, final=final),
        grid=(m // tm,),
        in_specs=[rows(x2d), rows(oa), rows(ob), rows(om), rows(z),
                  pl.BlockSpec((mix, d), lambda i: (0, 0)),
                  pl.BlockSpec((1, d), lambda i: (0, 0))],
        out_specs=rows(x2d),
        out_shape=jax.ShapeDtypeStruct((m, d), F32),
        compiler_params=_params("parallel"),
        name="out_proj",
    )(x2d, oa, ob, om, z, w_bf16, g.reshape(1, d))


def kernel(x_prompt, x_sample, mem_prompt, cache_k, cache_v, state_hgrn, cache_mem_k, cache_mem_v, page_table, norm_g, w_in, hgrn_lb_logits, hgrn_norm_g, diff_norm_g, lambda_q1, lambda_k1, lambda_q2, lambda_k2, mem_norm_g, w_mem_kv, w_out, final_g):
    B, L, D = x_prompt.shape
    SB, T, _ = x_sample.shape
    depth, _, H_A, DK_A, DV_A = state_hgrn.shape
    _, n_pool, PAGE, H_B, E_B = cache_k.shape
    DH_B = E_B // 2
    _, _, N_MEM, H_M, DH_M = cache_mem_k.shape
    W_A, W_K, W_B, W_M = H_A * DV_A, H_A * DK_A, H_B * E_B, H_M * DH_M
    MIX = W_A + W_B + W_M
    R = SAMPLE_ROWS
    TM = ROW_TILE
    assert T <= R and L % TM == 0
    sizes = (2 * W_K + W_A, W_B, W_M, MIX)
    diff_scale = DH_B ** -0.5

    cache_k3 = cache_k.reshape(depth * n_pool, PAGE * H_B, E_B)
    cache_v3 = cache_v.reshape(depth * n_pool, PAGE * H_B, E_B)

    yp = x_prompt.reshape(B * L, D)
    ys = jnp.pad(x_sample, ((0, 0), (0, R - T), (0, 0))).reshape(SB * R, D)
    outs = {k: [] for k in ("kp", "vp", "sp", "mkp", "mvp", "ks", "vs", "ss")}
    for l in range(depth):
        lam_init = 0.8 - 0.6 * math.exp(-0.3 * l)
        lam = (jnp.exp(jnp.sum(lambda_q1[l] * lambda_k1[l])) - jnp.exp(jnp.sum(lambda_q2[l] * lambda_k2[l]))
               + lam_init).reshape(1).astype(F32)
        w_in_b = w_in[l].astype(BF16)
        w_out_b = w_out[l].astype(BF16)
        w_mem_b = w_mem_kv[l].astype(BF16)

        mk, mv = _norm_proj(mem_prompt.reshape(B * N_MEM, D), mem_norm_g[l], w_mem_b, (W_M, W_M), 256, "mem_kv")
        pa, dq, kb, vb, ko, vo, mq, z = _in_proj(yp, norm_g[l], w_in_b, sizes, H_B, TM, diff_scale, True,
                                                 "in_proj_prompt")
        oa, st = _hgrn_prompt(pa.reshape(B, L, -1), hgrn_lb_logits, hgrn_norm_g[l], l, H_A, DK_A, DV_A)
        ob = _diff_prompt(dq.reshape(B, L, W_B), kb.reshape(B, L, W_B), vb.reshape(B, L, W_B),
                          lam, diff_norm_g[l], H_B, DH_B, 1.0 - lam_init)
        om = _mem_attn(mq.reshape(B, L, W_M), mk.reshape(B, N_MEM, W_M), mv.reshape(B, N_MEM, W_M),
                       H_M, DH_M, 256)
        yp = _out_proj(yp, oa.reshape(B * L, W_A), ob.reshape(B * L, W_B), om.reshape(B * L, W_M), z,
                       w_out_b, final_g, TM, l == depth - 1)
        outs["kp"].append(ko.reshape(B, L, H_B, E_B))
        outs["vp"].append(vo.reshape(B, L, H_B, E_B))
        outs["sp"].append(jnp.swapaxes(st, -1, -2))
        outs["mkp"].append(mk.reshape(B, N_MEM, H_M, DH_M))
        outs["mvp"].append(mv.reshape(B, N_MEM, H_M, DH_M))

        pa, dq, ko, vo, mq, z = _in_proj(ys, norm_g[l], w_in_b, sizes, H_B, SB * R, diff_scale, False,
                                         "in_proj_sample")
        pa3 = pa.reshape(SB, R, -1)
        qt = jnp.swapaxes(pa3[:, :T, 0:W_K], 1, 2)
        ft = jnp.swapaxes(pa3[:, :T, W_K:2 * W_K], 1, 2)
        oa, ss = _hgrn_sample(qt, ft, pa3[:, :, 2 * W_K:], hgrn_lb_logits.T, hgrn_norm_g[l], state_hgrn[l],
                              l, T, H_A, DK_A, DV_A)
        ob = _diff_sample(dq.reshape(SB, R, W_B), ko.reshape(SB, R * H_B, E_B), vo.reshape(SB, R * H_B, E_B),
                          cache_k3, cache_v3, page_table + l * n_pool,
                          lam, diff_norm_g[l], T, H_B, DH_B, 1.0 - lam_init)
        om = _mem_attn(mq.reshape(SB, R, W_M), cache_mem_k[l].reshape(SB, N_MEM, W_M),
                       cache_mem_v[l].reshape(SB, N_MEM, W_M), H_M, DH_M, R)
        ys = _out_proj(ys, oa.reshape(SB * R, W_A), ob.reshape(SB * R, W_B), om.reshape(SB * R, W_M), z,
                       w_out_b, final_g, SB * R, l == depth - 1)
        outs["ks"].append(ko.reshape(SB, R, H_B, E_B)[:, :T])
        outs["vs"].append(vo.reshape(SB, R, H_B, E_B)[:, :T])
        outs["ss"].append(ss)

    y_prompt = yp.reshape(B, L, D)
    y_sample = ys.reshape(SB, R, D)[:, :T]
    return (y_prompt, y_sample, jnp.stack(outs["kp"]), jnp.stack(outs["vp"]), jnp.stack(outs["sp"]),
            jnp.stack(outs["mkp"]), jnp.stack(outs["mvp"]), jnp.stack(outs["ks"]), jnp.stack(outs["vs"]),
            jnp.stack(outs["ss"]))
```

```python
import functools
import math

import jax
import jax.numpy as jnp
from jax import lax
from jax.experimental import pallas as pl
from jax.experimental.pallas import tpu as pltpu

F32 = jnp.float32
BF16 = jnp.bfloat16
EPS = 1e-6
NEG = -1e30
VMEM_LIMIT = 48 * 1024 * 1024
SAMPLE_ROWS = 8
ROW_TILE = 256
RING_SLOTS = 3

_NT = (((1,), (1,)), ((), ()))
_TN = (((0,), (0,)), ((), ()))


def _dot(a, b):
    return jnp.dot(a, b, preferred_element_type=F32)


def _dot_nt(a, b):
    return lax.dot_general(a, b, _NT, preferred_element_type=F32)


def _dot_tn(a, b):
    return lax.dot_general(a, b, _TN, preferred_element_type=F32)


def _rms(x, g):
    return x * lax.rsqrt(jnp.mean(x * x, axis=-1, keepdims=True) + EPS) * g


def _params(*sem):
    return pltpu.CompilerParams(dimension_semantics=sem, vmem_limit_bytes=VMEM_LIMIT)


def _split_halves(q, dh):
    lane = lax.broadcasted_iota(jnp.int32, q.shape, 1)
    zero = jnp.zeros_like(q)
    return jnp.concatenate([jnp.where(lane < dh, q, zero), jnp.where(lane < dh, zero, q)], axis=0)


def _proj_kernel(x_ref, g_ref, w_ref, *out_refs, col_sizes):
    h = _rms(x_ref[...], g_ref[...]).astype(BF16)
    start = 0
    for ref, size in zip(out_refs, col_sizes):
        ref[...] = _dot(h, w_ref[:, start:start + size]).astype(ref.dtype)
        start += size


def _norm_proj(x2d, g, w_bf16, col_sizes, tm, name):
    m, d = x2d.shape
    n = w_bf16.shape[1]
    assert sum(col_sizes) == n and m % tm == 0
    return pl.pallas_call(
        functools.partial(_proj_kernel, col_sizes=col_sizes),
        grid=(m // tm,),
        in_specs=[pl.BlockSpec((tm, d), lambda i: (i, 0)),
                  pl.BlockSpec((1, d), lambda i: (0, 0)),
                  pl.BlockSpec((d, n), lambda i: (0, 0))],
        out_specs=[pl.BlockSpec((tm, s), lambda i: (i, 0)) for s in col_sizes],
        out_shape=[jax.ShapeDtypeStruct((m, s), F32) for s in col_sizes],
        compiler_params=_params("parallel"),
        name=name,
    )(x2d, g.reshape(1, d), w_bf16)


def _in_proj_kernel(x_ref, g_ref, w_ref, *rest, sizes, H, prompt, scale):
    w_pa, w_b, w_m, mix = sizes
    if prompt:
        pa_ref, dq_ref, kb_ref, vb_ref, ko_ref, vo_ref, mq_ref, z_ref = rest
    else:
        pa_ref, dq_ref, ko_ref, vo_ref, mq_ref, z_ref = rest
    e = w_b // H
    h = _rms(x_ref[...], g_ref[...]).astype(BF16)
    c = 0
    pa_ref[...] = _dot(h, w_ref[:, c:c + w_pa])
    c += w_pa
    dq_ref[...] = (_dot(h, w_ref[:, c:c + w_b]) * scale).astype(BF16)
    c += w_b
    k = _dot(h, w_ref[:, c:c + w_b])
    c += w_b
    v = _dot(h, w_ref[:, c:c + w_b])
    c += w_b
    tm = k.shape[0]
    for hh in range(H):
        ko_ref[pl.ds(hh, tm, stride=H), :] = k[:, hh * e:(hh + 1) * e]
        vo_ref[pl.ds(hh, tm, stride=H), :] = v[:, hh * e:(hh + 1) * e]
    if prompt:
        kb_ref[...] = k.astype(BF16)
        vb_ref[...] = v.astype(BF16)
    mq_ref[...] = _dot(h, w_ref[:, c:c + w_m])
    c += w_m
    z_ref[...] = _dot(h, w_ref[:, c:c + mix])


def _in_proj(x2d, g, w_bf16, sizes, H, tm, scale, prompt, name):
    m, d = x2d.shape
    n = w_bf16.shape[1]
    w_pa, w_b, w_m, mix = sizes
    e = w_b // H
    assert w_pa + 3 * w_b + w_m + mix == n and m % tm == 0

    def rows(width):
        return pl.BlockSpec((tm, width), lambda i: (i, 0))

    const = lambda i: (0, 0)
    in_specs = [rows(d), pl.BlockSpec((1, d), const), pl.BlockSpec((d, n), const)]
    args = [x2d, g.reshape(1, d), w_bf16]
    out_specs = [rows(w_pa), rows(w_b)]
    out_shape = [jax.ShapeDtypeStruct((m, w_pa), F32), jax.ShapeDtypeStruct((m, w_b), BF16)]
    if prompt:
        out_specs += [rows(w_b), rows(w_b)]
        out_shape += [jax.ShapeDtypeStruct((m, w_b), BF16), jax.ShapeDtypeStruct((m, w_b), BF16)]
    head_rows = pl.BlockSpec((tm * H, e), lambda i: (i, 0))
    out_specs += [head_rows, head_rows, rows(w_m), rows(mix)]
    out_shape += [jax.ShapeDtypeStruct((m * H, e), F32), jax.ShapeDtypeStruct((m * H, e), F32),
                  jax.ShapeDtypeStruct((m, w_m), F32), jax.ShapeDtypeStruct((m, mix), F32)]
    return pl.pallas_call(
        functools.partial(_in_proj_kernel, sizes=sizes, H=H, prompt=prompt, scale=scale),
        grid=(m // tm,),
        in_specs=in_specs,
        out_specs=out_specs,
        out_shape=out_shape,
        compiler_params=_params("parallel"),
        name=name,
    )(*args)


def _lower_bound(lg, layer, axis=0):
    e = jnp.exp(lg - jnp.max(lg, axis=axis, keepdims=True))
    p = e / jnp.sum(e, axis=axis, keepdims=True)
    head = p[:layer + 1] if axis == 0 else p[:, :layer + 1]
    return jnp.sum(head, axis=axis, keepdims=True)


def _hgrn_chunk_kernel(pa_ref, lg_ref, ng_ref, o_ref, sout_ref, s_scr, b_scr,
                       *, C, layer, H, DK, DV):
    c = pl.program_id(1)

    @pl.when(c == 0)
    def _():
        s_scr[...] = jnp.zeros_like(s_scr)

    W = H * DK
    lb = _lower_bound(lg_ref[...], layer)
    f = lb + (1.0 - lb) * jax.nn.sigmoid(pa_ref[:, W:2 * W])
    g = jnp.log(f)
    kk = 1.0 - f
    q = pa_ref[:, 0:W]

    ti = lax.broadcasted_iota(jnp.int32, (C, C), 0)
    si = lax.broadcasted_iota(jnp.int32, (C, C), 1)
    lmat = (si <= ti).astype(BF16)
    g_hi = g.astype(BF16)
    r1 = g - g_hi.astype(F32)
    g_mid = r1.astype(BF16)
    g_lo = (r1 - g_mid.astype(F32)).astype(BF16)
    b = _dot(lmat, g_hi) + _dot(lmat, g_mid) + _dot(lmat, g_lo)
    b_scr[...] = b
    bl = b_scr[C - 1:C, :]

    row = lax.broadcasted_iota(jnp.int32, (C, 1), 0)
    xor_lower = jnp.where(ti > si, ti ^ si, 0)
    levels = []
    m = C // 2
    while m >= 1:
        upper = (row & (2 * m - 1)) >= m
        if m >= 4:
            pieces = [jnp.broadcast_to(b_scr[blk * 2 * m + m - 1:blk * 2 * m + m, :], (2 * m, W))
                      for blk in range(C // (2 * m))]
            bref = pieces[0] if len(pieces) == 1 else jnp.concatenate(pieces, axis=0)
            d = -jnp.abs(b - bref)
        elif m == 2:
            d = jnp.where(upper, g + jnp.where((row & 3) == 3, pltpu.roll(g, 1, 0), 0.0),
                          jnp.where((row & 3) == 0, pltpu.roll(g, C - 1, 0), 0.0))
        else:
            d = jnp.where(upper, g, 0.0)
        x = (jnp.where(upper, q, kk) * jnp.exp(d)).astype(BF16)
        levels.append((x, xor_lower >= m))
        m //= 2
    q_bf = q.astype(BF16)
    k_bf = kk.astype(BF16)
    q_in = (q * jnp.exp(b)).astype(BF16)
    k_out = (kk * jnp.exp(bl - b)).astype(BF16)
    ebl = jnp.exp(bl)

    for h in range(H):
        ks = slice(h * DK, (h + 1) * DK)
        vs = slice(h * DV, (h + 1) * DV)
        v = pa_ref[:, 2 * W + h * DV:2 * W + (h + 1) * DV].astype(BF16)
        a = jnp.where(ti == si, _dot_nt(q_bf[:, ks], k_bf[:, ks]), 0.0)
        for x, at_least in reversed(levels):
            a = jnp.where(at_least, _dot_nt(x[:, ks], x[:, ks]), a)
        st = s_scr[h]
        o = _dot_nt(q_in[:, ks], st.astype(BF16)) + _dot(a.astype(BF16), v)
        o_ref[:, vs] = _rms(o, ng_ref[...])
        s_scr[h] = ebl[:, ks] * st + _dot_tn(v, k_out[:, ks])

    @pl.when(c == pl.num_programs(1) - 1)
    def _():
        sout_ref[...] = s_scr[...]


def _hgrn_prompt(pa, lg, ng, layer, H, DK, DV, C=128):
    B, L, wp = pa.shape
    W = H * DK
    assert L % C == 0 and wp == 2 * W + H * DV
    return pl.pallas_call(
        functools.partial(_hgrn_chunk_kernel, C=C, layer=layer, H=H, DK=DK, DV=DV),
        grid=(B, L // C),
        in_specs=[pl.BlockSpec((None, C, wp), lambda b, c: (b, c, 0)),
                  pl.BlockSpec(lg.shape, lambda b, c: (0, 0)),
                  pl.BlockSpec((1, DV), lambda b, c: (0, 0))],
        out_specs=[pl.BlockSpec((None, C, H * DV), lambda b, c: (b, c, 0)),
                   pl.BlockSpec((None, H, DV, DK), lambda b, c: (b, 0, 0, 0))],
        out_shape=[jax.ShapeDtypeStruct((B, L, H * DV), F32),
                   jax.ShapeDtypeStruct((B, H, DV, DK), F32)],
        scratch_shapes=[pltpu.VMEM((H, DV, DK), F32), pltpu.VMEM((C, W), F32)],
        compiler_params=_params("parallel", "arbitrary"),
        name="hgrn_prompt",
    )(pa, lg, ng.reshape(1, DV))


def _hgrn_sample_kernel(pa_ref, lg_ref, ng_ref, s0_ref, o_ref, sout_ref, *, NB, T, layer, H, DK, DV):
    W = H * DK
    R = pa_ref.shape[1]
    lb = _lower_bound(lg_ref[...], layer)
    row = lax.broadcasted_iota(jnp.int32, (R, 1), 0)
    valid = row < T
    for n in range(NB):
        f = lb + (1.0 - lb) * jax.nn.sigmoid(pa_ref[n, :, W:2 * W])
        g = jnp.where(valid, jnp.log(f), 0.0)
        kk = jnp.where(valid, 1.0 - f, 0.0)
        q = pa_ref[n, :, 0:W]
        b = g
        sh = 1
        while sh < R:
            b = b + jnp.where(row >= sh, pltpu.roll(b, sh, 0), 0.0)
            sh *= 2
        bl = b[R - 1:R]
        q_in = (q * jnp.exp(b)).astype(BF16)
        k_out = (kk * jnp.exp(bl - b)).astype(BF16)
        ebl = jnp.exp(bl)
        for h in range(H):
            ks = slice(h * DK, (h + 1) * DK)
            v = pa_ref[n, :, 2 * W + h * DV:2 * W + (h + 1) * DV]
            st = s0_ref[n, h]
            o = _dot_nt(q_in[:, ks], st.astype(BF16))
            for s in range(T):
                later = row >= s
                decay = jnp.exp(jnp.where(later, b[:, ks] - b[s:s + 1, ks], 0.0))
                a_s = jnp.sum(jnp.where(later, q[:, ks] * kk[s:s + 1, ks] * decay, 0.0), axis=1, keepdims=True)
                o = o + a_s * v[s:s + 1, :]
            o_ref[n, :, h * DV:(h + 1) * DV] = _rms(o, ng_ref[...])
            sout_ref[n, h] = ebl[:, ks] * st + _dot_tn(v.astype(BF16), k_out[:, ks])


def _hgrn_sample(pa, lg, ng, s0t, layer, T, H, DK, DV, NB=4):
    B, R, wp = pa.shape
    assert B % NB == 0
    return pl.pallas_call(
        functools.partial(_hgrn_sample_kernel, NB=NB, T=T, layer=layer, H=H, DK=DK, DV=DV),
        grid=(B // NB,),
        in_specs=[pl.BlockSpec((NB, R, wp), lambda b: (b, 0, 0)),
                  pl.BlockSpec(lg.shape, lambda b: (0, 0)),
                  pl.BlockSpec((1, DV), lambda b: (0, 0)),
                  pl.BlockSpec((NB, H, DV, DK), lambda b: (b, 0, 0, 0))],
        out_specs=[pl.BlockSpec((NB, R, H * DV), lambda b: (b, 0, 0)),
                   pl.BlockSpec((NB, H, DV, DK), lambda b: (b, 0, 0, 0))],
        out_shape=[jax.ShapeDtypeStruct((B, R, H * DV), F32),
                   jax.ShapeDtypeStruct((B, H, DV, DK), F32)],
        compiler_params=_params("parallel"),
        name="hgrn_sample",
    )(pa, lg, ng.reshape(1, DV), s0t)


def _diff_prompt_kernel(slope_ref, lam_ref, q_ref, k_ref, v_ref, g_ref, o_ref, m_scr, l_scr, acc_scr,
                        *, TQ, DH, out_scale):
    h = pl.program_id(1)
    qi = pl.program_id(2)
    LANES = m_scr.shape[1]
    slope = slope_ref[h]
    qs = _split_halves(q_ref[...], DH)
    trel = lax.broadcasted_iota(jnp.int32, (2 * TQ, TQ), 0) & (TQ - 1)
    srel = lax.broadcasted_iota(jnp.int32, (2 * TQ, TQ), 1)
    key_bias = slope * lax.broadcasted_iota(jnp.int32, (1, TQ), 1).astype(F32)

    m_scr[...] = jnp.full_like(m_scr, NEG)
    l_scr[...] = jnp.zeros_like(l_scr)
    acc_scr[...] = jnp.zeros_like(acc_scr)

    def block(kj, masked):
        rows = pl.ds(pl.multiple_of(kj * TQ, TQ), TQ)
        s = _dot_nt(qs, k_ref[rows, :])
        s = s + (key_bias + slope * ((kj - qi) * TQ).astype(F32))
        if masked:
            s = jnp.where(srel <= trel, s, NEG)
        m_prev = m_scr[...]
        m_new = jnp.maximum(m_prev, jnp.max(s, axis=1, keepdims=True))
        alpha = jnp.exp(m_prev - m_new)
        p = jnp.exp(s - jnp.tile(m_new, (1, TQ // LANES)))
        l_scr[...] = alpha * l_scr[...] + jnp.sum(p, axis=1, keepdims=True)
        acc_scr[...] = alpha * acc_scr[...] + _dot(p.astype(BF16), v_ref[rows, :])
        m_scr[...] = m_new

    def body(kj, carry):
        block(kj, False)
        return carry

    lax.fori_loop(0, qi, body, 0)
    block(qi, True)

    o2 = acc_scr[...] / l_scr[...]
    o = o2[:TQ] - lam_ref[0] * o2[TQ:]
    o_ref[...] = _rms(o, g_ref[...]) * out_scale


def _diff_prompt(q, kb, vb, lam, ng, H, DH, out_scale, TQ=512):
    B, L, W = q.shape
    E = 2 * DH
    slopes = jnp.asarray([2.0 ** (-8.0 * (i + 1) / H) for i in range(H)], F32)
    assert L % TQ == 0 and TQ & (TQ - 1) == 0
    return pl.pallas_call(
        functools.partial(_diff_prompt_kernel, TQ=TQ, DH=DH, out_scale=out_scale),
        grid=(B, H, L // TQ),
        in_specs=[pl.BlockSpec(memory_space=pltpu.SMEM),
                  pl.BlockSpec(memory_space=pltpu.SMEM),
                  pl.BlockSpec((None, TQ, E), lambda b, h, i: (b, i, h)),
                  pl.BlockSpec((None, L, E), lambda b, h, i: (b, 0, h)),
                  pl.BlockSpec((None, L, E), lambda b, h, i: (b, 0, h)),
                  pl.BlockSpec((1, E), lambda b, h, i: (0, 0))],
        out_specs=pl.BlockSpec((None, TQ, E), lambda b, h, i: (b, i, h)),
        out_shape=jax.ShapeDtypeStruct((B, L, W), F32),
        scratch_shapes=[pltpu.VMEM((2 * TQ, E), F32), pltpu.VMEM((2 * TQ, E), F32),
                        pltpu.VMEM((2 * TQ, E), F32)],
        compiler_params=_params("parallel", "parallel", "arbitrary"),
        name="diff_prompt",
    )(slopes, lam, q, kb, vb, ng.reshape(1, E))


def _diff_sample_kernel(pt_ref, lam_ref, q_ref, kn_ref, vn_ref, ng_ref, ck_ref, cv_ref, o_ref,
                        kbuf, vbuf, sem, m_scr, l_scr, acc_scr,
                        *, PP, PAGE, T, H, DH, past, slopes, out_scale, total_steps):
    step = pl.program_id(1)
    n_steps = pl.num_programs(1)
    lin = pl.program_id(0) * n_steps + step
    E = 2 * DH
    R = SAMPLE_ROWS
    NR = 2 * R

    def page_copies(g, slot):
        seq = g // n_steps
        first = (g % n_steps) * PP
        copies = []
        for j in range(PP):
            page = pt_ref[seq, first + j]
            copies.append(pltpu.make_async_copy(ck_ref.at[page], kbuf.at[slot, j], sem.at[slot]))
            copies.append(pltpu.make_async_copy(cv_ref.at[page], vbuf.at[slot, j], sem.at[slot]))
        return copies

    @pl.when(lin == 0)
    def _():
        for g in range(min(RING_SLOTS - 1, total_steps)):
            for c in page_copies(g, g):
                c.start()

    @pl.when(lin + (RING_SLOTS - 1) < total_steps)
    def _():
        ahead = lin + (RING_SLOTS - 1)
        for c in page_copies(ahead, lax.rem(ahead, RING_SLOTS)):
            c.start()

    slot = lax.rem(lin, RING_SLOTS)
    for c in page_copies(lin, slot):
        c.wait()
    k_pages = kbuf.at[slot]
    v_pages = vbuf.at[slot]

    @pl.when(step == 0)
    def _():
        m_scr[...] = jnp.full_like(m_scr, NEG)
        l_scr[...] = jnp.zeros_like(l_scr)
        acc_scr[...] = jnp.zeros_like(acc_scr)

    qs = [_split_halves(q_ref[:, h * E:(h + 1) * E], DH) for h in range(H)]
    head = lax.broadcasted_iota(jnp.int32, (H * NR, 1), 0) // NR
    slope = jnp.zeros((H * NR, 1), F32)
    for h in range(H):
        slope = jnp.where(head == h, slopes[h], slope)

    def update(k_tile, v_tile, n_tiles, kpos, ok):
        s = jnp.concatenate(
            [jnp.concatenate([_dot_nt(qs[h], k_tile(h, j)) for j in range(n_tiles)], axis=1)
             for h in range(H)], axis=0)
        s = s + slope * kpos
        if ok is not None:
            s = jnp.where(ok, s, NEG)
        m_prev = m_scr[...]
        m_new = jnp.maximum(m_prev, jnp.max(s, axis=-1, keepdims=True))
        alpha = jnp.exp(m_prev - m_new)
        p = jnp.exp(s - m_new)
        l_scr[...] = alpha * l_scr[...] + jnp.sum(p, axis=-1, keepdims=True)
        pb = p.astype(BF16)
        pv = []
        for h in range(H):
            out = _dot(pb[h * NR:(h + 1) * NR, 0:PAGE], v_tile(h, 0))
            for j in range(1, n_tiles):
                out = out + _dot(pb[h * NR:(h + 1) * NR, j * PAGE:(j + 1) * PAGE], v_tile(h, j))
            pv.append(out)
        acc_scr[...] = alpha * acc_scr[...] + jnp.concatenate(pv, axis=0)
        m_scr[...] = m_new

    col = lax.broadcasted_iota(jnp.int32, (1, PP * PAGE), 1)
    kpos = (step * (PP * PAGE) + col - past).astype(F32)
    update(lambda h, j: k_pages[j, pl.ds(h, PAGE, stride=H), :].astype(BF16),
           lambda h, j: v_pages[j, pl.ds(h, PAGE, stride=H), :].astype(BF16), PP, kpos, None)

    @pl.when(step == pl.num_programs(1) - 1)
    def _():
        tq = lax.broadcasted_iota(jnp.int32, (H * NR, PAGE), 0) & (R - 1)
        coln = lax.broadcasted_iota(jnp.int32, (H * NR, PAGE), 1)
        pad = jnp.zeros((PAGE - R, E), F32)
        update(lambda h, j: jnp.concatenate([kn_ref[pl.ds(h, R, stride=H), :], pad], axis=0).astype(BF16),
               lambda h, j: jnp.concatenate([vn_ref[pl.ds(h, R, stride=H), :], pad], axis=0).astype(BF16),
               1, coln[0:1].astype(F32), (coln <= tq) & (coln < T))
        o2 = acc_scr[...] / l_scr[...]
        for h in range(H):
            o = o2[h * NR:h * NR + R] - lam_ref[0] * o2[h * NR + R:(h + 1) * NR]
            o_ref[:, h * E:(h + 1) * E] = _rms(o, ng_ref[...]) * out_scale


def _diff_sample(q, kn, vn, cache_k, cache_v, page_ids, lam, ng, T, H, DH, out_scale, PP=16):
    B = q.shape[0]
    n_pages = page_ids.shape[1]
    PAGE = cache_k.shape[1] // H
    E = 2 * DH
    R = SAMPLE_ROWS
    slopes = tuple(2.0 ** (-8.0 * (i + 1) / H) for i in range(H))
    assert n_pages % PP == 0 and q.shape[1] == R
    new_rows = pl.BlockSpec((None, R * H, E), lambda b, s, pt: (b, 0, 0))
    n_steps = n_pages // PP
    page_buf = pltpu.VMEM((RING_SLOTS, PP, PAGE * H, E), F32)
    grid_spec = pltpu.PrefetchScalarGridSpec(
        num_scalar_prefetch=1,
        grid=(B, n_steps),
        in_specs=[pl.BlockSpec(memory_space=pltpu.SMEM),
                  pl.BlockSpec((None, R, H * E), lambda b, s, pt: (b, 0, 0)),
                  new_rows, new_rows,
                  pl.BlockSpec((1, E), lambda b, s, pt: (0, 0)),
                  pl.BlockSpec(memory_space=pl.ANY),
                  pl.BlockSpec(memory_space=pl.ANY)],
        out_specs=pl.BlockSpec((None, R, H * E), lambda b, s, pt: (b, 0, 0)),
        scratch_shapes=[page_buf, page_buf, pltpu.SemaphoreType.DMA((RING_SLOTS,)),
                        pltpu.VMEM((H * 2 * R, 1), F32), pltpu.VMEM((H * 2 * R, 1), F32),
                        pltpu.VMEM((H * 2 * R, E), F32)])
    return pl.pallas_call(
        functools.partial(_diff_sample_kernel, PP=PP, PAGE=PAGE, T=T, H=H, DH=DH,
                          past=n_pages * PAGE, slopes=slopes, out_scale=out_scale,
                          total_steps=B * n_steps),
        grid_spec=grid_spec,
        out_shape=jax.ShapeDtypeStruct((B, R, H * E), F32),
        compiler_params=_params("arbitrary", "arbitrary"),
        name="diff_sample",
    )(page_ids, lam, q, kn, vn, ng.reshape(1, E), cache_k, cache_v)


def _mem_attn_kernel(q_ref, k_ref, v_ref, o_ref, *, H, DH, scale):
    q = q_ref[...] * scale
    T = q.shape[0]
    lane_h = lax.broadcasted_iota(jnp.int32, q.shape, 1) // DH
    qs = jnp.concatenate([jnp.where(lane_h == h, q, 0.0) for h in range(H)], axis=0).astype(BF16)
    s = _dot_nt(qs, k_ref[...].astype(BF16))
    p = jnp.exp(s - jnp.max(s, axis=-1, keepdims=True))
    o4 = _dot(p.astype(BF16), v_ref[...].astype(BF16)) / jnp.sum(p, axis=-1, keepdims=True)
    o = jnp.zeros_like(q)
    for h in range(H):
        o = jnp.where(lane_h == h, o4[h * T:(h + 1) * T], o)
    o_ref[...] = o


def _mem_attn(q, mk, mv, H, DH, T):
    B, L, W = q.shape
    N = mk.shape[1]
    assert L % T == 0
    return pl.pallas_call(
        functools.partial(_mem_attn_kernel, H=H, DH=DH, scale=DH ** -0.5),
        grid=(B, L // T),
        in_specs=[pl.BlockSpec((None, T, W), lambda b, i: (b, i, 0)),
                  pl.BlockSpec((None, N, W), lambda b, i: (b, 0, 0)),
                  pl.BlockSpec((None, N, W), lambda b, i: (b, 0, 0))],
        out_specs=pl.BlockSpec((None, T, W), lambda b, i: (b, i, 0)),
        out_shape=jax.ShapeDtypeStruct((B, L, W), F32),
        compiler_params=_params("parallel", "parallel"),
        name="mem_attn",
    )(q, mk, mv)


def _out_kernel(x_ref, oa_ref, ob_ref, om_ref, z_ref, w_ref, g_ref, y_ref, *, final):
    o = jnp.concatenate([oa_ref[...], ob_ref[...], om_ref[...]], axis=-1)
    z = z_ref[...]
    o = (o * (z * jax.nn.sigmoid(z))).astype(BF16)
    y = x_ref[...] + _dot(o, w_ref[...])
    y_ref[...] = _rms(y, g_ref[...]) if final else y


def _out_proj(x2d, oa, ob, om, z, w_bf16, g, tm, final):
    m, d = x2d.shape
    mix = w_bf16.shape[0]

    def rows(a):
        return pl.BlockSpec((tm, a.shape[1]), lambda i: (i, 0))

    return pl.pallas_call(
        functools.partial(_out_kernel, final=final),
        grid=(m // tm,),
        in_specs=[rows(x2d), rows(oa), rows(ob), rows(om), rows(z),
                  pl.BlockSpec((mix, d), lambda i: (0, 0)),
                  pl.BlockSpec((1, d), lambda i: (0, 0))],
        out_specs=rows(x2d),
        out_shape=jax.ShapeDtypeStruct((m, d), F32),
        compiler_params=_params("parallel"),
        name="out_proj",
    )(x2d, oa, ob, om, z, w_bf16, g.reshape(1, d))


def kernel(x_prompt, x_sample, mem_prompt, cache_k, cache_v, state_hgrn, cache_mem_k, cache_mem_v, page_table, norm_g, w_in, hgrn_lb_logits, hgrn_norm_g, diff_norm_g, lambda_q1, lambda_k1, lambda_q2, lambda_k2, mem_norm_g, w_mem_kv, w_out, final_g):
    B, L, D = x_prompt.shape
    SB, T, _ = x_sample.shape
    depth, _, H_A, DK_A, DV_A = state_hgrn.shape
    _, n_pool, PAGE, H_B, E_B = cache_k.shape
    DH_B = E_B // 2
    _, _, N_MEM, H_M, DH_M = cache_mem_k.shape
    W_A, W_K, W_B, W_M = H_A * DV_A, H_A * DK_A, H_B * E_B, H_M * DH_M
    MIX = W_A + W_B + W_M
    R = SAMPLE_ROWS
    TM = ROW_TILE
    assert T <= R and L % TM == 0
    sizes = (2 * W_K + W_A, W_B, W_M, MIX)
    diff_scale = DH_B ** -0.5

    cache_k3 = cache_k.reshape(depth * n_pool, PAGE * H_B, E_B)
    cache_v3 = cache_v.reshape(depth * n_pool, PAGE * H_B, E_B)

    yp = x_prompt.reshape(B * L, D)
    ys = jnp.pad(x_sample, ((0, 0), (0, R - T), (0, 0))).reshape(SB * R, D)
    outs = {k: [] for k in ("kp", "vp", "sp", "mkp", "mvp", "ks", "vs", "ss")}
    for l in range(depth):
        lam_init = 0.8 - 0.6 * math.exp(-0.3 * l)
        lam = (jnp.exp(jnp.sum(lambda_q1[l] * lambda_k1[l])) - jnp.exp(jnp.sum(lambda_q2[l] * lambda_k2[l]))
               + lam_init).reshape(1).astype(F32)
        w_in_b = w_in[l].astype(BF16)
        w_out_b = w_out[l].astype(BF16)
        w_mem_b = w_mem_kv[l].astype(BF16)

        mk, mv = _norm_proj(mem_prompt.reshape(B * N_MEM, D), mem_norm_g[l], w_mem_b, (W_M, W_M), 256, "mem_kv")
        pa, dq, kb, vb, ko, vo, mq, z = _in_proj(yp, norm_g[l], w_in_b, sizes, H_B, TM, diff_scale, True,
                                                 "in_proj_prompt")
        oa, st = _hgrn_prompt(pa.reshape(B, L, -1), hgrn_lb_logits, hgrn_norm_g[l], l, H_A, DK_A, DV_A)
        ob = _diff_prompt(dq.reshape(B, L, W_B), kb.reshape(B, L, W_B), vb.reshape(B, L, W_B),
                          lam, diff_norm_g[l], H_B, DH_B, 1.0 - lam_init)
        om = _mem_attn(mq.reshape(B, L, W_M), mk.reshape(B, N_MEM, W_M), mv.reshape(B, N_MEM, W_M),
                       H_M, DH_M, 256)
        yp = _out_proj(yp, oa.reshape(B * L, W_A), ob.reshape(B * L, W_B), om.reshape(B * L, W_M), z,
                       w_out_b, final_g, TM, l == depth - 1)
        outs["kp"].append(ko.reshape(B, L, H_B, E_B))
        outs["vp"].append(vo.reshape(B, L, H_B, E_B))
        outs["sp"].append(jnp.swapaxes(st, -1, -2))
        outs["mkp"].append(mk.reshape(B, N_MEM, H_M, DH_M))
        outs["mvp"].append(mv.reshape(B, N_MEM, H_M, DH_M))

        pa, dq, ko, vo, mq, z = _in_proj(ys, norm_g[l], w_in_b, sizes, H_B, SB * R, diff_scale, False,
                                         "in_proj_sample")
        oa, sst = _hgrn_sample(pa.reshape(SB, R, -1), hgrn_lb_logits, hgrn_norm_g[l],
                               jnp.swapaxes(state_hgrn[l], -1, -2), l, T, H_A, DK_A, DV_A)
        ob = _diff_sample(dq.reshape(SB, R, W_B), ko.reshape(SB, R * H_B, E_B), vo.reshape(SB, R * H_B, E_B),
                          cache_k3, cache_v3, page_table + l * n_pool,
                          lam, diff_norm_g[l], T, H_B, DH_B, 1.0 - lam_init)
        om = _mem_attn(mq.reshape(SB, R, W_M), cache_mem_k[l].reshape(SB, N_MEM, W_M),
                       cache_mem_v[l].reshape(SB, N_MEM, W_M), H_M, DH_M, R)
        ys = _out_proj(ys, oa.reshape(SB * R, W_A), ob.reshape(SB * R, W_B), om.reshape(SB * R, W_M), z,
                       w_out_b, final_g, SB * R, l == depth - 1)
        outs["ks"].append(ko.reshape(SB, R, H_B, E_B)[:, :T])
        outs["vs"].append(vo.reshape(SB, R, H_B, E_B)[:, :T])
        outs["ss"].append(jnp.swapaxes(sst, -1, -2))

    y_prompt = yp.reshape(B, L, D)
    y_sample = ys.reshape(SB, R, D)[:, :T]
    return (y_prompt, y_sample, jnp.stack(outs["kp"]), jnp.stack(outs["vp"]), jnp.stack(outs["sp"]),
            jnp.stack(outs["mkp"]), jnp.stack(outs["mvp"]), jnp.stack(outs["ks"]), jnp.stack(outs["vs"]),
            jnp.stack(outs["ss"]))
```

```python
import functools
import math

import jax
import jax.numpy as jnp
from jax import lax
from jax.experimental import pallas as pl
from jax.experimental.pallas import tpu as pltpu

F32 = jnp.float32
BF16 = jnp.bfloat16
EPS = 1e-6
NEG = -1e30
VMEM_LIMIT = 48 * 1024 * 1024
SAMPLE_ROWS = 8
ROW_TILE = 512
RING_SLOTS = 3

_NT = (((1,), (1,)), ((), ()))
_TN = (((0,), (0,)), ((), ()))


def _dot(a, b):
    return jnp.dot(a, b, preferred_element_type=F32)


def _dot_nt(a, b):
    return lax.dot_general(a, b, _NT, preferred_element_type=F32)


def _dot_tn(a, b):
    return lax.dot_general(a, b, _TN, preferred_element_type=F32)


def _rms(x, g):
    return x * lax.rsqrt(jnp.mean(x * x, axis=-1, keepdims=True) + EPS) * g


def _params(*sem):
    return pltpu.CompilerParams(dimension_semantics=sem, vmem_limit_bytes=VMEM_LIMIT)


def _split_halves(q, dh):
    lane = lax.broadcasted_iota(jnp.int32, q.shape, 1)
    zero = jnp.zeros_like(q)
    return jnp.concatenate([jnp.where(lane < dh, q, zero), jnp.where(lane < dh, zero, q)], axis=0)


def _proj_kernel(x_ref, g_ref, w_ref, *out_refs, col_sizes):
    h = _rms(x_ref[...], g_ref[...]).astype(BF16)
    start = 0
    for ref, size in zip(out_refs, col_sizes):
        ref[...] = _dot(h, w_ref[:, start:start + size]).astype(ref.dtype)
        start += size


def _norm_proj(x2d, g, w_bf16, col_sizes, tm, name):
    m, d = x2d.shape
    n = w_bf16.shape[1]
    assert sum(col_sizes) == n and m % tm == 0
    return pl.pallas_call(
        functools.partial(_proj_kernel, col_sizes=col_sizes),
        grid=(m // tm,),
        in_specs=[pl.BlockSpec((tm, d), lambda i: (i, 0)),
                  pl.BlockSpec((1, d), lambda i: (0, 0)),
                  pl.BlockSpec((d, n), lambda i: (0, 0))],
        out_specs=[pl.BlockSpec((tm, s), lambda i: (i, 0)) for s in col_sizes],
        out_shape=[jax.ShapeDtypeStruct((m, s), F32) for s in col_sizes],
        compiler_params=_params("parallel"),
        name=name,
    )(x2d, g.reshape(1, d), w_bf16)


def _in_proj_kernel(x_ref, g_ref, w_ref, *rest, sizes, H, prompt, scale):
    w_pa, w_b, w_m, mix = sizes
    if prompt:
        pa_ref, dq_ref, kb_ref, vb_ref, ko_ref, vo_ref, mq_ref, z_ref = rest
    else:
        pa_ref, dq_ref, ko_ref, vo_ref, mq_ref, z_ref = rest
    e = w_b // H
    h = _rms(x_ref[...], g_ref[...]).astype(BF16)
    c = 0
    pa_ref[...] = _dot(h, w_ref[:, c:c + w_pa])
    c += w_pa
    dq_ref[...] = (_dot(h, w_ref[:, c:c + w_b]) * scale).astype(BF16)
    c += w_b
    k = _dot(h, w_ref[:, c:c + w_b])
    c += w_b
    v = _dot(h, w_ref[:, c:c + w_b])
    c += w_b
    tm = k.shape[0]
    for hh in range(H):
        ko_ref[pl.ds(hh, tm, stride=H), :] = k[:, hh * e:(hh + 1) * e]
        vo_ref[pl.ds(hh, tm, stride=H), :] = v[:, hh * e:(hh + 1) * e]
    if prompt:
        kb_ref[...] = k.astype(BF16)
        vb_ref[...] = v.astype(BF16)
    mq_ref[...] = _dot(h, w_ref[:, c:c + w_m])
    c += w_m
    z_ref[...] = _dot(h, w_ref[:, c:c + mix]).astype(z_ref.dtype)


def _in_proj(x2d, g, w_bf16, sizes, H, tm, scale, prompt, name):
    m, d = x2d.shape
    n = w_bf16.shape[1]
    w_pa, w_b, w_m, mix = sizes
    e = w_b // H
    assert w_pa + 3 * w_b + w_m + mix == n and m % tm == 0

    def rows(width):
        return pl.BlockSpec((tm, width), lambda i: (i, 0))

    const = lambda i: (0, 0)
    in_specs = [rows(d), pl.BlockSpec((1, d), const), pl.BlockSpec((d, n), const)]
    args = [x2d, g.reshape(1, d), w_bf16]
    out_specs = [rows(w_pa), rows(w_b)]
    out_shape = [jax.ShapeDtypeStruct((m, w_pa), F32), jax.ShapeDtypeStruct((m, w_b), BF16)]
    if prompt:
        out_specs += [rows(w_b), rows(w_b)]
        out_shape += [jax.ShapeDtypeStruct((m, w_b), BF16), jax.ShapeDtypeStruct((m, w_b), BF16)]
    head_rows = pl.BlockSpec((tm * H, e), lambda i: (i, 0))
    out_specs += [head_rows, head_rows, rows(w_m), rows(mix)]
    out_shape += [jax.ShapeDtypeStruct((m * H, e), F32), jax.ShapeDtypeStruct((m * H, e), F32),
                  jax.ShapeDtypeStruct((m, w_m), F32), jax.ShapeDtypeStruct((m, mix), BF16)]
    return pl.pallas_call(
        functools.partial(_in_proj_kernel, sizes=sizes, H=H, prompt=prompt, scale=scale),
        grid=(m // tm,),
        in_specs=in_specs,
        out_specs=out_specs,
        out_shape=out_shape,
        compiler_params=_params("parallel"),
        name=name,
    )(*args)


def _lower_bound(lg, layer, axis=0):
    e = jnp.exp(lg - jnp.max(lg, axis=axis, keepdims=True))
    p = e / jnp.sum(e, axis=axis, keepdims=True)
    head = p[:layer + 1] if axis == 0 else p[:, :layer + 1]
    return jnp.sum(head, axis=axis, keepdims=True)


def _hgrn_chunk_kernel(pa_ref, lg_ref, ng_ref, o_ref, sout_ref, s_scr, b_scr,
                       *, C, layer, H, DK, DV):
    c = pl.program_id(1)

    @pl.when(c == 0)
    def _():
        s_scr[...] = jnp.zeros_like(s_scr)

    W = H * DK
    lb = _lower_bound(lg_ref[...], layer)
    f = lb + (1.0 - lb) * jax.nn.sigmoid(pa_ref[:, W:2 * W])
    g = jnp.log(f)
    kk = 1.0 - f
    q = pa_ref[:, 0:W]

    ti = lax.broadcasted_iota(jnp.int32, (C, C), 0)
    si = lax.broadcasted_iota(jnp.int32, (C, C), 1)
    lmat = (si <= ti).astype(BF16)
    g_hi = g.astype(BF16)
    r1 = g - g_hi.astype(F32)
    g_mid = r1.astype(BF16)
    g_lo = (r1 - g_mid.astype(F32)).astype(BF16)
    b = _dot(lmat, g_hi) + _dot(lmat, g_mid) + _dot(lmat, g_lo)
    b_scr[...] = b
    bl = b_scr[C - 1:C, :]

    row = lax.broadcasted_iota(jnp.int32, (C, 1), 0)
    xor_lower = jnp.where(ti > si, ti ^ si, 0)
    levels = []
    m = C // 2
    while m >= 1:
        upper = (row & (2 * m - 1)) >= m
        if m >= 4:
            pieces = [jnp.broadcast_to(b_scr[blk * 2 * m + m - 1:blk * 2 * m + m, :], (2 * m, W))
                      for blk in range(C // (2 * m))]
            bref = pieces[0] if len(pieces) == 1 else jnp.concatenate(pieces, axis=0)
            d = -jnp.abs(b - bref)
        elif m == 2:
            d = jnp.where(upper, g + jnp.where((row & 3) == 3, pltpu.roll(g, 1, 0), 0.0),
                          jnp.where((row & 3) == 0, pltpu.roll(g, C - 1, 0), 0.0))
        else:
            d = jnp.where(upper, g, 0.0)
        x = (jnp.where(upper, q, kk) * jnp.exp(d)).astype(BF16)
        levels.append((x, xor_lower >= m))
        m //= 2
    q_bf = q.astype(BF16)
    k_bf = kk.astype(BF16)
    q_in = (q * jnp.exp(b)).astype(BF16)
    k_out = (kk * jnp.exp(bl - b)).astype(BF16)
    ebl = jnp.exp(bl)

    for h in range(H):
        ks = slice(h * DK, (h + 1) * DK)
        vs = slice(h * DV, (h + 1) * DV)
        v = pa_ref[:, 2 * W + h * DV:2 * W + (h + 1) * DV].astype(BF16)
        a = jnp.where(ti == si, _dot_nt(q_bf[:, ks], k_bf[:, ks]), 0.0)
        for x, at_least in reversed(levels):
            a = jnp.where(at_least, _dot_nt(x[:, ks], x[:, ks]), a)
        st = s_scr[h]
        o = _dot_nt(q_in[:, ks], st.astype(BF16)) + _dot(a.astype(BF16), v)
        o_ref[:, vs] = _rms(o, ng_ref[...]).astype(o_ref.dtype)
        s_scr[h] = ebl[:, ks] * st + _dot_tn(v, k_out[:, ks])

    @pl.when(c == pl.num_programs(1) - 1)
    def _():
        sout_ref[...] = s_scr[...]


def _hgrn_prompt(pa, lg, ng, layer, H, DK, DV, C=128):
    B, L, wp = pa.shape
    W = H * DK
    assert L % C == 0 and wp == 2 * W + H * DV
    return pl.pallas_call(
        functools.partial(_hgrn_chunk_kernel, C=C, layer=layer, H=H, DK=DK, DV=DV),
        grid=(B, L // C),
        in_specs=[pl.BlockSpec((None, C, wp), lambda b, c: (b, c, 0)),
                  pl.BlockSpec(lg.shape, lambda b, c: (0, 0)),
                  pl.BlockSpec((1, DV), lambda b, c: (0, 0))],
        out_specs=[pl.BlockSpec((None, C, H * DV), lambda b, c: (b, c, 0)),
                   pl.BlockSpec((None, H, DV, DK), lambda b, c: (b, 0, 0, 0))],
        out_shape=[jax.ShapeDtypeStruct((B, L, H * DV), BF16),
                   jax.ShapeDtypeStruct((B, H, DV, DK), F32)],
        scratch_shapes=[pltpu.VMEM((H, DV, DK), F32), pltpu.VMEM((C, W), F32)],
        compiler_params=_params("parallel", "arbitrary"),
        name="hgrn_prompt",
    )(pa, lg, ng.reshape(1, DV))


def _hgrn_sample_kernel(pa_ref, lg_ref, ng_ref, s0_ref, o_ref, sout_ref, *, NB, T, layer, H, DK, DV):
    W = H * DK
    R = pa_ref.shape[1]
    lb = _lower_bound(lg_ref[...], layer)
    row = lax.broadcasted_iota(jnp.int32, (R, 1), 0)
    valid = row < T
    for n in range(NB):
        f = lb + (1.0 - lb) * jax.nn.sigmoid(pa_ref[n, :, W:2 * W])
        g = jnp.where(valid, jnp.log(f), 0.0)
        kk = jnp.where(valid, 1.0 - f, 0.0)
        q = pa_ref[n, :, 0:W]
        b = g
        sh = 1
        while sh < R:
            b = b + jnp.where(row >= sh, pltpu.roll(b, sh, 0), 0.0)
            sh *= 2
        bl = b[R - 1:R]
        q_in = (q * jnp.exp(b)).astype(BF16)
        k_out = (kk * jnp.exp(bl - b)).astype(BF16)
        ebl = jnp.exp(bl)
        for h in range(H):
            ks = slice(h * DK, (h + 1) * DK)
            v = pa_ref[n, :, 2 * W + h * DV:2 * W + (h + 1) * DV]
            st = s0_ref[n, h]
            o = _dot_nt(q_in[:, ks], st.astype(BF16))
            for s in range(T):
                later = row >= s
                decay = jnp.exp(jnp.where(later, b[:, ks] - b[s:s + 1, ks], 0.0))
                a_s = jnp.sum(jnp.where(later, q[:, ks] * kk[s:s + 1, ks] * decay, 0.0), axis=1, keepdims=True)
                o = o + a_s * v[s:s + 1, :]
            o_ref[n, :, h * DV:(h + 1) * DV] = _rms(o, ng_ref[...])
            sout_ref[n, h] = ebl[:, ks] * st + _dot_tn(v.astype(BF16), k_out[:, ks])


def _hgrn_sample(pa, lg, ng, s0t, layer, T, H, DK, DV, NB=4):
    B, R, wp = pa.shape
    assert B % NB == 0
    return pl.pallas_call(
        functools.partial(_hgrn_sample_kernel, NB=NB, T=T, layer=layer, H=H, DK=DK, DV=DV),
        grid=(B // NB,),
        in_specs=[pl.BlockSpec((NB, R, wp), lambda b: (b, 0, 0)),
                  pl.BlockSpec(lg.shape, lambda b: (0, 0)),
                  pl.BlockSpec((1, DV), lambda b: (0, 0)),
                  pl.BlockSpec((NB, H, DV, DK), lambda b: (b, 0, 0, 0))],
        out_specs=[pl.BlockSpec((NB, R, H * DV), lambda b: (b, 0, 0)),
                   pl.BlockSpec((NB, H, DV, DK), lambda b: (b, 0, 0, 0))],
        out_shape=[jax.ShapeDtypeStruct((B, R, H * DV), F32),
                   jax.ShapeDtypeStruct((B, H, DV, DK), F32)],
        compiler_params=_params("parallel"),
        name="hgrn_sample",
    )(pa, lg, ng.reshape(1, DV), s0t)


def _diff_prompt_kernel(slope_ref, lam_ref, q_ref, k_ref, v_ref, g_ref, o_ref, m_scr, l_scr, acc_scr,
                        *, TQ, DH, out_scale):
    h = pl.program_id(1)
    qi = pl.program_id(2)
    LANES = m_scr.shape[1]
    slope = slope_ref[h]
    qs = _split_halves(q_ref[...], DH)
    trel = lax.broadcasted_iota(jnp.int32, (2 * TQ, TQ), 0) & (TQ - 1)
    srel = lax.broadcasted_iota(jnp.int32, (2 * TQ, TQ), 1)
    key_bias = slope * lax.broadcasted_iota(jnp.int32, (1, TQ), 1).astype(F32)

    m_scr[...] = jnp.full_like(m_scr, NEG)
    l_scr[...] = jnp.zeros_like(l_scr)
    acc_scr[...] = jnp.zeros_like(acc_scr)

    def block(kj, masked):
        rows = pl.ds(pl.multiple_of(kj * TQ, TQ), TQ)
        s = _dot_nt(qs, k_ref[rows, :])
        s = s + (key_bias + slope * ((kj - qi) * TQ).astype(F32))
        if masked:
            s = jnp.where(srel <= trel, s, NEG)
        m_prev = m_scr[...]
        m_new = jnp.maximum(m_prev, jnp.max(s, axis=1, keepdims=True))
        alpha = jnp.exp(m_prev - m_new)
        p = jnp.exp(s - jnp.tile(m_new, (1, TQ // LANES)))
        l_scr[...] = alpha * l_scr[...] + jnp.sum(p, axis=1, keepdims=True)
        acc_scr[...] = alpha * acc_scr[...] + _dot(p.astype(BF16), v_ref[rows, :])
        m_scr[...] = m_new

    def body(kj, carry):
        block(kj, False)
        return carry

    lax.fori_loop(0, qi, body, 0)
    block(qi, True)

    o2 = acc_scr[...] / l_scr[...]
    o = o2[:TQ] - lam_ref[0] * o2[TQ:]
    o_ref[...] = (_rms(o, g_ref[...]) * out_scale).astype(o_ref.dtype)


def _diff_prompt(q, kb, vb, lam, ng, H, DH, out_scale, TQ=512):
    B, L, W = q.shape
    E = 2 * DH
    slopes = jnp.asarray([2.0 ** (-8.0 * (i + 1) / H) for i in range(H)], F32)
    assert L % TQ == 0 and TQ & (TQ - 1) == 0
    return pl.pallas_call(
        functools.partial(_diff_prompt_kernel, TQ=TQ, DH=DH, out_scale=out_scale),
        grid=(B, H, L // TQ),
        in_specs=[pl.BlockSpec(memory_space=pltpu.SMEM),
                  pl.BlockSpec(memory_space=pltpu.SMEM),
                  pl.BlockSpec((None, TQ, E), lambda b, h, i: (b, i, h)),
                  pl.BlockSpec((None, L, E), lambda b, h, i: (b, 0, h)),
                  pl.BlockSpec((None, L, E), lambda b, h, i: (b, 0, h)),
                  pl.BlockSpec((1, E), lambda b, h, i: (0, 0))],
        out_specs=pl.BlockSpec((None, TQ, E), lambda b, h, i: (b, i, h)),
        out_shape=jax.ShapeDtypeStruct((B, L, W), BF16),
        scratch_shapes=[pltpu.VMEM((2 * TQ, E), F32), pltpu.VMEM((2 * TQ, E), F32),
                        pltpu.VMEM((2 * TQ, E), F32)],
        compiler_params=_params("parallel", "parallel", "arbitrary"),
        name="diff_prompt",
    )(slopes, lam, q, kb, vb, ng.reshape(1, E))


def _diff_sample_kernel(pt_ref, lam_ref, q_ref, kn_ref, vn_ref, ng_ref, ck_ref, cv_ref, o_ref,
                        kbuf, vbuf, sem, m_scr, l_scr, acc_scr,
                        *, PP, PAGE, T, H, DH, past, slopes, out_scale, total_steps):
    step = pl.program_id(1)
    n_steps = pl.num_programs(1)
    lin = pl.program_id(0) * n_steps + step
    E = 2 * DH
    R = SAMPLE_ROWS
    NR = 2 * R

    def page_copies(g, slot):
        seq = g // n_steps
        first = (g % n_steps) * PP
        copies = []
        for j in range(PP):
            page = pt_ref[seq, first + j]
            copies.append(pltpu.make_async_copy(ck_ref.at[page], kbuf.at[slot, j], sem.at[slot]))
            copies.append(pltpu.make_async_copy(cv_ref.at[page], vbuf.at[slot, j], sem.at[slot]))
        return copies

    @pl.when(lin == 0)
    def _():
        for g in range(min(RING_SLOTS - 1, total_steps)):
            for c in page_copies(g, g):
                c.start()

    @pl.when(lin + (RING_SLOTS - 1) < total_steps)
    def _():
        ahead = lin + (RING_SLOTS - 1)
        for c in page_copies(ahead, lax.rem(ahead, RING_SLOTS)):
            c.start()

    slot = lax.rem(lin, RING_SLOTS)
    for c in page_copies(lin, slot):
        c.wait()
    k_pages = kbuf.at[slot]
    v_pages = vbuf.at[slot]

    @pl.when(step == 0)
    def _():
        m_scr[...] = jnp.full_like(m_scr, NEG)
        l_scr[...] = jnp.zeros_like(l_scr)
        acc_scr[...] = jnp.zeros_like(acc_scr)

    qs = [_split_halves(q_ref[:, h * E:(h + 1) * E], DH) for h in range(H)]
    head = lax.broadcasted_iota(jnp.int32, (H * NR, 1), 0) // NR
    slope = jnp.zeros((H * NR, 1), F32)
    for h in range(H):
        slope = jnp.where(head == h, slopes[h], slope)

    def update(k_tile, v_tile, n_tiles, kpos, ok):
        s = jnp.concatenate(
            [jnp.concatenate([_dot_nt(qs[h], k_tile(h, j)) for j in range(n_tiles)], axis=1)
             for h in range(H)], axis=0)
        s = s + slope * kpos
        if ok is not None:
            s = jnp.where(ok, s, NEG)
        m_prev = m_scr[...]
        m_new = jnp.maximum(m_prev, jnp.max(s, axis=-1, keepdims=True))
        alpha = jnp.exp(m_prev - m_new)
        p = jnp.exp(s - m_new)
        l_scr[...] = alpha * l_scr[...] + jnp.sum(p, axis=-1, keepdims=True)
        pb = p.astype(BF16)
        pv = []
        for h in range(H):
            out = _dot(pb[h * NR:(h + 1) * NR, 0:PAGE], v_tile(h, 0))
            for j in range(1, n_tiles):
                out = out + _dot(pb[h * NR:(h + 1) * NR, j * PAGE:(j + 1) * PAGE], v_tile(h, j))
            pv.append(out)
        acc_scr[...] = alpha * acc_scr[...] + jnp.concatenate(pv, axis=0)
        m_scr[...] = m_new

    col = lax.broadcasted_iota(jnp.int32, (1, PP * PAGE), 1)
    kpos = (step * (PP * PAGE) + col - past).astype(F32)
    update(lambda h, j: k_pages[j, pl.ds(h, PAGE, stride=H), :].astype(BF16),
           lambda h, j: v_pages[j, pl.ds(h, PAGE, stride=H), :].astype(BF16), PP, kpos, None)

    @pl.when(step == pl.num_programs(1) - 1)
    def _():
        tq = lax.broadcasted_iota(jnp.int32, (H * NR, PAGE), 0) & (R - 1)
        coln = lax.broadcasted_iota(jnp.int32, (H * NR, PAGE), 1)
        pad = jnp.zeros((PAGE - R, E), F32)
        update(lambda h, j: jnp.concatenate([kn_ref[pl.ds(h, R, stride=H), :], pad], axis=0).astype(BF16),
               lambda h, j: jnp.concatenate([vn_ref[pl.ds(h, R, stride=H), :], pad], axis=0).astype(BF16),
               1, coln[0:1].astype(F32), (coln <= tq) & (coln < T))
        o2 = acc_scr[...] / l_scr[...]
        for h in range(H):
            o = o2[h * NR:h * NR + R] - lam_ref[0] * o2[h * NR + R:(h + 1) * NR]
            o_ref[:, h * E:(h + 1) * E] = _rms(o, ng_ref[...]) * out_scale


def _diff_sample(q, kn, vn, cache_k, cache_v, page_ids, lam, ng, T, H, DH, out_scale, PP=16):
    B = q.shape[0]
    n_pages = page_ids.shape[1]
    PAGE = cache_k.shape[1] // H
    E = 2 * DH
    R = SAMPLE_ROWS
    slopes = tuple(2.0 ** (-8.0 * (i + 1) / H) for i in range(H))
    assert n_pages % PP == 0 and q.shape[1] == R
    new_rows = pl.BlockSpec((None, R * H, E), lambda b, s, pt: (b, 0, 0))
    n_steps = n_pages // PP
    page_buf = pltpu.VMEM((RING_SLOTS, PP, PAGE * H, E), F32)
    grid_spec = pltpu.PrefetchScalarGridSpec(
        num_scalar_prefetch=1,
        grid=(B, n_steps),
        in_specs=[pl.BlockSpec(memory_space=pltpu.SMEM),
                  pl.BlockSpec((None, R, H * E), lambda b, s, pt: (b, 0, 0)),
                  new_rows, new_rows,
                  pl.BlockSpec((1, E), lambda b, s, pt: (0, 0)),
                  pl.BlockSpec(memory_space=pl.ANY),
                  pl.BlockSpec(memory_space=pl.ANY)],
        out_specs=pl.BlockSpec((None, R, H * E), lambda b, s, pt: (b, 0, 0)),
        scratch_shapes=[page_buf, page_buf, pltpu.SemaphoreType.DMA((RING_SLOTS,)),
                        pltpu.VMEM((H * 2 * R, 1), F32), pltpu.VMEM((H * 2 * R, 1), F32),
                        pltpu.VMEM((H * 2 * R, E), F32)])
    return pl.pallas_call(
        functools.partial(_diff_sample_kernel, PP=PP, PAGE=PAGE, T=T, H=H, DH=DH,
                          past=n_pages * PAGE, slopes=slopes, out_scale=out_scale,
                          total_steps=B * n_steps),
        grid_spec=grid_spec,
        out_shape=jax.ShapeDtypeStruct((B, R, H * E), F32),
        compiler_params=_params("arbitrary", "arbitrary"),
        name="diff_sample",
    )(page_ids, lam, q, kn, vn, ng.reshape(1, E), cache_k, cache_v)


def _mem_kv_kernel(x_ref, g_ref, wt_ref, kt_ref, vt_ref):
    h = _rms(x_ref[...], g_ref[...]).astype(BF16)
    kv = _dot_nt(wt_ref[...], h)
    w = kt_ref.shape[0]
    kt_ref[...] = kv[:w]
    vt_ref[...] = kv[w:]


def _mem_kv(mem, g, wt_bf16):
    B, N, D = mem.shape
    w = wt_bf16.shape[0] // 2
    out = jax.ShapeDtypeStruct((B, w, N), F32)
    blk = pl.BlockSpec((None, w, N), lambda b: (b, 0, 0))
    return pl.pallas_call(
        _mem_kv_kernel,
        grid=(B,),
        in_specs=[pl.BlockSpec((None, N, D), lambda b: (b, 0, 0)),
                  pl.BlockSpec((1, D), lambda b: (0, 0)),
                  pl.BlockSpec((2 * w, D), lambda b: (0, 0))],
        out_specs=[blk, blk],
        out_shape=[out, out],
        compiler_params=_params("parallel"),
        name="mem_kv",
    )(mem, g.reshape(1, D), wt_bf16)


def _mem_attend(q, kt, vt, H, DH):
    T = q.shape[0]
    q = q * (DH ** -0.5)
    lane_h = lax.broadcasted_iota(jnp.int32, q.shape, 1) // DH
    qs = jnp.concatenate([jnp.where(lane_h == h, q, 0.0) for h in range(H)], axis=0).astype(BF16)
    s = _dot(qs, kt.astype(BF16))
    p = jnp.exp(s - jnp.max(s, axis=-1, keepdims=True))
    o4 = _dot_nt(p.astype(BF16), vt.astype(BF16)) / jnp.sum(p, axis=-1, keepdims=True)
    o = jnp.zeros_like(q)
    for h in range(H):
        o = jnp.where(lane_h == h, o4[h * T:(h + 1) * T], o)
    return o


def _mem_attn_kernel(q_ref, kt_ref, vt_ref, o_ref, *, NB, H, DH):
    for n in range(NB):
        o_ref[n] = _mem_attend(q_ref[n], kt_ref[n], vt_ref[n], H, DH)


def _mem_attn(q, kt, vt, H, DH, NB=4):
    B, T, W = q.shape
    N = kt.shape[2]
    assert B % NB == 0
    kv = pl.BlockSpec((NB, W, N), lambda b: (b, 0, 0))
    return pl.pallas_call(
        functools.partial(_mem_attn_kernel, NB=NB, H=H, DH=DH),
        grid=(B // NB,),
        in_specs=[pl.BlockSpec((NB, T, W), lambda b: (b, 0, 0)), kv, kv],
        out_specs=pl.BlockSpec((NB, T, W), lambda b: (b, 0, 0)),
        out_shape=jax.ShapeDtypeStruct((B, T, W), F32),
        compiler_params=_params("parallel"),
        name="mem_attn",
    )(q, kt, vt)


def _out_kernel(x_ref, oa_ref, ob_ref, m_ref, z_ref, w_ref, g_ref, *rest, final, mem_heads):
    if mem_heads is None:
        (y_ref,) = rest
        om = m_ref[...]
    else:
        kt_ref, vt_ref, y_ref = rest
        om = _mem_attend(m_ref[...], kt_ref[...], vt_ref[...], *mem_heads)
    o = jnp.concatenate([oa_ref[...].astype(F32), ob_ref[...].astype(F32), om], axis=-1)
    z = z_ref[...].astype(F32)
    o = (o * (z * jax.nn.sigmoid(z))).astype(BF16)
    y = x_ref[...] + _dot(o, w_ref[...])
    y_ref[...] = _rms(y, g_ref[...]) if final else y


def _out_proj(x2d, oa, ob, m, z, w_bf16, g, tm, final, mem=None):
    rows_total, d = x2d.shape
    mix = w_bf16.shape[0]

    def rows(a):
        return pl.BlockSpec((tm, a.shape[1]), lambda i: (i, 0))

    in_specs = [rows(x2d), rows(oa), rows(ob), rows(m), rows(z),
                pl.BlockSpec((mix, d), lambda i: (0, 0)),
                pl.BlockSpec((1, d), lambda i: (0, 0))]
    args = [x2d, oa, ob, m, z, w_bf16, g.reshape(1, d)]
    mem_heads = None
    if mem is not None:
        kt, vt, H, DH, rows_per_batch = mem
        assert rows_per_batch % tm == 0
        tiles = rows_per_batch // tm
        kv = pl.BlockSpec((None,) + kt.shape[1:], lambda i: (i // tiles, 0, 0))
        in_specs += [kv, kv]
        args += [kt, vt]
        mem_heads = (H, DH)
    return pl.pallas_call(
        functools.partial(_out_kernel, final=final, mem_heads=mem_heads),
        grid=(rows_total // tm,),
        in_specs=in_specs,
        out_specs=rows(x2d),
        out_shape=jax.ShapeDtypeStruct((rows_total, d), F32),
        compiler_params=_params("parallel"),
        name="out_proj",
    )(*args)


def kernel(x_prompt, x_sample, mem_prompt, cache_k, cache_v, state_hgrn, cache_mem_k, cache_mem_v, page_table, norm_g, w_in, hgrn_lb_logits, hgrn_norm_g, diff_norm_g, lambda_q1, lambda_k1, lambda_q2, lambda_k2, mem_norm_g, w_mem_kv, w_out, final_g):
    B, L, D = x_prompt.shape
    SB, T, _ = x_sample.shape
    depth, _, H_A, DK_A, DV_A = state_hgrn.shape
    _, n_pool, PAGE, H_B, E_B = cache_k.shape
    DH_B = E_B // 2
    _, _, N_MEM, H_M, DH_M = cache_mem_k.shape
    W_A, W_K, W_B, W_M = H_A * DV_A, H_A * DK_A, H_B * E_B, H_M * DH_M
    MIX = W_A + W_B + W_M
    R = SAMPLE_ROWS
    TM = ROW_TILE
    assert T <= R and L % TM == 0
    sizes = (2 * W_K + W_A, W_B, W_M, MIX)
    diff_scale = DH_B ** -0.5

    cache_k3 = cache_k.reshape(depth * n_pool, PAGE * H_B, E_B)
    cache_v3 = cache_v.reshape(depth * n_pool, PAGE * H_B, E_B)

    yp = x_prompt.reshape(B * L, D)
    ys = jnp.pad(x_sample, ((0, 0), (0, R - T), (0, 0))).reshape(SB * R, D)
    outs = {k: [] for k in ("kp", "vp", "sp", "mkp", "mvp", "ks", "vs", "ss")}
    for l in range(depth):
        lam_init = 0.8 - 0.6 * math.exp(-0.3 * l)
        lam = (jnp.exp(jnp.sum(lambda_q1[l] * lambda_k1[l])) - jnp.exp(jnp.sum(lambda_q2[l] * lambda_k2[l]))
               + lam_init).reshape(1).astype(F32)
        w_in_b = w_in[l].astype(BF16)
        w_out_b = w_out[l].astype(BF16)
        w_mem_t = w_mem_kv[l].T.astype(BF16)

        def channel_major(a):
            return jnp.transpose(a, (0, 2, 3, 1)).reshape(a.shape[0], W_M, N_MEM)

        def token_major(a):
            return jnp.transpose(a.reshape(a.shape[0], H_M, DH_M, N_MEM), (0, 3, 1, 2))

        mkt, mvt = _mem_kv(mem_prompt, mem_norm_g[l], w_mem_t)
        pa, dq, kb, vb, ko, vo, mq, z = _in_proj(yp, norm_g[l], w_in_b, sizes, H_B, TM, diff_scale, True,
                                                 "in_proj_prompt")
        oa, st = _hgrn_prompt(pa.reshape(B, L, -1), hgrn_lb_logits, hgrn_norm_g[l], l, H_A, DK_A, DV_A)
        ob = _diff_prompt(dq.reshape(B, L, W_B), kb.reshape(B, L, W_B), vb.reshape(B, L, W_B),
                          lam, diff_norm_g[l], H_B, DH_B, 1.0 - lam_init)
        yp = _out_proj(yp, oa.reshape(B * L, W_A), ob.reshape(B * L, W_B), mq, z,
                       w_out_b, final_g, TM, l == depth - 1, mem=(mkt, mvt, H_M, DH_M, L))
        outs["kp"].append(ko.reshape(B, L, H_B, E_B))
        outs["vp"].append(vo.reshape(B, L, H_B, E_B))
        outs["sp"].append(jnp.swapaxes(st, -1, -2))
        outs["mkp"].append(token_major(mkt))
        outs["mvp"].append(token_major(mvt))

        pa, dq, ko, vo, mq, z = _in_proj(ys, norm_g[l], w_in_b, sizes, H_B, SB * R, diff_scale, False,
                                         "in_proj_sample")
        oa, sst = _hgrn_sample(pa.reshape(SB, R, -1), hgrn_lb_logits, hgrn_norm_g[l],
                               jnp.swapaxes(state_hgrn[l], -1, -2), l, T, H_A, DK_A, DV_A)
        ob = _diff_sample(dq.reshape(SB, R, W_B), ko.reshape(SB, R * H_B, E_B), vo.reshape(SB, R * H_B, E_B),
                          cache_k3, cache_v3, page_table + l * n_pool,
                          lam, diff_norm_g[l], T, H_B, DH_B, 1.0 - lam_init)
        om = _mem_attn(mq.reshape(SB, R, W_M), channel_major(cache_mem_k[l]), channel_major(cache_mem_v[l]),
                       H_M, DH_M)
        ys = _out_proj(ys, oa.reshape(SB * R, W_A), ob.reshape(SB * R, W_B), om.reshape(SB * R, W_M), z,
                       w_out_b, final_g, SB * R, l == depth - 1)
        outs["ks"].append(ko.reshape(SB, R, H_B, E_B)[:, :T])
        outs["vs"].append(vo.reshape(SB, R, H_B, E_B)[:, :T])
        outs["ss"].append(jnp.swapaxes(sst, -1, -2))

    y_prompt = yp.reshape(B, L, D)
    y_sample = ys.reshape(SB, R, D)[:, :T]
    return (y_prompt, y_sample, jnp.stack(outs["kp"]), jnp.stack(outs["vp"]), jnp.stack(outs["sp"]),
            jnp.stack(outs["mkp"]), jnp.stack(outs["mvp"]), jnp.stack(outs["ks"]), jnp.stack(outs["vs"]),
            jnp.stack(outs["ss"]))
```

```python
import functools
import math

import jax
import jax.numpy as jnp
from jax import lax
from jax.experimental import pallas as pl
from jax.experimental.pallas import tpu as pltpu

F32 = jnp.float32
BF16 = jnp.bfloat16
EPS = 1e-6
NEG = -1e30
LOG2E = math.log2(math.e)
VMEM_LIMIT = 48 * 1024 * 1024
SAMPLE_ROWS = 8
ROW_TILE = 512
RING_SLOTS = 3

_NT = (((1,), (1,)), ((), ()))
_TN = (((0,), (0,)), ((), ()))


def _dot(a, b):
    return jnp.dot(a, b, preferred_element_type=F32)


def _dot_nt(a, b):
    return lax.dot_general(a, b, _NT, preferred_element_type=F32)


def _dot_tn(a, b):
    return lax.dot_general(a, b, _TN, preferred_element_type=F32)


def _rms(x, g):
    return x * lax.rsqrt(jnp.mean(x * x, axis=-1, keepdims=True) + EPS) * g


def _params(*sem):
    return pltpu.CompilerParams(dimension_semantics=sem, vmem_limit_bytes=VMEM_LIMIT)


def _split_halves(q, dh):
    lane = lax.broadcasted_iota(jnp.int32, q.shape, 1)
    zero = jnp.zeros_like(q)
    return jnp.concatenate([jnp.where(lane < dh, q, zero), jnp.where(lane < dh, zero, q)], axis=0)


def _proj_kernel(x_ref, g_ref, w_ref, *out_refs, col_sizes):
    h = _rms(x_ref[...], g_ref[...]).astype(BF16)
    start = 0
    for ref, size in zip(out_refs, col_sizes):
        ref[...] = _dot(h, w_ref[:, start:start + size]).astype(ref.dtype)
        start += size


def _norm_proj(x2d, g, w_bf16, col_sizes, tm, name):
    m, d = x2d.shape
    n = w_bf16.shape[1]
    assert sum(col_sizes) == n and m % tm == 0
    return pl.pallas_call(
        functools.partial(_proj_kernel, col_sizes=col_sizes),
        grid=(m // tm,),
        in_specs=[pl.BlockSpec((tm, d), lambda i: (i, 0)),
                  pl.BlockSpec((1, d), lambda i: (0, 0)),
                  pl.BlockSpec((d, n), lambda i: (0, 0))],
        out_specs=[pl.BlockSpec((tm, s), lambda i: (i, 0)) for s in col_sizes],
        out_shape=[jax.ShapeDtypeStruct((m, s), F32) for s in col_sizes],
        compiler_params=_params("parallel"),
        name=name,
    )(x2d, g.reshape(1, d), w_bf16)


def _in_proj_kernel(x_ref, g_ref, w_ref, *rest, sizes, H, prompt, scale):
    w_pa, w_b, w_m, mix = sizes
    if prompt:
        pa_ref, dq_ref, kb_ref, vb_ref, ko_ref, vo_ref, mq_ref, z_ref = rest
    else:
        pa_ref, dq_ref, ko_ref, vo_ref, mq_ref, z_ref = rest
    e = w_b // H
    h = _rms(x_ref[...], g_ref[...]).astype(BF16)
    c = 0
    pa_ref[...] = _dot(h, w_ref[:, c:c + w_pa])
    c += w_pa
    dq_ref[...] = (_dot(h, w_ref[:, c:c + w_b]) * scale).astype(BF16)
    c += w_b
    k = _dot(h, w_ref[:, c:c + w_b])
    c += w_b
    v = _dot(h, w_ref[:, c:c + w_b])
    c += w_b
    tm = k.shape[0]
    for hh in range(H):
        ko_ref[pl.ds(hh, tm, stride=H), :] = k[:, hh * e:(hh + 1) * e]
        vo_ref[pl.ds(hh, tm, stride=H), :] = v[:, hh * e:(hh + 1) * e]
    if prompt:
        kb_ref[...] = k.astype(BF16)
        vb_ref[...] = v.astype(BF16)
    mq_ref[...] = _dot(h, w_ref[:, c:c + w_m])
    c += w_m
    z_ref[...] = _dot(h, w_ref[:, c:c + mix]).astype(z_ref.dtype)


def _in_proj(x2d, g, w_bf16, sizes, H, tm, scale, prompt, name):
    m, d = x2d.shape
    n = w_bf16.shape[1]
    w_pa, w_b, w_m, mix = sizes
    e = w_b // H
    assert w_pa + 3 * w_b + w_m + mix == n and m % tm == 0

    def rows(width):
        return pl.BlockSpec((tm, width), lambda i: (i, 0))

    const = lambda i: (0, 0)
    in_specs = [rows(d), pl.BlockSpec((1, d), const), pl.BlockSpec((d, n), const)]
    args = [x2d, g.reshape(1, d), w_bf16]
    out_specs = [rows(w_pa), rows(w_b)]
    out_shape = [jax.ShapeDtypeStruct((m, w_pa), F32), jax.ShapeDtypeStruct((m, w_b), BF16)]
    if prompt:
        out_specs += [rows(w_b), rows(w_b)]
        out_shape += [jax.ShapeDtypeStruct((m, w_b), BF16), jax.ShapeDtypeStruct((m, w_b), BF16)]
    head_rows = pl.BlockSpec((tm * H, e), lambda i: (i, 0))
    out_specs += [head_rows, head_rows, rows(w_m), rows(mix)]
    out_shape += [jax.ShapeDtypeStruct((m * H, e), F32), jax.ShapeDtypeStruct((m * H, e), F32),
                  jax.ShapeDtypeStruct((m, w_m), F32), jax.ShapeDtypeStruct((m, mix), BF16)]
    return pl.pallas_call(
        functools.partial(_in_proj_kernel, sizes=sizes, H=H, prompt=prompt, scale=scale),
        grid=(m // tm,),
        in_specs=in_specs,
        out_specs=out_specs,
        out_shape=out_shape,
        compiler_params=_params("parallel"),
        name=name,
    )(*args)


def _lower_bound(lg, layer, axis=0):
    e = jnp.exp(lg - jnp.max(lg, axis=axis, keepdims=True))
    p = e / jnp.sum(e, axis=axis, keepdims=True)
    head = p[:layer + 1] if axis == 0 else p[:, :layer + 1]
    return jnp.sum(head, axis=axis, keepdims=True)


def _hgrn_chunk_kernel(pa_ref, lg_ref, ng_ref, o_ref, sout_ref, s_scr, b_scr,
                       *, C, NC, layer, H, DK, DV):
    step = pl.program_id(1)

    @pl.when(step == 0)
    def _():
        s_scr[...] = jnp.zeros_like(s_scr)

    W = H * DK
    lb = _lower_bound(lg_ref[...], layer)
    ti = lax.broadcasted_iota(jnp.int32, (C, C), 0)
    si = lax.broadcasted_iota(jnp.int32, (C, C), 1)
    lmat = (si <= ti).astype(BF16)
    xor_lower = jnp.where(ti > si, ti ^ si, 0)
    row = lax.broadcasted_iota(jnp.int32, (C, 1), 0)

    def prepare(n):
        rows = slice(n * C, (n + 1) * C)
        f = lb + (1.0 - lb) * jax.nn.sigmoid(pa_ref[rows, W:2 * W])
        g = jnp.log2(f)
        kk = 1.0 - f
        q = pa_ref[rows, 0:W]
        g_hi = g.astype(BF16)
        r1 = g - g_hi.astype(F32)
        g_mid = r1.astype(BF16)
        g_lo = (r1 - g_mid.astype(F32)).astype(BF16)
        b = _dot(lmat, g_hi) + _dot(lmat, g_mid) + _dot(lmat, g_lo)
        b_scr[rows, :] = b
        bl = b_scr[(n + 1) * C - 1:(n + 1) * C, :]

        levels = []
        m = C // 2
        while m >= 1:
            upper = (row & (2 * m - 1)) >= m
            if m >= 4:
                pieces = [jnp.broadcast_to(b_scr[n * C + blk * 2 * m + m - 1:n * C + blk * 2 * m + m, :],
                                           (2 * m, W)) for blk in range(C // (2 * m))]
                bref = pieces[0] if len(pieces) == 1 else jnp.concatenate(pieces, axis=0)
                d = -jnp.abs(b - bref)
            elif m == 2:
                d = jnp.where(upper, g + jnp.where((row & 3) == 3, pltpu.roll(g, 1, 0), 0.0),
                              jnp.where((row & 3) == 0, pltpu.roll(g, C - 1, 0), 0.0))
            else:
                d = jnp.where(upper, g, 0.0)
            levels.append((jnp.where(upper, q, kk) * jnp.exp2(d)).astype(BF16))
            m //= 2
        q_bf = q.astype(BF16)
        k_bf = kk.astype(BF16)
        intra = []
        for h in range(H):
            ks = slice(h * DK, (h + 1) * DK)
            a = jnp.where(ti == si, _dot_nt(q_bf[:, ks], k_bf[:, ks]), 0.0)
            m = 1
            for x in reversed(levels):
                a = jnp.where(xor_lower >= m, _dot_nt(x[:, ks], x[:, ks]), a)
                m *= 2
            v = pa_ref[rows, 2 * W + h * DV:2 * W + (h + 1) * DV].astype(BF16)
            intra.append((_dot(a.astype(BF16), v), v))
        q_in = (q * jnp.exp2(b)).astype(BF16)
        k_out = (kk * jnp.exp2(bl - b)).astype(BF16)
        return intra, q_in, k_out, jnp.exp2(bl)

    prepared = [prepare(n) for n in range(NC)]
    for n, (intra, q_in, k_out, ebl) in enumerate(prepared):
        rows = slice(n * C, (n + 1) * C)
        for h in range(H):
            ks = slice(h * DK, (h + 1) * DK)
            o_intra, v = intra[h]
            st = s_scr[h]
            o = _dot_nt(q_in[:, ks], st.astype(BF16)) + o_intra
            o_ref[rows, h * DV:(h + 1) * DV] = _rms(o, ng_ref[...]).astype(o_ref.dtype)
            s_scr[h] = ebl[:, ks] * st + _dot_tn(v, k_out[:, ks])

    @pl.when(step == pl.num_programs(1) - 1)
    def _():
        sout_ref[...] = s_scr[...]


def _hgrn_prompt(pa, lg, ng, layer, H, DK, DV, C=128, NC=2):
    B, L, wp = pa.shape
    W = H * DK
    R = C * NC
    assert L % R == 0 and wp == 2 * W + H * DV
    return pl.pallas_call(
        functools.partial(_hgrn_chunk_kernel, C=C, NC=NC, layer=layer, H=H, DK=DK, DV=DV),
        grid=(B, L // R),
        in_specs=[pl.BlockSpec((None, R, wp), lambda b, c: (b, c, 0)),
                  pl.BlockSpec(lg.shape, lambda b, c: (0, 0)),
                  pl.BlockSpec((1, DV), lambda b, c: (0, 0))],
        out_specs=[pl.BlockSpec((None, R, H * DV), lambda b, c: (b, c, 0)),
                   pl.BlockSpec((None, H, DV, DK), lambda b, c: (b, 0, 0, 0))],
        out_shape=[jax.ShapeDtypeStruct((B, L, H * DV), BF16),
                   jax.ShapeDtypeStruct((B, H, DV, DK), F32)],
        scratch_shapes=[pltpu.VMEM((H, DV, DK), F32), pltpu.VMEM((R, W), F32)],
        compiler_params=_params("parallel", "arbitrary"),
        name="hgrn_prompt",
    )(pa, lg, ng.reshape(1, DV))


def _hgrn_sample_kernel(pa_ref, lg_ref, ng_ref, s0_ref, o_ref, sout_ref, *, NB, T, layer, H, DK, DV):
    W = H * DK
    R = pa_ref.shape[1]
    lb = _lower_bound(lg_ref[...], layer)
    row = lax.broadcasted_iota(jnp.int32, (R, 1), 0)
    valid = row < T
    for n in range(NB):
        f = lb + (1.0 - lb) * jax.nn.sigmoid(pa_ref[n, :, W:2 * W])
        g = jnp.where(valid, jnp.log2(f), 0.0)
        kk = jnp.where(valid, 1.0 - f, 0.0)
        q = pa_ref[n, :, 0:W]
        b = g
        sh = 1
        while sh < R:
            b = b + jnp.where(row >= sh, pltpu.roll(b, sh, 0), 0.0)
            sh *= 2
        bl = b[R - 1:R]
        q_in = (q * jnp.exp2(b)).astype(BF16)
        k_out = (kk * jnp.exp2(bl - b)).astype(BF16)
        ebl = jnp.exp2(bl)
        for h in range(H):
            ks = slice(h * DK, (h + 1) * DK)
            v = pa_ref[n, :, 2 * W + h * DV:2 * W + (h + 1) * DV]
            st = s0_ref[n, h]
            o = _dot_nt(q_in[:, ks], st.astype(BF16))
            for s in range(T):
                later = row >= s
                decay = jnp.exp2(jnp.where(later, b[:, ks] - b[s:s + 1, ks], 0.0))
                a_s = jnp.sum(jnp.where(later, q[:, ks] * kk[s:s + 1, ks] * decay, 0.0), axis=1, keepdims=True)
                o = o + a_s * v[s:s + 1, :]
            o_ref[n, :, h * DV:(h + 1) * DV] = _rms(o, ng_ref[...])
            sout_ref[n, h] = ebl[:, ks] * st + _dot_tn(v.astype(BF16), k_out[:, ks])


def _hgrn_sample(pa, lg, ng, s0t, layer, T, H, DK, DV, NB=4):
    B, R, wp = pa.shape
    assert B % NB == 0
    return pl.pallas_call(
        functools.partial(_hgrn_sample_kernel, NB=NB, T=T, layer=layer, H=H, DK=DK, DV=DV),
        grid=(B // NB,),
        in_specs=[pl.BlockSpec((NB, R, wp), lambda b: (b, 0, 0)),
                  pl.BlockSpec(lg.shape, lambda b: (0, 0)),
                  pl.BlockSpec((1, DV), lambda b: (0, 0)),
                  pl.BlockSpec((NB, H, DV, DK), lambda b: (b, 0, 0, 0))],
        out_specs=[pl.BlockSpec((NB, R, H * DV), lambda b: (b, 0, 0)),
                   pl.BlockSpec((NB, H, DV, DK), lambda b: (b, 0, 0, 0))],
        out_shape=[jax.ShapeDtypeStruct((B, R, H * DV), F32),
                   jax.ShapeDtypeStruct((B, H, DV, DK), F32)],
        compiler_params=_params("parallel"),
        name="hgrn_sample",
    )(pa, lg, ng.reshape(1, DV), s0t)


def _diff_prompt_kernel(slope_ref, lam_ref, q_ref, k_ref, v_ref, g_ref, o_ref, m_scr, l_scr, acc_scr,
                        *, TQ, DH, out_scale):
    h = pl.program_id(1)
    qi = pl.program_id(2)
    LANES = m_scr.shape[1]
    slope = slope_ref[h]
    qs = _split_halves(q_ref[...], DH)
    trel = lax.broadcasted_iota(jnp.int32, (2 * TQ, TQ), 0) & (TQ - 1)
    srel = lax.broadcasted_iota(jnp.int32, (2 * TQ, TQ), 1)
    key_bias = slope * lax.broadcasted_iota(jnp.int32, (1, TQ), 1).astype(F32)

    m_scr[...] = jnp.full_like(m_scr, NEG)
    l_scr[...] = jnp.zeros_like(l_scr)
    acc_scr[...] = jnp.zeros_like(acc_scr)

    def block(kj, masked):
        rows = pl.ds(pl.multiple_of(kj * TQ, TQ), TQ)
        s = _dot_nt(qs, k_ref[rows, :])
        s = s + (key_bias + slope * ((kj - qi) * TQ).astype(F32))
        if masked:
            s = jnp.where(srel <= trel, s, NEG)
        m_prev = m_scr[...]
        m_new = jnp.maximum(m_prev, jnp.max(s, axis=1, keepdims=True))
        alpha = jnp.exp2(m_prev - m_new)
        p = jnp.exp2(s - jnp.tile(m_new, (1, TQ // LANES)))
        l_scr[...] = alpha * l_scr[...] + jnp.sum(p, axis=1, keepdims=True)
        acc_scr[...] = alpha * acc_scr[...] + _dot(p.astype(BF16), v_ref[rows, :])
        m_scr[...] = m_new

    def body(kj, carry):
        block(kj, False)
        return carry

    lax.fori_loop(0, qi, body, 0)
    block(qi, True)

    o2 = acc_scr[...] / l_scr[...]
    o = o2[:TQ] - lam_ref[0] * o2[TQ:]
    o_ref[...] = (_rms(o, g_ref[...]) * out_scale).astype(o_ref.dtype)


def _diff_prompt(q, kb, vb, lam, ng, H, DH, out_scale, TQ=512):
    B, L, W = q.shape
    E = 2 * DH
    slopes = jnp.asarray([LOG2E * 2.0 ** (-8.0 * (i + 1) / H) for i in range(H)], F32)
    assert L % TQ == 0 and TQ & (TQ - 1) == 0
    return pl.pallas_call(
        functools.partial(_diff_prompt_kernel, TQ=TQ, DH=DH, out_scale=out_scale),
        grid=(B, H, L // TQ),
        in_specs=[pl.BlockSpec(memory_space=pltpu.SMEM),
                  pl.BlockSpec(memory_space=pltpu.SMEM),
                  pl.BlockSpec((None, TQ, E), lambda b, h, i: (b, i, h)),
                  pl.BlockSpec((None, L, E), lambda b, h, i: (b, 0, h)),
                  pl.BlockSpec((None, L, E), lambda b, h, i: (b, 0, h)),
                  pl.BlockSpec((1, E), lambda b, h, i: (0, 0))],
        out_specs=pl.BlockSpec((None, TQ, E), lambda b, h, i: (b, i, h)),
        out_shape=jax.ShapeDtypeStruct((B, L, W), BF16),
        scratch_shapes=[pltpu.VMEM((2 * TQ, E), F32), pltpu.VMEM((2 * TQ, E), F32),
                        pltpu.VMEM((2 * TQ, E), F32)],
        compiler_params=_params("parallel", "parallel", "arbitrary"),
        name="diff_prompt",
    )(slopes, lam, q, kb, vb, ng.reshape(1, E))


def _diff_sample_kernel(pt_ref, lam_ref, q_ref, kn_ref, vn_ref, ng_ref, ck_ref, cv_ref, o_ref,
                        kbuf, vbuf, sem, m_scr, l_scr, acc_scr,
                        *, PP, PAGE, T, H, DH, past, slopes, out_scale, total_steps):
    step = pl.program_id(1)
    n_steps = pl.num_programs(1)
    lin = pl.program_id(0) * n_steps + step
    E = 2 * DH
    R = SAMPLE_ROWS
    NR = 2 * R

    def page_copies(g, slot):
        seq = g // n_steps
        first = (g % n_steps) * PP
        copies = []
        for j in range(PP):
            page = pt_ref[seq, first + j]
            copies.append(pltpu.make_async_copy(ck_ref.at[page], kbuf.at[slot, j], sem.at[slot]))
            copies.append(pltpu.make_async_copy(cv_ref.at[page], vbuf.at[slot, j], sem.at[slot]))
        return copies

    @pl.when(lin == 0)
    def _():
        for g in range(min(RING_SLOTS - 1, total_steps)):
            for c in page_copies(g, g):
                c.start()

    @pl.when(lin + (RING_SLOTS - 1) < total_steps)
    def _():
        ahead = lin + (RING_SLOTS - 1)
        for c in page_copies(ahead, lax.rem(ahead, RING_SLOTS)):
            c.start()

    slot = lax.rem(lin, RING_SLOTS)
    for c in page_copies(lin, slot):
        c.wait()
    k_pages = kbuf.at[slot]
    v_pages = vbuf.at[slot]

    @pl.when(step == 0)
    def _():
        m_scr[...] = jnp.full_like(m_scr, NEG)
        l_scr[...] = jnp.zeros_like(l_scr)
        acc_scr[...] = jnp.zeros_like(acc_scr)

    qs = [_split_halves(q_ref[:, h * E:(h + 1) * E], DH) for h in range(H)]
    head = lax.broadcasted_iota(jnp.int32, (H * NR, 1), 0) // NR
    slope = jnp.zeros((H * NR, 1), F32)
    for h in range(H):
        slope = jnp.where(head == h, slopes[h], slope)

    def update(k_tile, v_tile, n_tiles, kpos, ok):
        s = jnp.concatenate(
            [jnp.concatenate([_dot_nt(qs[h], k_tile(h, j)) for j in range(n_tiles)], axis=1)
             for h in range(H)], axis=0)
        s = s + slope * kpos
        if ok is not None:
            s = jnp.where(ok, s, NEG)
        m_prev = m_scr[...]
        m_new = jnp.maximum(m_prev, jnp.max(s, axis=-1, keepdims=True))
        alpha = jnp.exp2(m_prev - m_new)
        p = jnp.exp2(s - m_new)
        l_scr[...] = alpha * l_scr[...] + jnp.sum(p, axis=-1, keepdims=True)
        pb = p.astype(BF16)
        pv = []
        for h in range(H):
            out = _dot(pb[h * NR:(h + 1) * NR, 0:PAGE], v_tile(h, 0))
            for j in range(1, n_tiles):
                out = out + _dot(pb[h * NR:(h + 1) * NR, j * PAGE:(j + 1) * PAGE], v_tile(h, j))
            pv.append(out)
        acc_scr[...] = alpha * acc_scr[...] + jnp.concatenate(pv, axis=0)
        m_scr[...] = m_new

    col = lax.broadcasted_iota(jnp.int32, (1, PP * PAGE), 1)
    kpos = (step * (PP * PAGE) + col - past).astype(F32)
    update(lambda h, j: k_pages[j, pl.ds(h, PAGE, stride=H), :].astype(BF16),
           lambda h, j: v_pages[j, pl.ds(h, PAGE, stride=H), :].astype(BF16), PP, kpos, None)

    @pl.when(step == pl.num_programs(1) - 1)
    def _():
        tq = lax.broadcasted_iota(jnp.int32, (H * NR, PAGE), 0) & (R - 1)
        coln = lax.broadcasted_iota(jnp.int32, (H * NR, PAGE), 1)
        pad = jnp.zeros((PAGE - R, E), F32)
        update(lambda h, j: jnp.concatenate([kn_ref[pl.ds(h, R, stride=H), :], pad], axis=0).astype(BF16),
               lambda h, j: jnp.concatenate([vn_ref[pl.ds(h, R, stride=H), :], pad], axis=0).astype(BF16),
               1, coln[0:1].astype(F32), (coln <= tq) & (coln < T))
        o2 = acc_scr[...] / l_scr[...]
        for h in range(H):
            o = o2[h * NR:h * NR + R] - lam_ref[0] * o2[h * NR + R:(h + 1) * NR]
            o_ref[:, h * E:(h + 1) * E] = _rms(o, ng_ref[...]) * out_scale


def _diff_sample(q, kn, vn, cache_k, cache_v, page_ids, lam, ng, T, H, DH, out_scale, PP=16):
    B = q.shape[0]
    n_pages = page_ids.shape[1]
    PAGE = cache_k.shape[1] // H
    E = 2 * DH
    R = SAMPLE_ROWS
    slopes = tuple(LOG2E * 2.0 ** (-8.0 * (i + 1) / H) for i in range(H))
    assert n_pages % PP == 0 and q.shape[1] == R
    new_rows = pl.BlockSpec((None, R * H, E), lambda b, s, pt: (b, 0, 0))
    n_steps = n_pages // PP
    page_buf = pltpu.VMEM((RING_SLOTS, PP, PAGE * H, E), F32)
    grid_spec = pltpu.PrefetchScalarGridSpec(
        num_scalar_prefetch=1,
        grid=(B, n_steps),
        in_specs=[pl.BlockSpec(memory_space=pltpu.SMEM),
                  pl.BlockSpec((None, R, H * E), lambda b, s, pt: (b, 0, 0)),
                  new_rows, new_rows,
                  pl.BlockSpec((1, E), lambda b, s, pt: (0, 0)),
                  pl.BlockSpec(memory_space=pl.ANY),
                  pl.BlockSpec(memory_space=pl.ANY)],
        out_specs=pl.BlockSpec((None, R, H * E), lambda b, s, pt: (b, 0, 0)),
        scratch_shapes=[page_buf, page_buf, pltpu.SemaphoreType.DMA((RING_SLOTS,)),
                        pltpu.VMEM((H * 2 * R, 1), F32), pltpu.VMEM((H * 2 * R, 1), F32),
                        pltpu.VMEM((H * 2 * R, E), F32)])
    return pl.pallas_call(
        functools.partial(_diff_sample_kernel, PP=PP, PAGE=PAGE, T=T, H=H, DH=DH,
                          past=n_pages * PAGE, slopes=slopes, out_scale=out_scale,
                          total_steps=B * n_steps),
        grid_spec=grid_spec,
        out_shape=jax.ShapeDtypeStruct((B, R, H * E), F32),
        compiler_params=_params("arbitrary", "arbitrary"),
        name="diff_sample",
    )(page_ids, lam, q, kn, vn, ng.reshape(1, E), cache_k, cache_v)


def _mem_kv_kernel(x_ref, g_ref, wt_ref, kt_ref, vt_ref):
    h = _rms(x_ref[...], g_ref[...]).astype(BF16)
    kv = _dot_nt(wt_ref[...], h)
    w = kt_ref.shape[0]
    kt_ref[...] = kv[:w]
    vt_ref[...] = kv[w:]


def _mem_kv(mem, g, wt_bf16):
    B, N, D = mem.shape
    w = wt_bf16.shape[0] // 2
    out = jax.ShapeDtypeStruct((B, w, N), F32)
    blk = pl.BlockSpec((None, w, N), lambda b: (b, 0, 0))
    return pl.pallas_call(
        _mem_kv_kernel,
        grid=(B,),
        in_specs=[pl.BlockSpec((None, N, D), lambda b: (b, 0, 0)),
                  pl.BlockSpec((1, D), lambda b: (0, 0)),
                  pl.BlockSpec((2 * w, D), lambda b: (0, 0))],
        out_specs=[blk, blk],
        out_shape=[out, out],
        compiler_params=_params("parallel"),
        name="mem_kv",
    )(mem, g.reshape(1, D), wt_bf16)


def _mem_attend(q, kt, vt, H, DH):
    T = q.shape[0]
    q = q * (DH ** -0.5)
    lane_h = lax.broadcasted_iota(jnp.int32, q.shape, 1) // DH
    qs = jnp.concatenate([jnp.where(lane_h == h, q, 0.0) for h in range(H)], axis=0).astype(BF16)
    s = _dot(qs, kt.astype(BF16))
    p = jnp.exp(s - jnp.max(s, axis=-1, keepdims=True))
    o4 = _dot_nt(p.astype(BF16), vt.astype(BF16)) / jnp.sum(p, axis=-1, keepdims=True)
    o = jnp.zeros_like(q)
    for h in range(H):
        o = jnp.where(lane_h == h, o4[h * T:(h + 1) * T], o)
    return o


def _mem_attn_kernel(q_ref, kt_ref, vt_ref, o_ref, *, NB, H, DH):
    for n in range(NB):
        o_ref[n] = _mem_attend(q_ref[n], kt_ref[n], vt_ref[n], H, DH)


def _mem_attn(q, kt, vt, H, DH, NB=4):
    B, T, W = q.shape
    N = kt.shape[2]
    assert B % NB == 0
    kv = pl.BlockSpec((NB, W, N), lambda b: (b, 0, 0))
    return pl.pallas_call(
        functools.partial(_mem_attn_kernel, NB=NB, H=H, DH=DH),
        grid=(B // NB,),
        in_specs=[pl.BlockSpec((NB, T, W), lambda b: (b, 0, 0)), kv, kv],
        out_specs=pl.BlockSpec((NB, T, W), lambda b: (b, 0, 0)),
        out_shape=jax.ShapeDtypeStruct((B, T, W), F32),
        compiler_params=_params("parallel"),
        name="mem_attn",
    )(q, kt, vt)


def _out_kernel(x_ref, oa_ref, ob_ref, m_ref, z_ref, w_ref, g_ref, *rest, final, mem_heads):
    if mem_heads is None:
        (y_ref,) = rest
        om = m_ref[...]
    else:
        kt_ref, vt_ref, y_ref = rest
        om = _mem_attend(m_ref[...], kt_ref[...], vt_ref[...], *mem_heads)
    o = jnp.concatenate([oa_ref[...].astype(F32), ob_ref[...].astype(F32), om], axis=-1)
    z = z_ref[...].astype(F32)
    o = (o * (z * jax.nn.sigmoid(z))).astype(BF16)
    y = x_ref[...] + _dot(o, w_ref[...])
    y_ref[...] = _rms(y, g_ref[...]) if final else y


def _out_proj(x2d, oa, ob, m, z, w_bf16, g, tm, final, mem=None):
    rows_total, d = x2d.shape
    mix = w_bf16.shape[0]

    def rows(a):
        return pl.BlockSpec((tm, a.shape[1]), lambda i: (i, 0))

    in_specs = [rows(x2d), rows(oa), rows(ob), rows(m), rows(z),
                pl.BlockSpec((mix, d), lambda i: (0, 0)),
                pl.BlockSpec((1, d), lambda i: (0, 0))]
    args = [x2d, oa, ob, m, z, w_bf16, g.reshape(1, d)]
    mem_heads = None
    if mem is not None:
        kt, vt, H, DH, rows_per_batch = mem
        assert rows_per_batch % tm == 0
        tiles = rows_per_batch // tm
        kv = pl.BlockSpec((None,) + kt.shape[1:], lambda i: (i // tiles, 0, 0))
        in_specs += [kv, kv]
        args += [kt, vt]
        mem_heads = (H, DH)
    return pl.pallas_call(
        functools.partial(_out_kernel, final=final, mem_heads=mem_heads),
        grid=(rows_total // tm,),
        in_specs=in_specs,
        out_specs=rows(x2d),
        out_shape=jax.ShapeDtypeStruct((rows_total, d), F32),
        compiler_params=_params("parallel"),
        name="out_proj",
    )(*args)


def kernel(x_prompt, x_sample, mem_prompt, cache_k, cache_v, state_hgrn, cache_mem_k, cache_mem_v, page_table, norm_g, w_in, hgrn_lb_logits, hgrn_norm_g, diff_norm_g, lambda_q1, lambda_k1, lambda_q2, lambda_k2, mem_norm_g, w_mem_kv, w_out, final_g):
    B, L, D = x_prompt.shape
    SB, T, _ = x_sample.shape
    depth, _, H_A, DK_A, DV_A = state_hgrn.shape
    _, n_pool, PAGE, H_B, E_B = cache_k.shape
    DH_B = E_B // 2
    _, _, N_MEM, H_M, DH_M = cache_mem_k.shape
    W_A, W_K, W_B, W_M = H_A * DV_A, H_A * DK_A, H_B * E_B, H_M * DH_M
    MIX = W_A + W_B + W_M
    R = SAMPLE_ROWS
    TM = ROW_TILE
    assert T <= R and L % TM == 0
    sizes = (2 * W_K + W_A, W_B, W_M, MIX)
    diff_scale = LOG2E * DH_B ** -0.5

    cache_k3 = cache_k.reshape(depth * n_pool, PAGE * H_B, E_B)
    cache_v3 = cache_v.reshape(depth * n_pool, PAGE * H_B, E_B)

    yp = x_prompt.reshape(B * L, D)
    ys = jnp.pad(x_sample, ((0, 0), (0, R - T), (0, 0))).reshape(SB * R, D)
    outs = {k: [] for k in ("kp", "vp", "sp", "mkp", "mvp", "ks", "vs", "ss")}
    for l in range(depth):
        lam_init = 0.8 - 0.6 * math.exp(-0.3 * l)
        lam = (jnp.exp(jnp.sum(lambda_q1[l] * lambda_k1[l])) - jnp.exp(jnp.sum(lambda_q2[l] * lambda_k2[l]))
               + lam_init).reshape(1).astype(F32)
        w_in_b = w_in[l].astype(BF16)
        w_out_b = w_out[l].astype(BF16)
        w_mem_t = w_mem_kv[l].T.astype(BF16)

        def channel_major(a):
            return jnp.transpose(a, (0, 2, 3, 1)).reshape(a.shape[0], W_M, N_MEM)

        def token_major(a):
            return jnp.transpose(a.reshape(a.shape[0], H_M, DH_M, N_MEM), (0, 3, 1, 2))

        mkt, mvt = _mem_kv(mem_prompt, mem_norm_g[l], w_mem_t)
        pa, dq, kb, vb, ko, vo, mq, z = _in_proj(yp, norm_g[l], w_in_b, sizes, H_B, TM, diff_scale, True,
                                                 "in_proj_prompt")
        oa, st = _hgrn_prompt(pa.reshape(B, L, -1), hgrn_lb_logits, hgrn_norm_g[l], l, H_A, DK_A, DV_A)
        ob = _diff_prompt(dq.reshape(B, L, W_B), kb.reshape(B, L, W_B), vb.reshape(B, L, W_B),
                          lam, diff_norm_g[l], H_B, DH_B, 1.0 - lam_init)
        yp = _out_proj(yp, oa.reshape(B * L, W_A), ob.reshape(B * L, W_B), mq, z,
                       w_out_b, final_g, TM, l == depth - 1, mem=(mkt, mvt, H_M, DH_M, L))
        outs["kp"].append(ko.reshape(B, L, H_B, E_B))
        outs["vp"].append(vo.reshape(B, L, H_B, E_B))
        outs["sp"].append(jnp.swapaxes(st, -1, -2))
        outs["mkp"].append(token_major(mkt))
        outs["mvp"].append(token_major(mvt))

        pa, dq, ko, vo, mq, z = _in_proj(ys, norm_g[l], w_in_b, sizes, H_B, SB * R, diff_scale, False,
                                         "in_proj_sample")
        oa, sst = _hgrn_sample(pa.reshape(SB, R, -1), hgrn_lb_logits, hgrn_norm_g[l],
                               jnp.swapaxes(state_hgrn[l], -1, -2), l, T, H_A, DK_A, DV_A)
        ob = _diff_sample(dq.reshape(SB, R, W_B), ko.reshape(SB, R * H_B, E_B), vo.reshape(SB, R * H_B, E_B),
                          cache_k3, cache_v3, page_table + l * n_pool,
                          lam, diff_norm_g[l], T, H_B, DH_B, 1.0 - lam_init)
        om = _mem_attn(mq.reshape(SB, R, W_M), channel_major(cache_mem_k[l]), channel_major(cache_mem_v[l]),
                       H_M, DH_M)
        ys = _out_proj(ys, oa.reshape(SB * R, W_A), ob.reshape(SB * R, W_B), om.reshape(SB * R, W_M), z,
                       w_out_b, final_g, SB * R, l == depth - 1)
        outs["ks"].append(ko.reshape(SB, R, H_B, E_B)[:, :T])
        outs["vs"].append(vo.reshape(SB, R, H_B, E_B)[:, :T])
        outs["ss"].append(jnp.swapaxes(sst, -1, -2))

    y_prompt = yp.reshape(B, L, D)
    y_sample = ys.reshape(SB, R, D)[:, :T]
    return (y_prompt, y_sample, jnp.stack(outs["kp"]), jnp.stack(outs["vp"]), jnp.stack(outs["sp"]),
            jnp.stack(outs["mkp"]), jnp.stack(outs["mvp"]), jnp.stack(outs["ks"]), jnp.stack(outs["vs"]),
            jnp.stack(outs["ss"]))
```

```python
import functools
import math

import jax
import jax.numpy as jnp
from jax import lax
from jax.experimental import pallas as pl
from jax.experimental.pallas import tpu as pltpu

F32 = jnp.float32
BF16 = jnp.bfloat16
EPS = 1e-6
NEG = -1e30
LOG2E = math.log2(math.e)
VMEM_LIMIT = 48 * 1024 * 1024
SUBLANES = 8
SAMPLE_ROWS = SUBLANES
ROW_TILE = 512
RING_SLOTS = 3

_NT = (((1,), (1,)), ((), ()))
_TN = (((0,), (0,)), ((), ()))


def _dot(a, b):
    return jnp.dot(a, b, preferred_element_type=F32)


def _dot_nt(a, b):
    return lax.dot_general(a, b, _NT, preferred_element_type=F32)


def _dot_tn(a, b):
    return lax.dot_general(a, b, _TN, preferred_element_type=F32)


def _rms(x, g):
    return x * lax.rsqrt(jnp.mean(x * x, axis=-1, keepdims=True) + EPS) * g


def _params(*sem):
    return pltpu.CompilerParams(dimension_semantics=sem, vmem_limit_bytes=VMEM_LIMIT)


def _split_halves(q, dh):
    lane = lax.broadcasted_iota(jnp.int32, q.shape, 1)
    zero = jnp.zeros_like(q)
    return jnp.concatenate([jnp.where(lane < dh, q, zero), jnp.where(lane < dh, zero, q)], axis=0)


def _proj_kernel(x_ref, g_ref, w_ref, *out_refs, col_sizes):
    h = _rms(x_ref[...], g_ref[...]).astype(BF16)
    start = 0
    for ref, size in zip(out_refs, col_sizes):
        ref[...] = _dot(h, w_ref[:, start:start + size]).astype(ref.dtype)
        start += size


def _norm_proj(x2d, g, w_bf16, col_sizes, tm, name):
    m, d = x2d.shape
    n = w_bf16.shape[1]
    assert sum(col_sizes) == n and m % tm == 0
    return pl.pallas_call(
        functools.partial(_proj_kernel, col_sizes=col_sizes),
        grid=(m // tm,),
        in_specs=[pl.BlockSpec((tm, d), lambda i: (i, 0)),
                  pl.BlockSpec((1, d), lambda i: (0, 0)),
                  pl.BlockSpec((d, n), lambda i: (0, 0))],
        out_specs=[pl.BlockSpec((tm, s), lambda i: (i, 0)) for s in col_sizes],
        out_shape=[jax.ShapeDtypeStruct((m, s), F32) for s in col_sizes],
        compiler_params=_params("parallel"),
        name=name,
    )(x2d, g.reshape(1, d), w_bf16)


def _in_proj_kernel(x_ref, g_ref, w_ref, *rest, sizes, H, prompt, scale):
    w_pa, w_b, w_m, mix = sizes
    if prompt:
        pa_ref, dq_ref, kb_ref, vb_ref, ko_ref, vo_ref, mq_ref, z_ref = rest
    else:
        pa_ref, dq_ref, ko_ref, vo_ref, mq_ref, z_ref = rest
    e = w_b // H
    h = _rms(x_ref[...], g_ref[...]).astype(BF16)
    c = 0
    pa_ref[...] = _dot(h, w_ref[:, c:c + w_pa])
    c += w_pa
    dq_ref[...] = (_dot(h, w_ref[:, c:c + w_b]) * scale).astype(BF16)
    c += w_b
    k = _dot(h, w_ref[:, c:c + w_b])
    c += w_b
    v = _dot(h, w_ref[:, c:c + w_b])
    c += w_b
    tm = k.shape[0]
    for hh in range(H):
        ko_ref[pl.ds(hh, tm, stride=H), :] = k[:, hh * e:(hh + 1) * e]
        vo_ref[pl.ds(hh, tm, stride=H), :] = v[:, hh * e:(hh + 1) * e]
    if prompt:
        kb_ref[...] = k.astype(BF16)
        vb_ref[...] = v.astype(BF16)
    mq_ref[...] = _dot(h, w_ref[:, c:c + w_m])
    c += w_m
    z_ref[...] = _dot(h, w_ref[:, c:c + mix]).astype(z_ref.dtype)


def _in_proj(x2d, g, w_bf16, sizes, H, tm, scale, prompt, name):
    m, d = x2d.shape
    n = w_bf16.shape[1]
    w_pa, w_b, w_m, mix = sizes
    e = w_b // H
    assert w_pa + 3 * w_b + w_m + mix == n and m % tm == 0

    def rows(width):
        return pl.BlockSpec((tm, width), lambda i: (i, 0))

    const = lambda i: (0, 0)
    in_specs = [rows(d), pl.BlockSpec((1, d), const), pl.BlockSpec((d, n), const)]
    args = [x2d, g.reshape(1, d), w_bf16]
    out_specs = [rows(w_pa), rows(w_b)]
    out_shape = [jax.ShapeDtypeStruct((m, w_pa), F32), jax.ShapeDtypeStruct((m, w_b), BF16)]
    if prompt:
        out_specs += [rows(w_b), rows(w_b)]
        out_shape += [jax.ShapeDtypeStruct((m, w_b), BF16), jax.ShapeDtypeStruct((m, w_b), BF16)]
    head_rows = pl.BlockSpec((tm * H, e), lambda i: (i, 0))
    out_specs += [head_rows, head_rows, rows(w_m), rows(mix)]
    out_shape += [jax.ShapeDtypeStruct((m * H, e), F32), jax.ShapeDtypeStruct((m * H, e), F32),
                  jax.ShapeDtypeStruct((m, w_m), F32), jax.ShapeDtypeStruct((m, mix), BF16)]
    return pl.pallas_call(
        functools.partial(_in_proj_kernel, sizes=sizes, H=H, prompt=prompt, scale=scale),
        grid=(m // tm,),
        in_specs=in_specs,
        out_specs=out_specs,
        out_shape=out_shape,
        compiler_params=_params("parallel"),
        name=name,
    )(*args)


def _lower_bound(lg, layer, axis=0):
    e = jnp.exp(lg - jnp.max(lg, axis=axis, keepdims=True))
    p = e / jnp.sum(e, axis=axis, keepdims=True)
    head = p[:layer + 1] if axis == 0 else p[:, :layer + 1]
    return jnp.sum(head, axis=axis, keepdims=True)


def _hgrn_chunk_kernel(pa_ref, lg_ref, ng_ref, o_ref, sout_ref, s_scr, b_scr,
                       *, C, NC, layer, H, DK, DV):
    step = pl.program_id(1)

    @pl.when(step == 0)
    def _():
        s_scr[...] = jnp.zeros_like(s_scr)

    W = H * DK
    lb = _lower_bound(lg_ref[...], layer)
    ti = lax.broadcasted_iota(jnp.int32, (C, C), 0)
    si = lax.broadcasted_iota(jnp.int32, (C, C), 1)
    lmat = (si <= ti).astype(BF16)
    xor_lower = jnp.where(ti > si, ti ^ si, 0)
    row = lax.broadcasted_iota(jnp.int32, (C, 1), 0)

    def prepare(n):
        rows = slice(n * C, (n + 1) * C)
        f = lb + (1.0 - lb) * jax.nn.sigmoid(pa_ref[rows, W:2 * W])
        g = jnp.log2(f)
        kk = 1.0 - f
        q = pa_ref[rows, 0:W]
        g_hi = g.astype(BF16)
        r1 = g - g_hi.astype(F32)
        g_mid = r1.astype(BF16)
        g_lo = (r1 - g_mid.astype(F32)).astype(BF16)
        b = _dot(lmat, g_hi) + _dot(lmat, g_mid) + _dot(lmat, g_lo)
        b_scr[rows, :] = b
        bl = b_scr[(n + 1) * C - 1:(n + 1) * C, :]

        levels = []
        m = C // 2
        while m >= 1:
            upper = (row & (2 * m - 1)) >= m
            if m >= SUBLANES:
                side, d = [], []
                for blk in range(C // (2 * m)):
                    lo = slice(blk * 2 * m, blk * 2 * m + m)
                    hi = slice(blk * 2 * m + m, (blk + 1) * 2 * m)
                    bref = b_scr[n * C + blk * 2 * m + m - 1:n * C + blk * 2 * m + m, :]
                    side += [kk[lo], q[hi]]
                    d += [bref - b[lo], b[hi] - bref]
                side = jnp.concatenate(side, axis=0)
                d = jnp.concatenate(d, axis=0)
            else:
                side = jnp.where(upper, q, kk)
                if m == 4:
                    bref = jnp.concatenate(
                        [jnp.broadcast_to(b_scr[n * C + blk * 8 + 3:n * C + blk * 8 + 4, :], (8, W))
                         for blk in range(C // 8)], axis=0)
                    d = -jnp.abs(b - bref)
                elif m == 2:
                    d = jnp.where(upper, g + jnp.where((row & 3) == 3, pltpu.roll(g, 1, 0), 0.0),
                                  jnp.where((row & 3) == 0, pltpu.roll(g, C - 1, 0), 0.0))
                else:
                    d = jnp.where(upper, g, 0.0)
            levels.append((side * jnp.exp2(d)).astype(BF16))
            m //= 2
        q_bf = q.astype(BF16)
        k_bf = kk.astype(BF16)
        intra = []
        for h in range(H):
            ks = slice(h * DK, (h + 1) * DK)
            a = jnp.where(ti == si, _dot_nt(q_bf[:, ks], k_bf[:, ks]), 0.0)
            m = 1
            for x in reversed(levels):
                a = jnp.where(xor_lower >= m, _dot_nt(x[:, ks], x[:, ks]), a)
                m *= 2
            v = pa_ref[rows, 2 * W + h * DV:2 * W + (h + 1) * DV].astype(BF16)
            intra.append((_dot(a.astype(BF16), v), v))
        q_in = (q * jnp.exp2(b)).astype(BF16)
        k_out = (kk * jnp.exp2(bl - b)).astype(BF16)
        return intra, q_in, k_out, jnp.exp2(bl)

    prepared = [prepare(n) for n in range(NC)]
    for n, (intra, q_in, k_out, ebl) in enumerate(prepared):
        rows = slice(n * C, (n + 1) * C)
        for h in range(H):
            ks = slice(h * DK, (h + 1) * DK)
            o_intra, v = intra[h]
            st = s_scr[h]
            o = _dot_nt(q_in[:, ks], st.astype(BF16)) + o_intra
            o_ref[rows, h * DV:(h + 1) * DV] = _rms(o, ng_ref[...]).astype(o_ref.dtype)
            s_scr[h] = ebl[:, ks] * st + _dot_tn(v, k_out[:, ks])

    @pl.when(step == pl.num_programs(1) - 1)
    def _():
        sout_ref[...] = s_scr[...]


def _hgrn_prompt(pa, lg, ng, layer, H, DK, DV, C=128, NC=4):
    B, L, wp = pa.shape
    W = H * DK
    R = C * NC
    assert L % R == 0 and wp == 2 * W + H * DV
    return pl.pallas_call(
        functools.partial(_hgrn_chunk_kernel, C=C, NC=NC, layer=layer, H=H, DK=DK, DV=DV),
        grid=(B, L // R),
        in_specs=[pl.BlockSpec((None, R, wp), lambda b, c: (b, c, 0)),
                  pl.BlockSpec(lg.shape, lambda b, c: (0, 0)),
                  pl.BlockSpec((1, DV), lambda b, c: (0, 0))],
        out_specs=[pl.BlockSpec((None, R, H * DV), lambda b, c: (b, c, 0)),
                   pl.BlockSpec((None, H, DV, DK), lambda b, c: (b, 0, 0, 0))],
        out_shape=[jax.ShapeDtypeStruct((B, L, H * DV), BF16),
                   jax.ShapeDtypeStruct((B, H, DV, DK), F32)],
        scratch_shapes=[pltpu.VMEM((H, DV, DK), F32), pltpu.VMEM((R, W), F32)],
        compiler_params=_params("parallel", "arbitrary"),
        name="hgrn_prompt",
    )(pa, lg, ng.reshape(1, DV))


def _hgrn_sample_kernel(pa_ref, lg_ref, ng_ref, s0_ref, o_ref, sout_ref, *, NB, T, layer, H, DK, DV):
    W = H * DK
    R = pa_ref.shape[1]
    lb = _lower_bound(lg_ref[...], layer)
    row = lax.broadcasted_iota(jnp.int32, (R, 1), 0)
    valid = row < T
    for n in range(NB):
        f = lb + (1.0 - lb) * jax.nn.sigmoid(pa_ref[n, :, W:2 * W])
        g = jnp.where(valid, jnp.log2(f), 0.0)
        kk = jnp.where(valid, 1.0 - f, 0.0)
        q = pa_ref[n, :, 0:W]
        b = g
        sh = 1
        while sh < R:
            b = b + jnp.where(row >= sh, pltpu.roll(b, sh, 0), 0.0)
            sh *= 2
        bl = b[R - 1:R]
        q_in = (q * jnp.exp2(b)).astype(BF16)
        k_out = (kk * jnp.exp2(bl - b)).astype(BF16)
        ebl = jnp.exp2(bl)
        for h in range(H):
            ks = slice(h * DK, (h + 1) * DK)
            v = pa_ref[n, :, 2 * W + h * DV:2 * W + (h + 1) * DV]
            st = s0_ref[n, h]
            o = _dot_nt(q_in[:, ks], st.astype(BF16))
            for s in range(T):
                later = row >= s
                decay = jnp.exp2(jnp.where(later, b[:, ks] - b[s:s + 1, ks], 0.0))
                a_s = jnp.sum(jnp.where(later, q[:, ks] * kk[s:s + 1, ks] * decay, 0.0), axis=1, keepdims=True)
                o = o + a_s * v[s:s + 1, :]
            o_ref[n, :, h * DV:(h + 1) * DV] = _rms(o, ng_ref[...])
            sout_ref[n, h] = ebl[:, ks] * st + _dot_tn(v.astype(BF16), k_out[:, ks])


def _hgrn_sample(pa, lg, ng, s0t, layer, T, H, DK, DV, NB=8):
    B, R, wp = pa.shape
    assert B % NB == 0
    return pl.pallas_call(
        functools.partial(_hgrn_sample_kernel, NB=NB, T=T, layer=layer, H=H, DK=DK, DV=DV),
        grid=(B // NB,),
        in_specs=[pl.BlockSpec((NB, R, wp), lambda b: (b, 0, 0)),
                  pl.BlockSpec(lg.shape, lambda b: (0, 0)),
                  pl.BlockSpec((1, DV), lambda b: (0, 0)),
                  pl.BlockSpec((NB, H, DV, DK), lambda b: (b, 0, 0, 0))],
        out_specs=[pl.BlockSpec((NB, R, H * DV), lambda b: (b, 0, 0)),
                   pl.BlockSpec((NB, H, DV, DK), lambda b: (b, 0, 0, 0))],
        out_shape=[jax.ShapeDtypeStruct((B, R, H * DV), F32),
                   jax.ShapeDtypeStruct((B, H, DV, DK), F32)],
        compiler_params=_params("parallel"),
        name="hgrn_sample",
    )(pa, lg, ng.reshape(1, DV), s0t)


def _diff_prompt_kernel(slope_ref, lam_ref, q_ref, k_ref, v_ref, g_ref, o_ref, m_scr, l_scr, acc_scr,
                        *, TQ, DH, out_scale):
    h = pl.program_id(1)
    qi = pl.program_id(2)
    LANES = m_scr.shape[1]
    slope = slope_ref[h]
    qs = _split_halves(q_ref[...], DH)
    trel = lax.broadcasted_iota(jnp.int32, (2 * TQ, TQ), 0) & (TQ - 1)
    srel = lax.broadcasted_iota(jnp.int32, (2 * TQ, TQ), 1)
    key_bias = slope * lax.broadcasted_iota(jnp.int32, (1, TQ), 1).astype(F32)

    m_scr[...] = jnp.full_like(m_scr, NEG)
    l_scr[...] = jnp.zeros_like(l_scr)
    acc_scr[...] = jnp.zeros_like(acc_scr)

    def block(kj, masked):
        rows = pl.ds(pl.multiple_of(kj * TQ, TQ), TQ)
        s = _dot_nt(qs, k_ref[rows, :])
        s = s + (key_bias + slope * ((kj - qi) * TQ).astype(F32))
        if masked:
            s = jnp.where(srel <= trel, s, NEG)
        m_prev = m_scr[...]
        m_new = jnp.maximum(m_prev, jnp.max(s, axis=1, keepdims=True))
        alpha = jnp.exp2(m_prev - m_new)
        p = jnp.exp2(s - jnp.tile(m_new, (1, TQ // LANES)))
        l_scr[...] = alpha * l_scr[...] + jnp.sum(p, axis=1, keepdims=True)
        acc_scr[...] = alpha * acc_scr[...] + _dot(p.astype(BF16), v_ref[rows, :])
        m_scr[...] = m_new

    def body(kj, carry):
        block(kj, False)
        return carry

    lax.fori_loop(0, qi, body, 0)
    block(qi, True)

    o2 = acc_scr[...] / l_scr[...]
    o = o2[:TQ] - lam_ref[0] * o2[TQ:]
    o_ref[...] = (_rms(o, g_ref[...]) * out_scale).astype(o_ref.dtype)


def _diff_prompt(q, kb, vb, lam, ng, H, DH, out_scale, TQ=512):
    B, L, W = q.shape
    E = 2 * DH
    slopes = jnp.asarray([LOG2E * 2.0 ** (-8.0 * (i + 1) / H) for i in range(H)], F32)
    assert L % TQ == 0 and TQ & (TQ - 1) == 0
    return pl.pallas_call(
        functools.partial(_diff_prompt_kernel, TQ=TQ, DH=DH, out_scale=out_scale),
        grid=(B, H, L // TQ),
        in_specs=[pl.BlockSpec(memory_space=pltpu.SMEM),
                  pl.BlockSpec(memory_space=pltpu.SMEM),
                  pl.BlockSpec((None, TQ, E), lambda b, h, i: (b, i, h)),
                  pl.BlockSpec((None, L, E), lambda b, h, i: (b, 0, h)),
                  pl.BlockSpec((None, L, E), lambda b, h, i: (b, 0, h)),
                  pl.BlockSpec((1, E), lambda b, h, i: (0, 0))],
        out_specs=pl.BlockSpec((None, TQ, E), lambda b, h, i: (b, i, h)),
        out_shape=jax.ShapeDtypeStruct((B, L, W), BF16),
        scratch_shapes=[pltpu.VMEM((2 * TQ, E), F32), pltpu.VMEM((2 * TQ, E), F32),
                        pltpu.VMEM((2 * TQ, E), F32)],
        compiler_params=_params("parallel", "parallel", "arbitrary"),
        name="diff_prompt",
    )(slopes, lam, q, kb, vb, ng.reshape(1, E))


def _diff_sample_kernel(pt_ref, lam_ref, q_ref, kn_ref, vn_ref, ng_ref, ck_ref, cv_ref, o_ref,
                        kbuf, vbuf, sem, m_scr, l_scr, acc_scr,
                        *, PP, PAGE, T, H, DH, past, slopes, out_scale, total_steps):
    step = pl.program_id(1)
    n_steps = pl.num_programs(1)
    lin = pl.program_id(0) * n_steps + step
    E = 2 * DH
    R = SAMPLE_ROWS
    NR = 2 * R

    def page_copies(g, slot):
        seq = g // n_steps
        first = (g % n_steps) * PP
        copies = []
        for j in range(PP):
            page = pt_ref[seq, first + j]
            copies.append(pltpu.make_async_copy(ck_ref.at[page], kbuf.at[slot, j], sem.at[slot]))
            copies.append(pltpu.make_async_copy(cv_ref.at[page], vbuf.at[slot, j], sem.at[slot]))
        return copies

    @pl.when(lin == 0)
    def _():
        for g in range(min(RING_SLOTS - 1, total_steps)):
            for c in page_copies(g, g):
                c.start()

    @pl.when(lin + (RING_SLOTS - 1) < total_steps)
    def _():
        ahead = lin + (RING_SLOTS - 1)
        for c in page_copies(ahead, lax.rem(ahead, RING_SLOTS)):
            c.start()

    slot = lax.rem(lin, RING_SLOTS)
    for c in page_copies(lin, slot):
        c.wait()
    k_pages = kbuf.at[slot]
    v_pages = vbuf.at[slot]

    @pl.when(step == 0)
    def _():
        m_scr[...] = jnp.full_like(m_scr, NEG)
        l_scr[...] = jnp.zeros_like(l_scr)
        acc_scr[...] = jnp.zeros_like(acc_scr)

    qs = [_split_halves(q_ref[:, h * E:(h + 1) * E], DH) for h in range(H)]
    head = lax.broadcasted_iota(jnp.int32, (H * NR, 1), 0) // NR
    slope = jnp.zeros((H * NR, 1), F32)
    for h in range(H):
        slope = jnp.where(head == h, slopes[h], slope)

    def update(k_tile, v_tile, n_tiles, kpos, ok):
        s = jnp.concatenate(
            [jnp.concatenate([_dot_nt(qs[h], k_tile(h, j)) for j in range(n_tiles)], axis=1)
             for h in range(H)], axis=0)
        s = s + slope * kpos
        if ok is not None:
            s = jnp.where(ok, s, NEG)
        m_prev = m_scr[...]
        m_new = jnp.maximum(m_prev, jnp.max(s, axis=-1, keepdims=True))
        alpha = jnp.exp2(m_prev - m_new)
        p = jnp.exp2(s - m_new)
        l_scr[...] = alpha * l_scr[...] + jnp.sum(p, axis=-1, keepdims=True)
        pb = p.astype(BF16)
        pv = []
        for h in range(H):
            out = _dot(pb[h * NR:(h + 1) * NR, 0:PAGE], v_tile(h, 0))
            for j in range(1, n_tiles):
                out = out + _dot(pb[h * NR:(h + 1) * NR, j * PAGE:(j + 1) * PAGE], v_tile(h, j))
            pv.append(out)
        acc_scr[...] = alpha * acc_scr[...] + jnp.concatenate(pv, axis=0)
        m_scr[...] = m_new

    col = lax.broadcasted_iota(jnp.int32, (1, PP * PAGE), 1)
    kpos = (step * (PP * PAGE) + col - past).astype(F32)
    update(lambda h, j: k_pages[j, pl.ds(h, PAGE, stride=H), :].astype(BF16),
           lambda h, j: v_pages[j, pl.ds(h, PAGE, stride=H), :].astype(BF16), PP, kpos, None)

    @pl.when(step == pl.num_programs(1) - 1)
    def _():
        tq = lax.broadcasted_iota(jnp.int32, (H * NR, PAGE), 0) & (R - 1)
        coln = lax.broadcasted_iota(jnp.int32, (H * NR, PAGE), 1)
        pad = jnp.zeros((PAGE - R, E), F32)
        update(lambda h, j: jnp.concatenate([kn_ref[pl.ds(h, R, stride=H), :], pad], axis=0).astype(BF16),
               lambda h, j: jnp.concatenate([vn_ref[pl.ds(h, R, stride=H), :], pad], axis=0).astype(BF16),
               1, coln[0:1].astype(F32), (coln <= tq) & (coln < T))
        o2 = acc_scr[...] / l_scr[...]
        for h in range(H):
            o = o2[h * NR:h * NR + R] - lam_ref[0] * o2[h * NR + R:(h + 1) * NR]
            o_ref[:, h * E:(h + 1) * E] = _rms(o, ng_ref[...]) * out_scale


def _diff_sample(q, kn, vn, cache_k, cache_v, page_ids, lam, ng, T, H, DH, out_scale, PP=16):
    B = q.shape[0]
    n_pages = page_ids.shape[1]
    PAGE = cache_k.shape[1] // H
    E = 2 * DH
    R = SAMPLE_ROWS
    slopes = tuple(LOG2E * 2.0 ** (-8.0 * (i + 1) / H) for i in range(H))
    assert n_pages % PP == 0 and q.shape[1] == R
    new_rows = pl.BlockSpec((None, R * H, E), lambda b, s, pt: (b, 0, 0))
    n_steps = n_pages // PP
    page_buf = pltpu.VMEM((RING_SLOTS, PP, PAGE * H, E), F32)
    grid_spec = pltpu.PrefetchScalarGridSpec(
        num_scalar_prefetch=1,
        grid=(B, n_steps),
        in_specs=[pl.BlockSpec(memory_space=pltpu.SMEM),
                  pl.BlockSpec((None, R, H * E), lambda b, s, pt: (b, 0, 0)),
                  new_rows, new_rows,
                  pl.BlockSpec((1, E), lambda b, s, pt: (0, 0)),
                  pl.BlockSpec(memory_space=pl.ANY),
                  pl.BlockSpec(memory_space=pl.ANY)],
        out_specs=pl.BlockSpec((None, R, H * E), lambda b, s, pt: (b, 0, 0)),
        scratch_shapes=[page_buf, page_buf, pltpu.SemaphoreType.DMA((RING_SLOTS,)),
                        pltpu.VMEM((H * 2 * R, 1), F32), pltpu.VMEM((H * 2 * R, 1), F32),
                        pltpu.VMEM((H * 2 * R, E), F32)])
    return pl.pallas_call(
        functools.partial(_diff_sample_kernel, PP=PP, PAGE=PAGE, T=T, H=H, DH=DH,
                          past=n_pages * PAGE, slopes=slopes, out_scale=out_scale,
                          total_steps=B * n_steps),
        grid_spec=grid_spec,
        out_shape=jax.ShapeDtypeStruct((B, R, H * E), F32),
        compiler_params=_params("arbitrary", "arbitrary"),
        name="diff_sample",
    )(page_ids, lam, q, kn, vn, ng.reshape(1, E), cache_k, cache_v)


def _mem_kv_kernel(x_ref, g_ref, wt_ref, kt_ref, vt_ref):
    h = _rms(x_ref[...], g_ref[...]).astype(BF16)
    kv = _dot_nt(wt_ref[...], h)
    w = kt_ref.shape[0]
    kt_ref[...] = kv[:w]
    vt_ref[...] = kv[w:]


def _mem_kv(mem, g, wt_bf16):
    B, N, D = mem.shape
    w = wt_bf16.shape[0] // 2
    out = jax.ShapeDtypeStruct((B, w, N), F32)
    blk = pl.BlockSpec((None, w, N), lambda b: (b, 0, 0))
    return pl.pallas_call(
        _mem_kv_kernel,
        grid=(B,),
        in_specs=[pl.BlockSpec((None, N, D), lambda b: (b, 0, 0)),
                  pl.BlockSpec((1, D), lambda b: (0, 0)),
                  pl.BlockSpec((2 * w, D), lambda b: (0, 0))],
        out_specs=[blk, blk],
        out_shape=[out, out],
        compiler_params=_params("parallel"),
        name="mem_kv",
    )(mem, g.reshape(1, D), wt_bf16)


def _mem_attend(q, kt, vt, H, DH):
    T = q.shape[0]
    q = q * (LOG2E * DH ** -0.5)
    lane_h = lax.broadcasted_iota(jnp.int32, q.shape, 1) // DH
    qs = jnp.concatenate([jnp.where(lane_h == h, q, 0.0) for h in range(H)], axis=0).astype(BF16)
    s = _dot(qs, kt.astype(BF16))
    p = jnp.exp2(s - jnp.max(s, axis=-1, keepdims=True))
    o4 = _dot_nt(p.astype(BF16), vt.astype(BF16)) / jnp.sum(p, axis=-1, keepdims=True)
    o = jnp.zeros_like(q)
    for h in range(H):
        o = jnp.where(lane_h == h, o4[h * T:(h + 1) * T], o)
    return o


def _mem_attn_kernel(q_ref, kt_ref, vt_ref, o_ref, *, NB, H, DH):
    for n in range(NB):
        o_ref[n] = _mem_attend(q_ref[n], kt_ref[n], vt_ref[n], H, DH)


def _mem_attn(q, kt, vt, H, DH, NB=8):
    B, T, W = q.shape
    N = kt.shape[2]
    assert B % NB == 0
    kv = pl.BlockSpec((NB, W, N), lambda b: (b, 0, 0))
    return pl.pallas_call(
        functools.partial(_mem_attn_kernel, NB=NB, H=H, DH=DH),
        grid=(B // NB,),
        in_specs=[pl.BlockSpec((NB, T, W), lambda b: (b, 0, 0)), kv, kv],
        out_specs=pl.BlockSpec((NB, T, W), lambda b: (b, 0, 0)),
        out_shape=jax.ShapeDtypeStruct((B, T, W), F32),
        compiler_params=_params("parallel"),
        name="mem_attn",
    )(q, kt, vt)


def _out_kernel(x_ref, oa_ref, ob_ref, m_ref, z_ref, w_ref, g_ref, *rest, final, mem_heads):
    if mem_heads is None:
        (y_ref,) = rest
        om = m_ref[...]
    else:
        kt_ref, vt_ref, y_ref = rest
        om = _mem_attend(m_ref[...], kt_ref[...], vt_ref[...], *mem_heads)
    o = jnp.concatenate([oa_ref[...].astype(F32), ob_ref[...].astype(F32), om], axis=-1)
    z = z_ref[...].astype(F32)
    o = (o * (z * jax.nn.sigmoid(z))).astype(BF16)
    y = x_ref[...] + _dot(o, w_ref[...])
    y_ref[...] = _rms(y, g_ref[...]) if final else y


def _out_proj(x2d, oa, ob, m, z, w_bf16, g, tm, final, mem=None):
    rows_total, d = x2d.shape
    mix = w_bf16.shape[0]

    def rows(a):
        return pl.BlockSpec((tm, a.shape[1]), lambda i: (i, 0))

    in_specs = [rows(x2d), rows(oa), rows(ob), rows(m), rows(z),
                pl.BlockSpec((mix, d), lambda i: (0, 0)),
                pl.BlockSpec((1, d), lambda i: (0, 0))]
    args = [x2d, oa, ob, m, z, w_bf16, g.reshape(1, d)]
    mem_heads = None
    if mem is not None:
        kt, vt, H, DH, rows_per_batch = mem
        assert rows_per_batch % tm == 0
        tiles = rows_per_batch // tm
        kv = pl.BlockSpec((None,) + kt.shape[1:], lambda i: (i // tiles, 0, 0))
        in_specs += [kv, kv]
        args += [kt, vt]
        mem_heads = (H, DH)
    return pl.pallas_call(
        functools.partial(_out_kernel, final=final, mem_heads=mem_heads),
        grid=(rows_total // tm,),
        in_specs=in_specs,
        out_specs=rows(x2d),
        out_shape=jax.ShapeDtypeStruct((rows_total, d), F32),
        compiler_params=_params("parallel"),
        name="out_proj",
    )(*args)


def kernel(x_prompt, x_sample, mem_prompt, cache_k, cache_v, state_hgrn, cache_mem_k, cache_mem_v, page_table, norm_g, w_in, hgrn_lb_logits, hgrn_norm_g, diff_norm_g, lambda_q1, lambda_k1, lambda_q2, lambda_k2, mem_norm_g, w_mem_kv, w_out, final_g):
    B, L, D = x_prompt.shape
    SB, T, _ = x_sample.shape
    depth, _, H_A, DK_A, DV_A = state_hgrn.shape
    _, n_pool, PAGE, H_B, E_B = cache_k.shape
    DH_B = E_B // 2
    _, _, N_MEM, H_M, DH_M = cache_mem_k.shape
    W_A, W_K, W_B, W_M = H_A * DV_A, H_A * DK_A, H_B * E_B, H_M * DH_M
    MIX = W_A + W_B + W_M
    R = SAMPLE_ROWS
    TM = ROW_TILE
    assert T <= R and L % TM == 0
    sizes = (2 * W_K + W_A, W_B, W_M, MIX)
    diff_scale = LOG2E * DH_B ** -0.5

    cache_k3 = cache_k.reshape(depth * n_pool, PAGE * H_B, E_B)
    cache_v3 = cache_v.reshape(depth * n_pool, PAGE * H_B, E_B)

    yp = x_prompt.reshape(B * L, D)
    ys = jnp.pad(x_sample, ((0, 0), (0, R - T), (0, 0))).reshape(SB * R, D)
    outs = {k: [] for k in ("kp", "vp", "sp", "mkp", "mvp", "ks", "vs", "ss")}
    for l in range(depth):
        lam_init = 0.8 - 0.6 * math.exp(-0.3 * l)
        lam = (jnp.exp(jnp.sum(lambda_q1[l] * lambda_k1[l])) - jnp.exp(jnp.sum(lambda_q2[l] * lambda_k2[l]))
               + lam_init).reshape(1).astype(F32)
        w_in_b = w_in[l].astype(BF16)
        w_out_b = w_out[l].astype(BF16)
        w_mem_t = w_mem_kv[l].T.astype(BF16)

        def channel_major(a):
            return jnp.transpose(a, (0, 2, 3, 1)).reshape(a.shape[0], W_M, N_MEM)

        def token_major(a):
            return jnp.transpose(a.reshape(a.shape[0], H_M, DH_M, N_MEM), (0, 3, 1, 2))

        mkt, mvt = _mem_kv(mem_prompt, mem_norm_g[l], w_mem_t)
        pa, dq, kb, vb, ko, vo, mq, z = _in_proj(yp, norm_g[l], w_in_b, sizes, H_B, TM, diff_scale, True,
                                                 "in_proj_prompt")
        oa, st = _hgrn_prompt(pa.reshape(B, L, -1), hgrn_lb_logits, hgrn_norm_g[l], l, H_A, DK_A, DV_A)
        ob = _diff_prompt(dq.reshape(B, L, W_B), kb.reshape(B, L, W_B), vb.reshape(B, L, W_B),
                          lam, diff_norm_g[l], H_B, DH_B, 1.0 - lam_init)
        yp = _out_proj(yp, oa.reshape(B * L, W_A), ob.reshape(B * L, W_B), mq, z,
                       w_out_b, final_g, TM, l == depth - 1, mem=(mkt, mvt, H_M, DH_M, L))
        outs["kp"].append(ko.reshape(B, L, H_B, E_B))
        outs["vp"].append(vo.reshape(B, L, H_B, E_B))
        outs["sp"].append(jnp.swapaxes(st, -1, -2))
        outs["mkp"].append(token_major(mkt))
        outs["mvp"].append(token_major(mvt))

        pa, dq, ko, vo, mq, z = _in_proj(ys, norm_g[l], w_in_b, sizes, H_B, SB * R, diff_scale, False,
                                         "in_proj_sample")
        oa, sst = _hgrn_sample(pa.reshape(SB, R, -1), hgrn_lb_logits, hgrn_norm_g[l],
                               jnp.swapaxes(state_hgrn[l], -1, -2), l, T, H_A, DK_A, DV_A)
        ob = _diff_sample(dq.reshape(SB, R, W_B), ko.reshape(SB, R * H_B, E_B), vo.reshape(SB, R * H_B, E_B),
                          cache_k3, cache_v3, page_table + l * n_pool,
                          lam, diff_norm_g[l], T, H_B, DH_B, 1.0 - lam_init)
        om = _mem_attn(mq.reshape(SB, R, W_M), channel_major(cache_mem_k[l]), channel_major(cache_mem_v[l]),
                       H_M, DH_M)
        ys = _out_proj(ys, oa.reshape(SB * R, W_A), ob.reshape(SB * R, W_B), om.reshape(SB * R, W_M), z,
                       w_out_b, final_g, SB * R, l == depth - 1)
        outs["ks"].append(ko.reshape(SB, R, H_B, E_B)[:, :T])
        outs["vs"].append(vo.reshape(SB, R, H_B, E_B)[:, :T])
        outs["ss"].append(jnp.swapaxes(sst, -1, -2))

    y_prompt = yp.reshape(B, L, D)
    y_sample = ys.reshape(SB, R, D)[:, :T]
    return (y_prompt, y_sample, jnp.stack(outs["kp"]), jnp.stack(outs["vp"]), jnp.stack(outs["sp"]),
            jnp.stack(outs["mkp"]), jnp.stack(outs["mvp"]), jnp.stack(outs["ks"]), jnp.stack(outs["vs"]),
            jnp.stack(outs["ss"]))
```

```python
import functools
import math

import jax
import jax.numpy as jnp
from jax import lax
from jax.experimental import pallas as pl
from jax.experimental.pallas import tpu as pltpu

F32 = jnp.float32
BF16 = jnp.bfloat16
EPS = 1e-6
NEG = -1e30
LOG2E = math.log2(math.e)
VMEM_LIMIT = 48 * 1024 * 1024
SUBLANES = 8
SAMPLE_ROWS = SUBLANES
ROW_TILE = 512
RING_SLOTS = 3

_NT = (((1,), (1,)), ((), ()))
_TN = (((0,), (0,)), ((), ()))


def _dot(a, b):
    return jnp.dot(a, b, preferred_element_type=F32)


def _dot_nt(a, b):
    return lax.dot_general(a, b, _NT, preferred_element_type=F32)


def _dot_tn(a, b):
    return lax.dot_general(a, b, _TN, preferred_element_type=F32)


def _rms(x, g):
    return x * lax.rsqrt(jnp.mean(x * x, axis=-1, keepdims=True) + EPS) * g


def _params(*sem):
    return pltpu.CompilerParams(dimension_semantics=sem, vmem_limit_bytes=VMEM_LIMIT)


def _split_halves(q, dh):
    lane = lax.broadcasted_iota(jnp.int32, q.shape, 1)
    zero = jnp.zeros_like(q)
    return jnp.concatenate([jnp.where(lane < dh, q, zero), jnp.where(lane < dh, zero, q)], axis=0)


def _proj_kernel(x_ref, g_ref, w_ref, *out_refs, col_sizes):
    h = _rms(x_ref[...], g_ref[...]).astype(BF16)
    start = 0
    for ref, size in zip(out_refs, col_sizes):
        ref[...] = _dot(h, w_ref[:, start:start + size]).astype(ref.dtype)
        start += size


def _norm_proj(x2d, g, w_bf16, col_sizes, tm, name):
    m, d = x2d.shape
    n = w_bf16.shape[1]
    assert sum(col_sizes) == n and m % tm == 0
    return pl.pallas_call(
        functools.partial(_proj_kernel, col_sizes=col_sizes),
        grid=(m // tm,),
        in_specs=[pl.BlockSpec((tm, d), lambda i: (i, 0)),
                  pl.BlockSpec((1, d), lambda i: (0, 0)),
                  pl.BlockSpec((d, n), lambda i: (0, 0))],
        out_specs=[pl.BlockSpec((tm, s), lambda i: (i, 0)) for s in col_sizes],
        out_shape=[jax.ShapeDtypeStruct((m, s), F32) for s in col_sizes],
        compiler_params=_params("parallel"),
        name=name,
    )(x2d, g.reshape(1, d), w_bf16)


def _in_proj_kernel(x_ref, g_ref, w_ref, *rest, sizes, H, prompt, scale):
    w_pa, w_b, w_m, mix = sizes
    if prompt:
        pa_ref, dq_ref, kb_ref, vb_ref, ko_ref, vo_ref, mq_ref, z_ref = rest
    else:
        pa_ref, dq_ref, ko_ref, vo_ref, mq_ref, z_ref = rest
    e = w_b // H
    h = _rms(x_ref[...], g_ref[...]).astype(BF16)
    c = 0
    pa_ref[...] = _dot(h, w_ref[:, c:c + w_pa])
    c += w_pa
    dq_ref[...] = (_dot(h, w_ref[:, c:c + w_b]) * scale).astype(BF16)
    c += w_b
    k = _dot(h, w_ref[:, c:c + w_b])
    c += w_b
    v = _dot(h, w_ref[:, c:c + w_b])
    c += w_b
    tm = k.shape[0]
    for hh in range(H):
        ko_ref[pl.ds(hh, tm, stride=H), :] = k[:, hh * e:(hh + 1) * e]
        vo_ref[pl.ds(hh, tm, stride=H), :] = v[:, hh * e:(hh + 1) * e]
    if prompt:
        kb_ref[...] = k.astype(BF16)
        vb_ref[...] = v.astype(BF16)
    mq_ref[...] = _dot(h, w_ref[:, c:c + w_m])
    c += w_m
    z_ref[...] = _dot(h, w_ref[:, c:c + mix]).astype(z_ref.dtype)


def _in_proj(x2d, g, w_bf16, sizes, H, tm, scale, prompt, name):
    m, d = x2d.shape
    n = w_bf16.shape[1]
    w_pa, w_b, w_m, mix = sizes
    e = w_b // H
    assert w_pa + 3 * w_b + w_m + mix == n and m % tm == 0

    def rows(width):
        return pl.BlockSpec((tm, width), lambda i: (i, 0))

    const = lambda i: (0, 0)
    in_specs = [rows(d), pl.BlockSpec((1, d), const), pl.BlockSpec((d, n), const)]
    args = [x2d, g.reshape(1, d), w_bf16]
    out_specs = [rows(w_pa), rows(w_b)]
    out_shape = [jax.ShapeDtypeStruct((m, w_pa), F32), jax.ShapeDtypeStruct((m, w_b), BF16)]
    if prompt:
        out_specs += [rows(w_b), rows(w_b)]
        out_shape += [jax.ShapeDtypeStruct((m, w_b), BF16), jax.ShapeDtypeStruct((m, w_b), BF16)]
    head_rows = pl.BlockSpec((tm * H, e), lambda i: (i, 0))
    out_specs += [head_rows, head_rows, rows(w_m), rows(mix)]
    out_shape += [jax.ShapeDtypeStruct((m * H, e), F32), jax.ShapeDtypeStruct((m * H, e), F32),
                  jax.ShapeDtypeStruct((m, w_m), F32), jax.ShapeDtypeStruct((m, mix), BF16)]
    return pl.pallas_call(
        functools.partial(_in_proj_kernel, sizes=sizes, H=H, prompt=prompt, scale=scale),
        grid=(m // tm,),
        in_specs=in_specs,
        out_specs=out_specs,
        out_shape=out_shape,
        compiler_params=_params("parallel"),
        name=name,
    )(*args)


def _lower_bound(lg, layer, axis=0):
    e = jnp.exp(lg - jnp.max(lg, axis=axis, keepdims=True))
    p = e / jnp.sum(e, axis=axis, keepdims=True)
    head = p[:layer + 1] if axis == 0 else p[:, :layer + 1]
    return jnp.sum(head, axis=axis, keepdims=True)


def _hgrn_chunk_kernel(pa_ref, lg_ref, ng_ref, o_ref, sout_ref, s_scr, b_scr,
                       *, C, NC, layer, H, DK, DV):
    step = pl.program_id(1)

    @pl.when(step == 0)
    def _():
        s_scr[...] = jnp.zeros_like(s_scr)

    W = H * DK
    lb = _lower_bound(lg_ref[...], layer)
    ti = lax.broadcasted_iota(jnp.int32, (C, C), 0)
    si = lax.broadcasted_iota(jnp.int32, (C, C), 1)
    lmat = (si <= ti).astype(BF16)
    xor_lower = jnp.where(ti > si, ti ^ si, 0)
    row = lax.broadcasted_iota(jnp.int32, (C, 1), 0)

    def prepare(n):
        rows = slice(n * C, (n + 1) * C)
        f = lb + (1.0 - lb) * jax.nn.sigmoid(pa_ref[rows, W:2 * W])
        g = jnp.log2(f)
        kk = 1.0 - f
        q = pa_ref[rows, 0:W]
        g_hi = g.astype(BF16)
        r1 = g - g_hi.astype(F32)
        g_mid = r1.astype(BF16)
        g_lo = (r1 - g_mid.astype(F32)).astype(BF16)
        b = _dot(lmat, g_hi) + _dot(lmat, g_mid) + _dot(lmat, g_lo)
        b_scr[rows, :] = b
        bl = b_scr[(n + 1) * C - 1:(n + 1) * C, :]

        levels = []
        m = C // 2
        while m >= 1:
            upper = (row & (2 * m - 1)) >= m
            if m >= SUBLANES:
                side, d = [], []
                for blk in range(C // (2 * m)):
                    lo = slice(blk * 2 * m, blk * 2 * m + m)
                    hi = slice(blk * 2 * m + m, (blk + 1) * 2 * m)
                    bref = b_scr[n * C + blk * 2 * m + m - 1:n * C + blk * 2 * m + m, :]
                    side += [kk[lo], q[hi]]
                    d += [bref - b[lo], b[hi] - bref]
                side = jnp.concatenate(side, axis=0)
                d = jnp.concatenate(d, axis=0)
            else:
                side = jnp.where(upper, q, kk)
                if m == 4:
                    bref = jnp.concatenate(
                        [jnp.broadcast_to(b_scr[n * C + blk * 8 + 3:n * C + blk * 8 + 4, :], (8, W))
                         for blk in range(C // 8)], axis=0)
                    d = -jnp.abs(b - bref)
                elif m == 2:
                    d = jnp.where(upper, g + jnp.where((row & 3) == 3, pltpu.roll(g, 1, 0), 0.0),
                                  jnp.where((row & 3) == 0, pltpu.roll(g, C - 1, 0), 0.0))
                else:
                    d = jnp.where(upper, g, 0.0)
            levels.append((side * jnp.exp2(d)).astype(BF16))
            m //= 2
        q_bf = q.astype(BF16)
        k_bf = kk.astype(BF16)
        intra = []
        for h in range(H):
            ks = slice(h * DK, (h + 1) * DK)
            a = jnp.where(ti == si, _dot_nt(q_bf[:, ks], k_bf[:, ks]), 0.0)
            m = 1
            for x in reversed(levels):
                a = jnp.where(xor_lower >= m, _dot_nt(x[:, ks], x[:, ks]), a)
                m *= 2
            v = pa_ref[rows, 2 * W + h * DV:2 * W + (h + 1) * DV].astype(BF16)
            intra.append((_dot(a.astype(BF16), v), v))
        q_in = (q * jnp.exp2(b)).astype(BF16)
        k_out = (kk * jnp.exp2(bl - b)).astype(BF16)
        return intra, q_in, k_out, jnp.exp2(bl)

    prepared = [prepare(n) for n in range(NC)]
    for n, (intra, q_in, k_out, ebl) in enumerate(prepared):
        rows = slice(n * C, (n + 1) * C)
        for h in range(H):
            ks = slice(h * DK, (h + 1) * DK)
            o_intra, v = intra[h]
            st = s_scr[h]
            o = _dot_nt(q_in[:, ks], st.astype(BF16)) + o_intra
            o_ref[rows, h * DV:(h + 1) * DV] = _rms(o, ng_ref[...]).astype(o_ref.dtype)
            s_scr[h] = ebl[:, ks] * st + _dot_tn(v, k_out[:, ks])

    @pl.when(step == pl.num_programs(1) - 1)
    def _():
        sout_ref[...] = s_scr[...]


def _hgrn_prompt(pa, lg, ng, layer, H, DK, DV, C=128, NC=4):
    B, L, wp = pa.shape
    W = H * DK
    R = C * NC
    assert L % R == 0 and wp == 2 * W + H * DV
    return pl.pallas_call(
        functools.partial(_hgrn_chunk_kernel, C=C, NC=NC, layer=layer, H=H, DK=DK, DV=DV),
        grid=(B, L // R),
        in_specs=[pl.BlockSpec((None, R, wp), lambda b, c: (b, c, 0)),
                  pl.BlockSpec(lg.shape, lambda b, c: (0, 0)),
                  pl.BlockSpec((1, DV), lambda b, c: (0, 0))],
        out_specs=[pl.BlockSpec((None, R, H * DV), lambda b, c: (b, c, 0)),
                   pl.BlockSpec((None, H, DV, DK), lambda b, c: (b, 0, 0, 0))],
        out_shape=[jax.ShapeDtypeStruct((B, L, H * DV), BF16),
                   jax.ShapeDtypeStruct((B, H, DV, DK), F32)],
        scratch_shapes=[pltpu.VMEM((H, DV, DK), F32), pltpu.VMEM((R, W), F32)],
        compiler_params=_params("parallel", "arbitrary"),
        name="hgrn_prompt",
    )(pa, lg, ng.reshape(1, DV))


def _hgrn_sample_kernel(pa_ref, lg_ref, ng_ref, s0_ref, o_ref, sout_ref, *, NB, T, layer, H, DK, DV):
    W = H * DK
    R = pa_ref.shape[1]
    lb = _lower_bound(lg_ref[...], layer)
    row = lax.broadcasted_iota(jnp.int32, (R, 1), 0)
    valid = row < T
    for n in range(NB):
        f = lb + (1.0 - lb) * jax.nn.sigmoid(pa_ref[n, :, W:2 * W])
        g = jnp.where(valid, jnp.log2(f), 0.0)
        kk = jnp.where(valid, 1.0 - f, 0.0)
        q = pa_ref[n, :, 0:W]
        b = g
        sh = 1
        while sh < R:
            b = b + jnp.where(row >= sh, pltpu.roll(b, sh, 0), 0.0)
            sh *= 2
        bl = b[R - 1:R]
        q_in = (q * jnp.exp2(b)).astype(BF16)
        k_out = (kk * jnp.exp2(bl - b)).astype(BF16)
        ebl = jnp.exp2(bl)
        for h in range(H):
            ks = slice(h * DK, (h + 1) * DK)
            v = pa_ref[n, :, 2 * W + h * DV:2 * W + (h + 1) * DV]
            st = s0_ref[n, h]
            o = _dot_nt(q_in[:, ks], st.astype(BF16))
            for s in range(T):
                later = row >= s
                decay = jnp.exp2(jnp.where(later, b[:, ks] - b[s:s + 1, ks], 0.0))
                a_s = jnp.sum(jnp.where(later, q[:, ks] * kk[s:s + 1, ks] * decay, 0.0), axis=1, keepdims=True)
                o = o + a_s * v[s:s + 1, :]
            o_ref[n, :, h * DV:(h + 1) * DV] = _rms(o, ng_ref[...])
            sout_ref[n, h] = ebl[:, ks] * st + _dot_tn(v.astype(BF16), k_out[:, ks])


def _hgrn_sample(pa, lg, ng, s0t, layer, T, H, DK, DV, NB=4):
    B, R, wp = pa.shape
    assert B % NB == 0
    return pl.pallas_call(
        functools.partial(_hgrn_sample_kernel, NB=NB, T=T, layer=layer, H=H, DK=DK, DV=DV),
        grid=(B // NB,),
        in_specs=[pl.BlockSpec((NB, R, wp), lambda b: (b, 0, 0)),
                  pl.BlockSpec(lg.shape, lambda b: (0, 0)),
                  pl.BlockSpec((1, DV), lambda b: (0, 0)),
                  pl.BlockSpec((NB, H, DV, DK), lambda b: (b, 0, 0, 0))],
        out_specs=[pl.BlockSpec((NB, R, H * DV), lambda b: (b, 0, 0)),
                   pl.BlockSpec((NB, H, DV, DK), lambda b: (b, 0, 0, 0))],
        out_shape=[jax.ShapeDtypeStruct((B, R, H * DV), F32),
                   jax.ShapeDtypeStruct((B, H, DV, DK), F32)],
        compiler_params=_params("parallel"),
        name="hgrn_sample",
    )(pa, lg, ng.reshape(1, DV), s0t)


def _diff_prompt_kernel(slope_ref, lam_ref, q_ref, k_ref, v_ref, g_ref, o_ref, m_scr, l_scr, acc_scr,
                        *, TQ, DH, HP, out_scale):
    hg = pl.program_id(1)
    qi = pl.program_id(2)
    E = 2 * DH
    LANES = m_scr.shape[2]
    trel = lax.broadcasted_iota(jnp.int32, (2 * TQ, TQ), 0) & (TQ - 1)
    srel = lax.broadcasted_iota(jnp.int32, (2 * TQ, TQ), 1)
    key_pos = lax.broadcasted_iota(jnp.int32, (1, TQ), 1).astype(F32)
    slopes = [slope_ref[hg * HP + j] for j in range(HP)]
    qs = [_split_halves(q_ref[:, j * E:(j + 1) * E], DH) for j in range(HP)]

    m_scr[...] = jnp.full_like(m_scr, NEG)
    l_scr[...] = jnp.zeros_like(l_scr)
    acc_scr[...] = jnp.zeros_like(acc_scr)

    def block(kj, masked):
        rows = pl.ds(pl.multiple_of(kj * TQ, TQ), TQ)
        for j in range(HP):
            cols = slice(j * E, (j + 1) * E)
            s = _dot_nt(qs[j], k_ref[rows, cols])
            s = s + slopes[j] * (key_pos + ((kj - qi) * TQ).astype(F32))
            if masked:
                s = jnp.where(srel <= trel, s, NEG)
            m_prev = m_scr[j]
            m_new = jnp.maximum(m_prev, jnp.max(s, axis=1, keepdims=True))
            alpha = jnp.exp2(m_prev - m_new)
            p = jnp.exp2(s - jnp.tile(m_new, (1, TQ // LANES)))
            l_scr[j] = alpha * l_scr[j] + jnp.sum(p, axis=1, keepdims=True)
            acc_scr[j] = alpha * acc_scr[j] + _dot(p.astype(BF16), v_ref[rows, cols])
            m_scr[j] = m_new

    def body(kj, carry):
        block(kj, False)
        return carry

    lax.fori_loop(0, qi, body, 0)
    block(qi, True)

    for j in range(HP):
        o2 = acc_scr[j] / l_scr[j]
        o = o2[:TQ] - lam_ref[0] * o2[TQ:]
        o_ref[:, j * E:(j + 1) * E] = (_rms(o, g_ref[...]) * out_scale).astype(o_ref.dtype)


def _diff_prompt(q, kb, vb, lam, ng, H, DH, out_scale, TQ=512, HP=4):
    B, L, W = q.shape
    E = 2 * DH
    slopes = jnp.asarray([LOG2E * 2.0 ** (-8.0 * (i + 1) / H) for i in range(H)], F32)
    assert L % TQ == 0 and TQ & (TQ - 1) == 0 and H % HP == 0
    return pl.pallas_call(
        functools.partial(_diff_prompt_kernel, TQ=TQ, DH=DH, HP=HP, out_scale=out_scale),
        grid=(B, H // HP, L // TQ),
        in_specs=[pl.BlockSpec(memory_space=pltpu.SMEM),
                  pl.BlockSpec(memory_space=pltpu.SMEM),
                  pl.BlockSpec((None, TQ, HP * E), lambda b, h, i: (b, i, h)),
                  pl.BlockSpec((None, L, HP * E), lambda b, h, i: (b, 0, h)),
                  pl.BlockSpec((None, L, HP * E), lambda b, h, i: (b, 0, h)),
                  pl.BlockSpec((1, E), lambda b, h, i: (0, 0))],
        out_specs=pl.BlockSpec((None, TQ, HP * E), lambda b, h, i: (b, i, h)),
        out_shape=jax.ShapeDtypeStruct((B, L, W), BF16),
        scratch_shapes=[pltpu.VMEM((HP, 2 * TQ, E), F32), pltpu.VMEM((HP, 2 * TQ, E), F32),
                        pltpu.VMEM((HP, 2 * TQ, E), F32)],
        compiler_params=_params("parallel", "parallel", "arbitrary"),
        name="diff_prompt",
    )(slopes, lam, q, kb, vb, ng.reshape(1, E))


def _diff_sample_kernel(pt_ref, lam_ref, q_ref, kn_ref, vn_ref, ng_ref, ck_ref, cv_ref, o_ref,
                        kbuf, vbuf, sem, m_scr, l_scr, acc_scr,
                        *, PP, PAGE, T, H, DH, past, slopes, out_scale, total_steps):
    step = pl.program_id(1)
    n_steps = pl.num_programs(1)
    lin = pl.program_id(0) * n_steps + step
    E = 2 * DH
    R = SAMPLE_ROWS
    NR = 2 * R

    def page_copies(g, slot):
        seq = g // n_steps
        first = (g % n_steps) * PP
        copies = []
        for j in range(PP):
            page = pt_ref[seq, first + j]
            copies.append(pltpu.make_async_copy(ck_ref.at[page], kbuf.at[slot, j], sem.at[slot]))
            copies.append(pltpu.make_async_copy(cv_ref.at[page], vbuf.at[slot, j], sem.at[slot]))
        return copies

    @pl.when(lin == 0)
    def _():
        for g in range(min(RING_SLOTS - 1, total_steps)):
            for c in page_copies(g, g):
                c.start()

    @pl.when(lin + (RING_SLOTS - 1) < total_steps)
    def _():
        ahead = lin + (RING_SLOTS - 1)
        for c in page_copies(ahead, lax.rem(ahead, RING_SLOTS)):
            c.start()

    slot = lax.rem(lin, RING_SLOTS)
    for c in page_copies(lin, slot):
        c.wait()
    k_pages = kbuf.at[slot]
    v_pages = vbuf.at[slot]

    @pl.when(step == 0)
    def _():
        m_scr[...] = jnp.full_like(m_scr, NEG)
        l_scr[...] = jnp.zeros_like(l_scr)
        acc_scr[...] = jnp.zeros_like(acc_scr)

    qs = [_split_halves(q_ref[:, h * E:(h + 1) * E], DH) for h in range(H)]
    head = lax.broadcasted_iota(jnp.int32, (H * NR, 1), 0) // NR
    slope = jnp.zeros((H * NR, 1), F32)
    for h in range(H):
        slope = jnp.where(head == h, slopes[h], slope)

    def update(k_tile, v_tile, n_tiles, kpos, ok):
        s = jnp.concatenate(
            [jnp.concatenate([_dot_nt(qs[h], k_tile(h, j)) for j in range(n_tiles)], axis=1)
             for h in range(H)], axis=0)
        s = s + slope * kpos
        if ok is not None:
            s = jnp.where(ok, s, NEG)
        m_prev = m_scr[...]
        m_new = jnp.maximum(m_prev, jnp.max(s, axis=-1, keepdims=True))
        alpha = jnp.exp2(m_prev - m_new)
        p = jnp.exp2(s - m_new)
        l_scr[...] = alpha * l_scr[...] + jnp.sum(p, axis=-1, keepdims=True)
        pb = p.astype(BF16)
        pv = []
        for h in range(H):
            out = _dot(pb[h * NR:(h + 1) * NR, 0:PAGE], v_tile(h, 0))
            for j in range(1, n_tiles):
                out = out + _dot(pb[h * NR:(h + 1) * NR, j * PAGE:(j + 1) * PAGE], v_tile(h, j))
            pv.append(out)
        acc_scr[...] = alpha * acc_scr[...] + jnp.concatenate(pv, axis=0)
        m_scr[...] = m_new

    col = lax.broadcasted_iota(jnp.int32, (1, PP * PAGE), 1)
    kpos = (step * (PP * PAGE) + col - past).astype(F32)
    update(lambda h, j: k_pages[j, pl.ds(h, PAGE, stride=H), :].astype(BF16),
           lambda h, j: v_pages[j, pl.ds(h, PAGE, stride=H), :].astype(BF16), PP, kpos, None)

    @pl.when(step == pl.num_programs(1) - 1)
    def _():
        tq = lax.broadcasted_iota(jnp.int32, (H * NR, PAGE), 0) & (R - 1)
        coln = lax.broadcasted_iota(jnp.int32, (H * NR, PAGE), 1)
        pad = jnp.zeros((PAGE - R, E), F32)
        update(lambda h, j: jnp.concatenate([kn_ref[pl.ds(h, R, stride=H), :], pad], axis=0).astype(BF16),
               lambda h, j: jnp.concatenate([vn_ref[pl.ds(h, R, stride=H), :], pad], axis=0).astype(BF16),
               1, coln[0:1].astype(F32), (coln <= tq) & (coln < T))
        o2 = acc_scr[...] / l_scr[...]
        for h in range(H):
            o = o2[h * NR:h * NR + R] - lam_ref[0] * o2[h * NR + R:(h + 1) * NR]
            o_ref[:, h * E:(h + 1) * E] = _rms(o, ng_ref[...]) * out_scale


def _diff_sample(q, kn, vn, cache_k, cache_v, page_ids, lam, ng, T, H, DH, out_scale, PP=16):
    B = q.shape[0]
    n_pages = page_ids.shape[1]
    PAGE = cache_k.shape[1] // H
    E = 2 * DH
    R = SAMPLE_ROWS
    slopes = tuple(LOG2E * 2.0 ** (-8.0 * (i + 1) / H) for i in range(H))
    assert n_pages % PP == 0 and q.shape[1] == R
    new_rows = pl.BlockSpec((None, R * H, E), lambda b, s, pt: (b, 0, 0))
    n_steps = n_pages // PP
    page_buf = pltpu.VMEM((RING_SLOTS, PP, PAGE * H, E), F32)
    grid_spec = pltpu.PrefetchScalarGridSpec(
        num_scalar_prefetch=1,
        grid=(B, n_steps),
        in_specs=[pl.BlockSpec(memory_space=pltpu.SMEM),
                  pl.BlockSpec((None, R, H * E), lambda b, s, pt: (b, 0, 0)),
                  new_rows, new_rows,
                  pl.BlockSpec((1, E), lambda b, s, pt: (0, 0)),
                  pl.BlockSpec(memory_space=pl.ANY),
                  pl.BlockSpec(memory_space=pl.ANY)],
        out_specs=pl.BlockSpec((None, R, H * E), lambda b, s, pt: (b, 0, 0)),
        scratch_shapes=[page_buf, page_buf, pltpu.SemaphoreType.DMA((RING_SLOTS,)),
                        pltpu.VMEM((H * 2 * R, 1), F32), pltpu.VMEM((H * 2 * R, 1), F32),
                        pltpu.VMEM((H * 2 * R, E), F32)])
    return pl.pallas_call(
        functools.partial(_diff_sample_kernel, PP=PP, PAGE=PAGE, T=T, H=H, DH=DH,
                          past=n_pages * PAGE, slopes=slopes, out_scale=out_scale,
                          total_steps=B * n_steps),
        grid_spec=grid_spec,
        out_shape=jax.ShapeDtypeStruct((B, R, H * E), F32),
        compiler_params=_params("arbitrary", "arbitrary"),
        name="diff_sample",
    )(page_ids, lam, q, kn, vn, ng.reshape(1, E), cache_k, cache_v)


def _mem_kv_kernel(x_ref, g_ref, wt_ref, kt_ref, vt_ref):
    h = _rms(x_ref[...], g_ref[...]).astype(BF16)
    kv = _dot_nt(wt_ref[...], h)
    w = kt_ref.shape[0]
    kt_ref[...] = kv[:w]
    vt_ref[...] = kv[w:]


def _mem_kv(mem, g, wt_bf16):
    B, N, D = mem.shape
    w = wt_bf16.shape[0] // 2
    out = jax.ShapeDtypeStruct((B, w, N), F32)
    blk = pl.BlockSpec((None, w, N), lambda b: (b, 0, 0))
    return pl.pallas_call(
        _mem_kv_kernel,
        grid=(B,),
        in_specs=[pl.BlockSpec((None, N, D), lambda b: (b, 0, 0)),
                  pl.BlockSpec((1, D), lambda b: (0, 0)),
                  pl.BlockSpec((2 * w, D), lambda b: (0, 0))],
        out_specs=[blk, blk],
        out_shape=[out, out],
        compiler_params=_params("parallel"),
        name="mem_kv",
    )(mem, g.reshape(1, D), wt_bf16)


def _mem_attend(q, kt, vt, H, DH):
    T = q.shape[0]
    q = q * (LOG2E * DH ** -0.5)
    lane_h = lax.broadcasted_iota(jnp.int32, q.shape, 1) // DH
    qs = jnp.concatenate([jnp.where(lane_h == h, q, 0.0) for h in range(H)], axis=0).astype(BF16)
    s = _dot(qs, kt.astype(BF16))
    p = jnp.exp2(s - jnp.max(s, axis=-1, keepdims=True))
    o4 = _dot_nt(p.astype(BF16), vt.astype(BF16)) / jnp.sum(p, axis=-1, keepdims=True)
    o = jnp.zeros_like(q)
    for h in range(H):
        o = jnp.where(lane_h == h, o4[h * T:(h + 1) * T], o)
    return o


def _mem_attn_kernel(q_ref, kt_ref, vt_ref, o_ref, *, NB, H, DH):
    for n in range(NB):
        o_ref[n] = _mem_attend(q_ref[n], kt_ref[n], vt_ref[n], H, DH)


def _mem_attn(q, kt, vt, H, DH, NB=8):
    B, T, W = q.shape
    N = kt.shape[2]
    assert B % NB == 0
    kv = pl.BlockSpec((NB, W, N), lambda b: (b, 0, 0))
    return pl.pallas_call(
        functools.partial(_mem_attn_kernel, NB=NB, H=H, DH=DH),
        grid=(B // NB,),
        in_specs=[pl.BlockSpec((NB, T, W), lambda b: (b, 0, 0)), kv, kv],
        out_specs=pl.BlockSpec((NB, T, W), lambda b: (b, 0, 0)),
        out_shape=jax.ShapeDtypeStruct((B, T, W), F32),
        compiler_params=_params("parallel"),
        name="mem_attn",
    )(q, kt, vt)


def _out_kernel(x_ref, oa_ref, ob_ref, m_ref, z_ref, w_ref, g_ref, *rest, final, mem_heads):
    if mem_heads is None:
        (y_ref,) = rest
        om = m_ref[...]
    else:
        kt_ref, vt_ref, y_ref = rest
        om = _mem_attend(m_ref[...], kt_ref[...], vt_ref[...], *mem_heads)
    o = jnp.concatenate([oa_ref[...].astype(F32), ob_ref[...].astype(F32), om], axis=-1)
    z = z_ref[...].astype(F32)
    o = (o * (z * jax.nn.sigmoid(z))).astype(BF16)
    y = x_ref[...] + _dot(o, w_ref[...])
    y_ref[...] = _rms(y, g_ref[...]) if final else y


def _out_proj(x2d, oa, ob, m, z, w_bf16, g, tm, final, mem=None):
    rows_total, d = x2d.shape
    mix = w_bf16.shape[0]

    def rows(a):
        return pl.BlockSpec((tm, a.shape[1]), lambda i: (i, 0))

    in_specs = [rows(x2d), rows(oa), rows(ob), rows(m), rows(z),
                pl.BlockSpec((mix, d), lambda i: (0, 0)),
                pl.BlockSpec((1, d), lambda i: (0, 0))]
    args = [x2d, oa, ob, m, z, w_bf16, g.reshape(1, d)]
    mem_heads = None
    if mem is not None:
        kt, vt, H, DH, rows_per_batch = mem
        assert rows_per_batch % tm == 0
        tiles = rows_per_batch // tm
        kv = pl.BlockSpec((None,) + kt.shape[1:], lambda i: (i // tiles, 0, 0))
        in_specs += [kv, kv]
        args += [kt, vt]
        mem_heads = (H, DH)
    return pl.pallas_call(
        functools.partial(_out_kernel, final=final, mem_heads=mem_heads),
        grid=(rows_total // tm,),
        in_specs=in_specs,
        out_specs=rows(x2d),
        out_shape=jax.ShapeDtypeStruct((rows_total, d), F32),
        compiler_params=_params("parallel"),
        name="out_proj",
    )(*args)


def kernel(x_prompt, x_sample, mem_prompt, cache_k, cache_v, state_hgrn, cache_mem_k, cache_mem_v, page_table, norm_g, w_in, hgrn_lb_logits, hgrn_norm_g, diff_norm_g, lambda_q1, lambda_k1, lambda_q2, lambda_k2, mem_norm_g, w_mem_kv, w_out, final_g):
    B, L, D = x_prompt.shape
    SB, T, _ = x_sample.shape
    depth, _, H_A, DK_A, DV_A = state_hgrn.shape
    _, n_pool, PAGE, H_B, E_B = cache_k.shape
    DH_B = E_B // 2
    _, _, N_MEM, H_M, DH_M = cache_mem_k.shape
    W_A, W_K, W_B, W_M = H_A * DV_A, H_A * DK_A, H_B * E_B, H_M * DH_M
    MIX = W_A + W_B + W_M
    R = SAMPLE_ROWS
    TM = ROW_TILE
    assert T <= R and L % TM == 0
    sizes = (2 * W_K + W_A, W_B, W_M, MIX)
    diff_scale = LOG2E * DH_B ** -0.5

    cache_k3 = cache_k.reshape(depth * n_pool, PAGE * H_B, E_B)
    cache_v3 = cache_v.reshape(depth * n_pool, PAGE * H_B, E_B)

    yp = x_prompt.reshape(B * L, D)
    ys = jnp.pad(x_sample, ((0, 0), (0, R - T), (0, 0))).reshape(SB * R, D)
    outs = {k: [] for k in ("kp", "vp", "sp", "mkp", "mvp", "ks", "vs", "ss")}
    for l in range(depth):
        lam_init = 0.8 - 0.6 * math.exp(-0.3 * l)
        lam = (jnp.exp(jnp.sum(lambda_q1[l] * lambda_k1[l])) - jnp.exp(jnp.sum(lambda_q2[l] * lambda_k2[l]))
               + lam_init).reshape(1).astype(F32)
        w_in_b = w_in[l].astype(BF16)
        w_out_b = w_out[l].astype(BF16)
        w_mem_t = w_mem_kv[l].T.astype(BF16)

        def channel_major(a):
            return jnp.transpose(a, (0, 2, 3, 1)).reshape(a.shape[0], W_M, N_MEM)

        def token_major(a):
            return jnp.transpose(a.reshape(a.shape[0], H_M, DH_M, N_MEM), (0, 3, 1, 2))

        mkt, mvt = _mem_kv(mem_prompt, mem_norm_g[l], w_mem_t)
        pa, dq, kb, vb, ko, vo, mq, z = _in_proj(yp, norm_g[l], w_in_b, sizes, H_B, TM, diff_scale, True,
                                                 "in_proj_prompt")
        oa, st = _hgrn_prompt(pa.reshape(B, L, -1), hgrn_lb_logits, hgrn_norm_g[l], l, H_A, DK_A, DV_A)
        ob = _diff_prompt(dq.reshape(B, L, W_B), kb.reshape(B, L, W_B), vb.reshape(B, L, W_B),
                          lam, diff_norm_g[l], H_B, DH_B, 1.0 - lam_init)
        yp = _out_proj(yp, oa.reshape(B * L, W_A), ob.reshape(B * L, W_B), mq, z,
                       w_out_b, final_g, 2 * TM, l == depth - 1, mem=(mkt, mvt, H_M, DH_M, L))
        outs["kp"].append(ko.reshape(B, L, H_B, E_B))
        outs["vp"].append(vo.reshape(B, L, H_B, E_B))
        outs["sp"].append(jnp.swapaxes(st, -1, -2))
        outs["mkp"].append(token_major(mkt))
        outs["mvp"].append(token_major(mvt))

        pa, dq, ko, vo, mq, z = _in_proj(ys, norm_g[l], w_in_b, sizes, H_B, SB * R, diff_scale, False,
                                         "in_proj_sample")
        oa, sst = _hgrn_sample(pa.reshape(SB, R, -1), hgrn_lb_logits, hgrn_norm_g[l],
                               jnp.swapaxes(state_hgrn[l], -1, -2), l, T, H_A, DK_A, DV_A)
        ob = _diff_sample(dq.reshape(SB, R, W_B), ko.reshape(SB, R * H_B, E_B), vo.reshape(SB, R * H_B, E_B),
                          cache_k3, cache_v3, page_table + l * n_pool,
                          lam, diff_norm_g[l], T, H_B, DH_B, 1.0 - lam_init)
        om = _mem_attn(mq.reshape(SB, R, W_M), channel_major(cache_mem_k[l]), channel_major(cache_mem_v[l]),
                       H_M, DH_M)
        ys = _out_proj(ys, oa.reshape(SB * R, W_A), ob.reshape(SB * R, W_B), om.reshape(SB * R, W_M), z,
                       w_out_b, final_g, SB * R, l == depth - 1)
        outs["ks"].append(ko.reshape(SB, R, H_B, E_B)[:, :T])
        outs["vs"].append(vo.reshape(SB, R, H_B, E_B)[:, :T])
        outs["ss"].append(jnp.swapaxes(sst, -1, -2))

    y_prompt = yp.reshape(B, L, D)
    y_sample = ys.reshape(SB, R, D)[:, :T]
    return (y_prompt, y_sample, jnp.stack(outs["kp"]), jnp.stack(outs["vp"]), jnp.stack(outs["sp"]),
            jnp.stack(outs["mkp"]), jnp.stack(outs["mvp"]), jnp.stack(outs["ks"]), jnp.stack(outs["vs"]),
            jnp.stack(outs["ss"]))
```

```python
import functools
import math

import jax
import jax.numpy as jnp
from jax import lax
from jax.experimental import pallas as pl
from jax.experimental.pallas import tpu as pltpu

F32 = jnp.float32
BF16 = jnp.bfloat16
EPS = 1e-6
NEG = -1e30
LOG2E = math.log2(math.e)
VMEM_LIMIT = 48 * 1024 * 1024
SUBLANES = 8
SAMPLE_ROWS = SUBLANES
ROW_TILE = 512
RING_SLOTS = 3

_NT = (((1,), (1,)), ((), ()))
_TN = (((0,), (0,)), ((), ()))


def _dot(a, b):
    return jnp.dot(a, b, preferred_element_type=F32)


def _dot_nt(a, b):
    return lax.dot_general(a, b, _NT, preferred_element_type=F32)


def _dot_tn(a, b):
    return lax.dot_general(a, b, _TN, preferred_element_type=F32)


def _rms(x, g):
    return x * lax.rsqrt(jnp.mean(x * x, axis=-1, keepdims=True) + EPS) * g


def _params(*sem):
    return pltpu.CompilerParams(dimension_semantics=sem, vmem_limit_bytes=VMEM_LIMIT)


def _split_halves(q, dh):
    lane = lax.broadcasted_iota(jnp.int32, q.shape, 1)
    zero = jnp.zeros_like(q)
    return jnp.concatenate([jnp.where(lane < dh, q, zero), jnp.where(lane < dh, zero, q)], axis=0)


def _proj_kernel(x_ref, g_ref, w_ref, *out_refs, col_sizes):
    h = _rms(x_ref[...], g_ref[...]).astype(BF16)
    start = 0
    for ref, size in zip(out_refs, col_sizes):
        ref[...] = _dot(h, w_ref[:, start:start + size]).astype(ref.dtype)
        start += size


def _norm_proj(x2d, g, w_bf16, col_sizes, tm, name):
    m, d = x2d.shape
    n = w_bf16.shape[1]
    assert sum(col_sizes) == n and m % tm == 0
    return pl.pallas_call(
        functools.partial(_proj_kernel, col_sizes=col_sizes),
        grid=(m // tm,),
        in_specs=[pl.BlockSpec((tm, d), lambda i: (i, 0)),
                  pl.BlockSpec((1, d), lambda i: (0, 0)),
                  pl.BlockSpec((d, n), lambda i: (0, 0))],
        out_specs=[pl.BlockSpec((tm, s), lambda i: (i, 0)) for s in col_sizes],
        out_shape=[jax.ShapeDtypeStruct((m, s), F32) for s in col_sizes],
        compiler_params=_params("parallel"),
        name=name,
    )(x2d, g.reshape(1, d), w_bf16)


def _in_proj_kernel(x_ref, g_ref, w_ref, *rest, sizes, H, prompt, scale):
    w_pa, w_b, w_m, mix = sizes
    if prompt:
        pa_ref, dq_ref, kb_ref, vb_ref, ko_ref, vo_ref, mq_ref, z_ref = rest
    else:
        pa_ref, dq_ref, ko_ref, vo_ref, mq_ref, z_ref = rest
    e = w_b // H
    h = _rms(x_ref[...], g_ref[...]).astype(BF16)
    c = 0
    pa_ref[...] = _dot(h, w_ref[:, c:c + w_pa])
    c += w_pa
    dq_ref[...] = (_dot(h, w_ref[:, c:c + w_b]) * scale).astype(BF16)
    c += w_b
    k = _dot(h, w_ref[:, c:c + w_b])
    c += w_b
    v = _dot(h, w_ref[:, c:c + w_b])
    c += w_b
    tm = k.shape[0]
    for hh in range(H):
        ko_ref[pl.ds(hh, tm, stride=H), :] = k[:, hh * e:(hh + 1) * e]
        vo_ref[pl.ds(hh, tm, stride=H), :] = v[:, hh * e:(hh + 1) * e]
    if prompt:
        kb_ref[...] = k.astype(BF16)
        vb_ref[...] = v.astype(BF16)
    mq_ref[...] = _dot(h, w_ref[:, c:c + w_m])
    c += w_m
    z_ref[...] = _dot(h, w_ref[:, c:c + mix]).astype(z_ref.dtype)


def _in_proj(x2d, g, w_bf16, sizes, H, tm, scale, prompt, name):
    m, d = x2d.shape
    n = w_bf16.shape[1]
    w_pa, w_b, w_m, mix = sizes
    e = w_b // H
    assert w_pa + 3 * w_b + w_m + mix == n and m % tm == 0

    def rows(width):
        return pl.BlockSpec((tm, width), lambda i: (i, 0))

    const = lambda i: (0, 0)
    in_specs = [rows(d), pl.BlockSpec((1, d), const), pl.BlockSpec((d, n), const)]
    args = [x2d, g.reshape(1, d), w_bf16]
    out_specs = [rows(w_pa), rows(w_b)]
    out_shape = [jax.ShapeDtypeStruct((m, w_pa), F32), jax.ShapeDtypeStruct((m, w_b), BF16)]
    if prompt:
        out_specs += [rows(w_b), rows(w_b)]
        out_shape += [jax.ShapeDtypeStruct((m, w_b), BF16), jax.ShapeDtypeStruct((m, w_b), BF16)]
    head_rows = pl.BlockSpec((tm * H, e), lambda i: (i, 0))
    out_specs += [head_rows, head_rows, rows(w_m), rows(mix)]
    out_shape += [jax.ShapeDtypeStruct((m * H, e), F32), jax.ShapeDtypeStruct((m * H, e), F32),
                  jax.ShapeDtypeStruct((m, w_m), F32), jax.ShapeDtypeStruct((m, mix), BF16)]
    return pl.pallas_call(
        functools.partial(_in_proj_kernel, sizes=sizes, H=H, prompt=prompt, scale=scale),
        grid=(m // tm,),
        in_specs=in_specs,
        out_specs=out_specs,
        out_shape=out_shape,
        compiler_params=_params("parallel"),
        name=name,
    )(*args)


def _lower_bound(lg, layer, axis=0):
    e = jnp.exp(lg - jnp.max(lg, axis=axis, keepdims=True))
    p = e / jnp.sum(e, axis=axis, keepdims=True)
    head = p[:layer + 1] if axis == 0 else p[:, :layer + 1]
    return jnp.sum(head, axis=axis, keepdims=True)


def _hgrn_chunk_kernel(pa_ref, lg_ref, ng_ref, o_ref, sout_ref, s_scr, b_scr,
                       *, C, NC, layer, H, DK, DV):
    step = pl.program_id(1)

    @pl.when(step == 0)
    def _():
        s_scr[...] = jnp.zeros_like(s_scr)

    W = H * DK
    lb = _lower_bound(lg_ref[...], layer)
    ti = lax.broadcasted_iota(jnp.int32, (C, C), 0)
    si = lax.broadcasted_iota(jnp.int32, (C, C), 1)
    lmat = (si <= ti).astype(BF16)
    xor_lower = jnp.where(ti > si, ti ^ si, 0)
    row = lax.broadcasted_iota(jnp.int32, (C, 1), 0)

    def prepare(n):
        rows = slice(n * C, (n + 1) * C)
        f = lb + (1.0 - lb) * jax.nn.sigmoid(pa_ref[rows, W:2 * W])
        g = jnp.log2(f)
        kk = 1.0 - f
        q = pa_ref[rows, 0:W]
        g_hi = g.astype(BF16)
        r1 = g - g_hi.astype(F32)
        g_mid = r1.astype(BF16)
        g_lo = (r1 - g_mid.astype(F32)).astype(BF16)
        b = _dot(lmat, g_hi) + _dot(lmat, g_mid) + _dot(lmat, g_lo)
        b_scr[rows, :] = b
        bl = b_scr[(n + 1) * C - 1:(n + 1) * C, :]

        levels = []
        m = C // 2
        while m >= 1:
            upper = (row & (2 * m - 1)) >= m
            if m >= SUBLANES:
                side, d = [], []
                for blk in range(C // (2 * m)):
                    lo = slice(blk * 2 * m, blk * 2 * m + m)
                    hi = slice(blk * 2 * m + m, (blk + 1) * 2 * m)
                    bref = b_scr[n * C + blk * 2 * m + m - 1:n * C + blk * 2 * m + m, :]
                    side += [kk[lo], q[hi]]
                    d += [bref - b[lo], b[hi] - bref]
                side = jnp.concatenate(side, axis=0)
                d = jnp.concatenate(d, axis=0)
            else:
                side = jnp.where(upper, q, kk)
                if m == 4:
                    bref = jnp.concatenate(
                        [jnp.broadcast_to(b_scr[n * C + blk * 8 + 3:n * C + blk * 8 + 4, :], (8, W))
                         for blk in range(C // 8)], axis=0)
                    d = -jnp.abs(b - bref)
                elif m == 2:
                    d = jnp.where(upper, g + jnp.where((row & 3) == 3, pltpu.roll(g, 1, 0), 0.0),
                                  jnp.where((row & 3) == 0, pltpu.roll(g, C - 1, 0), 0.0))
                else:
                    d = jnp.where(upper, g, 0.0)
            levels.append((side * jnp.exp2(d)).astype(BF16))
            m //= 2
        q_bf = q.astype(BF16)
        k_bf = kk.astype(BF16)
        intra = []
        for h in range(H):
            ks = slice(h * DK, (h + 1) * DK)
            a = jnp.where(ti == si, _dot_nt(q_bf[:, ks], k_bf[:, ks]), 0.0)
            m = 1
            for x in reversed(levels):
                a = jnp.where(xor_lower >= m, _dot_nt(x[:, ks], x[:, ks]), a)
                m *= 2
            v = pa_ref[rows, 2 * W + h * DV:2 * W + (h + 1) * DV].astype(BF16)
            intra.append((_dot(a.astype(BF16), v), v))
        q_in = (q * jnp.exp2(b)).astype(BF16)
        k_out = (kk * jnp.exp2(bl - b)).astype(BF16)
        return intra, q_in, k_out, jnp.exp2(bl)

    prepared = [prepare(n) for n in range(NC)]
    for n, (intra, q_in, k_out, ebl) in enumerate(prepared):
        rows = slice(n * C, (n + 1) * C)
        for h in range(H):
            ks = slice(h * DK, (h + 1) * DK)
            o_intra, v = intra[h]
            st = s_scr[h]
            o = _dot_nt(q_in[:, ks], st.astype(BF16)) + o_intra
            o_ref[rows, h * DV:(h + 1) * DV] = _rms(o, ng_ref[...]).astype(o_ref.dtype)
            s_scr[h] = ebl[:, ks] * st + _dot_tn(v, k_out[:, ks])

    @pl.when(step == pl.num_programs(1) - 1)
    def _():
        sout_ref[...] = s_scr[...]


def _hgrn_prompt(pa, lg, ng, layer, H, DK, DV, C=128, NC=4):
    B, L, wp = pa.shape
    W = H * DK
    R = C * NC
    assert L % R == 0 and wp == 2 * W + H * DV
    return pl.pallas_call(
        functools.partial(_hgrn_chunk_kernel, C=C, NC=NC, layer=layer, H=H, DK=DK, DV=DV),
        grid=(B, L // R),
        in_specs=[pl.BlockSpec((None, R, wp), lambda b, c: (b, c, 0)),
                  pl.BlockSpec(lg.shape, lambda b, c: (0, 0)),
                  pl.BlockSpec((1, DV), lambda b, c: (0, 0))],
        out_specs=[pl.BlockSpec((None, R, H * DV), lambda b, c: (b, c, 0)),
                   pl.BlockSpec((None, H, DV, DK), lambda b, c: (b, 0, 0, 0))],
        out_shape=[jax.ShapeDtypeStruct((B, L, H * DV), BF16),
                   jax.ShapeDtypeStruct((B, H, DV, DK), F32)],
        scratch_shapes=[pltpu.VMEM((H, DV, DK), F32), pltpu.VMEM((R, W), F32)],
        compiler_params=_params("parallel", "arbitrary"),
        name="hgrn_prompt",
    )(pa, lg, ng.reshape(1, DV))


def _hgrn_sample_kernel(pa_ref, lg_ref, ng_ref, s0_ref, o_ref, sout_ref, *, NB, T, layer, H, DK, DV):
    W = H * DK
    R = pa_ref.shape[1]
    lb = _lower_bound(lg_ref[...], layer)
    row = lax.broadcasted_iota(jnp.int32, (R, 1), 0)
    valid = row < T
    for n in range(NB):
        f = lb + (1.0 - lb) * jax.nn.sigmoid(pa_ref[n, :, W:2 * W])
        g = jnp.where(valid, jnp.log2(f), 0.0)
        kk = jnp.where(valid, 1.0 - f, 0.0)
        q = pa_ref[n, :, 0:W]
        b = g
        sh = 1
        while sh < R:
            b = b + jnp.where(row >= sh, pltpu.roll(b, sh, 0), 0.0)
            sh *= 2
        bl = b[R - 1:R]
        q_in = (q * jnp.exp2(b)).astype(BF16)
        k_out = (kk * jnp.exp2(bl - b)).astype(BF16)
        ebl = jnp.exp2(bl)
        for h in range(H):
            ks = slice(h * DK, (h + 1) * DK)
            v = pa_ref[n, :, 2 * W + h * DV:2 * W + (h + 1) * DV]
            st = s0_ref[n, h]
            o = _dot_nt(q_in[:, ks], st.astype(BF16))
            for s in range(T):
                later = row >= s
                decay = jnp.exp2(jnp.where(later, b[:, ks] - b[s:s + 1, ks], 0.0))
                a_s = jnp.sum(jnp.where(later, q[:, ks] * kk[s:s + 1, ks] * decay, 0.0), axis=1, keepdims=True)
                o = o + a_s * v[s:s + 1, :]
            o_ref[n, :, h * DV:(h + 1) * DV] = _rms(o, ng_ref[...])
            sout_ref[n, h] = ebl[:, ks] * st + _dot_tn(v.astype(BF16), k_out[:, ks])


def _hgrn_sample(pa, lg, ng, s0t, layer, T, H, DK, DV, NB=4):
    B, R, wp = pa.shape
    assert B % NB == 0
    return pl.pallas_call(
        functools.partial(_hgrn_sample_kernel, NB=NB, T=T, layer=layer, H=H, DK=DK, DV=DV),
        grid=(B // NB,),
        in_specs=[pl.BlockSpec((NB, R, wp), lambda b: (b, 0, 0)),
                  pl.BlockSpec(lg.shape, lambda b: (0, 0)),
                  pl.BlockSpec((1, DV), lambda b: (0, 0)),
                  pl.BlockSpec((NB, H, DV, DK), lambda b: (b, 0, 0, 0))],
        out_specs=[pl.BlockSpec((NB, R, H * DV), lambda b: (b, 0, 0)),
                   pl.BlockSpec((NB, H, DV, DK), lambda b: (b, 0, 0, 0))],
        out_shape=[jax.ShapeDtypeStruct((B, R, H * DV), F32),
                   jax.ShapeDtypeStruct((B, H, DV, DK), F32)],
        compiler_params=_params("parallel"),
        name="hgrn_sample",
    )(pa, lg, ng.reshape(1, DV), s0t)


def _diff_prompt_kernel(slope_ref, lam_ref, q_ref, k_ref, v_ref, g_ref, o_ref, m_scr, l_scr, acc_scr,
                        *, TQ, DH, HP, out_scale):
    hg = pl.program_id(1)
    qi = pl.program_id(2)
    E = 2 * DH
    LANES = m_scr.shape[2]
    trel = lax.broadcasted_iota(jnp.int32, (2 * TQ, TQ), 0) & (TQ - 1)
    srel = lax.broadcasted_iota(jnp.int32, (2 * TQ, TQ), 1)
    key_pos = lax.broadcasted_iota(jnp.int32, (1, TQ), 1).astype(F32)
    slopes = [slope_ref[hg * HP + j] for j in range(HP)]
    qs = [_split_halves(q_ref[:, j * E:(j + 1) * E], DH) for j in range(HP)]

    m_scr[...] = jnp.full_like(m_scr, NEG)
    l_scr[...] = jnp.zeros_like(l_scr)
    acc_scr[...] = jnp.zeros_like(acc_scr)

    def block(kj, masked):
        rows = pl.ds(pl.multiple_of(kj * TQ, TQ), TQ)
        for j in range(HP):
            cols = slice(j * E, (j + 1) * E)
            s = _dot_nt(qs[j], k_ref[rows, cols])
            s = s + slopes[j] * (key_pos + ((kj - qi) * TQ).astype(F32))
            if masked:
                s = jnp.where(srel <= trel, s, NEG)
            m_prev = m_scr[j]
            m_new = jnp.maximum(m_prev, jnp.max(s, axis=1, keepdims=True))
            alpha = jnp.exp2(m_prev - m_new)
            p = jnp.exp2(s - jnp.tile(m_new, (1, TQ // LANES)))
            l_scr[j] = alpha * l_scr[j] + jnp.sum(p, axis=1, keepdims=True)
            acc_scr[j] = alpha * acc_scr[j] + _dot(p.astype(BF16), v_ref[rows, cols])
            m_scr[j] = m_new

    def body(kj, carry):
        block(kj, False)
        return carry

    lax.fori_loop(0, qi, body, 0)
    block(qi, True)

    for j in range(HP):
        o2 = acc_scr[j] / l_scr[j]
        o = o2[:TQ] - lam_ref[0] * o2[TQ:]
        o_ref[:, j * E:(j + 1) * E] = (_rms(o, g_ref[...]) * out_scale).astype(o_ref.dtype)


def _diff_prompt(q, kb, vb, lam, ng, H, DH, out_scale, TQ=512, HP=4):
    B, L, W = q.shape
    E = 2 * DH
    slopes = jnp.asarray([LOG2E * 2.0 ** (-8.0 * (i + 1) / H) for i in range(H)], F32)
    assert L % TQ == 0 and TQ & (TQ - 1) == 0 and H % HP == 0
    return pl.pallas_call(
        functools.partial(_diff_prompt_kernel, TQ=TQ, DH=DH, HP=HP, out_scale=out_scale),
        grid=(B, H // HP, L // TQ),
        in_specs=[pl.BlockSpec(memory_space=pltpu.SMEM),
                  pl.BlockSpec(memory_space=pltpu.SMEM),
                  pl.BlockSpec((None, TQ, HP * E), lambda b, h, i: (b, i, h)),
                  pl.BlockSpec((None, L, HP * E), lambda b, h, i: (b, 0, h)),
                  pl.BlockSpec((None, L, HP * E), lambda b, h, i: (b, 0, h)),
                  pl.BlockSpec((1, E), lambda b, h, i: (0, 0))],
        out_specs=pl.BlockSpec((None, TQ, HP * E), lambda b, h, i: (b, i, h)),
        out_shape=jax.ShapeDtypeStruct((B, L, W), BF16),
        scratch_shapes=[pltpu.VMEM((HP, 2 * TQ, E), F32), pltpu.VMEM((HP, 2 * TQ, E), F32),
                        pltpu.VMEM((HP, 2 * TQ, E), F32)],
        compiler_params=_params("parallel", "parallel", "arbitrary"),
        name="diff_prompt",
    )(slopes, lam, q, kb, vb, ng.reshape(1, E))


def _diff_sample_kernel(pt_ref, lam_ref, q_ref, kn_ref, vn_ref, ng_ref, ck_ref, cv_ref, o_ref,
                        kbuf, vbuf, sem, m_scr, l_scr, acc_scr,
                        *, PP, PAGE, T, H, DH, past, slopes, out_scale, total_steps):
    step = pl.program_id(1)
    n_steps = pl.num_programs(1)
    lin = pl.program_id(0) * n_steps + step
    E = 2 * DH
    R = SAMPLE_ROWS
    NR = 2 * R

    def page_copies(g, slot):
        seq = g // n_steps
        first = (g % n_steps) * PP
        copies = []
        for j in range(PP):
            page = pt_ref[seq, first + j]
            copies.append(pltpu.make_async_copy(ck_ref.at[page], kbuf.at[slot, j], sem.at[slot]))
            copies.append(pltpu.make_async_copy(cv_ref.at[page], vbuf.at[slot, j], sem.at[slot]))
        return copies

    @pl.when(lin == 0)
    def _():
        for g in range(min(RING_SLOTS - 1, total_steps)):
            for c in page_copies(g, g):
                c.start()

    @pl.when(lin + (RING_SLOTS - 1) < total_steps)
    def _():
        ahead = lin + (RING_SLOTS - 1)
        for c in page_copies(ahead, lax.rem(ahead, RING_SLOTS)):
            c.start()

    slot = lax.rem(lin, RING_SLOTS)
    for c in page_copies(lin, slot):
        c.wait()
    k_pages = kbuf.at[slot]
    v_pages = vbuf.at[slot]

    @pl.when(step == 0)
    def _():
        m_scr[...] = jnp.full_like(m_scr, NEG)
        l_scr[...] = jnp.zeros_like(l_scr)
        acc_scr[...] = jnp.zeros_like(acc_scr)

    qs = [_split_halves(q_ref[:, h * E:(h + 1) * E], DH) for h in range(H)]
    head = lax.broadcasted_iota(jnp.int32, (H * NR, 1), 0) // NR
    slope = jnp.zeros((H * NR, 1), F32)
    for h in range(H):
        slope = jnp.where(head == h, slopes[h], slope)

    def update(k_tile, v_tile, n_tiles, kpos, ok):
        s = jnp.concatenate(
            [jnp.concatenate([_dot_nt(qs[h], k_tile(h, j)) for j in range(n_tiles)], axis=1)
             for h in range(H)], axis=0)
        s = s + slope * kpos
        if ok is not None:
            s = jnp.where(ok, s, NEG)
        m_prev = m_scr[...]
        m_new = jnp.maximum(m_prev, jnp.max(s, axis=-1, keepdims=True))
        alpha = jnp.exp2(m_prev - m_new)
        p = jnp.exp2(s - m_new)
        l_scr[...] = alpha * l_scr[...] + jnp.sum(p, axis=-1, keepdims=True)
        pb = p.astype(BF16)
        pv = []
        for h in range(H):
            out = _dot(pb[h * NR:(h + 1) * NR, 0:PAGE], v_tile(h, 0))
            for j in range(1, n_tiles):
                out = out + _dot(pb[h * NR:(h + 1) * NR, j * PAGE:(j + 1) * PAGE], v_tile(h, j))
            pv.append(out)
        acc_scr[...] = alpha * acc_scr[...] + jnp.concatenate(pv, axis=0)
        m_scr[...] = m_new

    col = lax.broadcasted_iota(jnp.int32, (1, PP * PAGE), 1)
    kpos = (step * (PP * PAGE) + col - past).astype(F32)
    update(lambda h, j: k_pages[j, pl.ds(h, PAGE, stride=H), :].astype(BF16),
           lambda h, j: v_pages[j, pl.ds(h, PAGE, stride=H), :].astype(BF16), PP, kpos, None)

    @pl.when(step == pl.num_programs(1) - 1)
    def _():
        tq = lax.broadcasted_iota(jnp.int32, (H * NR, PAGE), 0) & (R - 1)
        coln = lax.broadcasted_iota(jnp.int32, (H * NR, PAGE), 1)
        pad = jnp.zeros((PAGE - R, E), F32)
        update(lambda h, j: jnp.concatenate([kn_ref[pl.ds(h, R, stride=H), :], pad], axis=0).astype(BF16),
               lambda h, j: jnp.concatenate([vn_ref[pl.ds(h, R, stride=H), :], pad], axis=0).astype(BF16),
               1, coln[0:1].astype(F32), (coln <= tq) & (coln < T))
        o2 = acc_scr[...] / l_scr[...]
        for h in range(H):
            o = o2[h * NR:h * NR + R] - lam_ref[0] * o2[h * NR + R:(h + 1) * NR]
            o_ref[:, h * E:(h + 1) * E] = _rms(o, ng_ref[...]) * out_scale


def _diff_sample(q, kn, vn, cache_k, cache_v, page_ids, lam, ng, T, H, DH, out_scale, PP=16):
    B = q.shape[0]
    n_pages = page_ids.shape[1]
    PAGE = cache_k.shape[1] // H
    E = 2 * DH
    R = SAMPLE_ROWS
    slopes = tuple(LOG2E * 2.0 ** (-8.0 * (i + 1) / H) for i in range(H))
    assert n_pages % PP == 0 and q.shape[1] == R
    new_rows = pl.BlockSpec((None, R * H, E), lambda b, s, pt: (b, 0, 0))
    n_steps = n_pages // PP
    page_buf = pltpu.VMEM((RING_SLOTS, PP, PAGE * H, E), F32)
    grid_spec = pltpu.PrefetchScalarGridSpec(
        num_scalar_prefetch=1,
        grid=(B, n_steps),
        in_specs=[pl.BlockSpec(memory_space=pltpu.SMEM),
                  pl.BlockSpec((None, R, H * E), lambda b, s, pt: (b, 0, 0)),
                  new_rows, new_rows,
                  pl.BlockSpec((1, E), lambda b, s, pt: (0, 0)),
                  pl.BlockSpec(memory_space=pl.ANY),
                  pl.BlockSpec(memory_space=pl.ANY)],
        out_specs=pl.BlockSpec((None, R, H * E), lambda b, s, pt: (b, 0, 0)),
        scratch_shapes=[page_buf, page_buf, pltpu.SemaphoreType.DMA((RING_SLOTS,)),
                        pltpu.VMEM((H * 2 * R, 1), F32), pltpu.VMEM((H * 2 * R, 1), F32),
                        pltpu.VMEM((H * 2 * R, E), F32)])
    return pl.pallas_call(
        functools.partial(_diff_sample_kernel, PP=PP, PAGE=PAGE, T=T, H=H, DH=DH,
                          past=n_pages * PAGE, slopes=slopes, out_scale=out_scale,
                          total_steps=B * n_steps),
        grid_spec=grid_spec,
        out_shape=jax.ShapeDtypeStruct((B, R, H * E), F32),
        compiler_params=_params("arbitrary", "arbitrary"),
        name="diff_sample",
    )(page_ids, lam, q, kn, vn, ng.reshape(1, E), cache_k, cache_v)


def _hgrn_sample_attn_kernel(pt_ref, lam_ref, pa_ref, lg_ref, hng_ref, q_ref, kn_ref, vn_ref, dng_ref,
                             ck_ref, cv_ref, oa_ref, sout_ref, ob_ref,
                             s_scr, b_scr, kbuf, vbuf, sem, m_scr, l_scr, acc_scr,
                             *, C, NC, layer, H, DK, DV, SUB, sub_per_seq, total_sub, hsteps,
                             PP, PAGE, T, HB, DH, past, slopes, out_scale):
    fs = pl.program_id(0)
    n_fused = pl.num_programs(0)
    R = SAMPLE_ROWS
    NR = 2 * R
    E = 2 * DH
    W = H * DK
    fused_per_seq = sub_per_seq // SUB
    part = fs % fused_per_seq

    def page_copies(g, slot):
        seq = g // sub_per_seq
        first = (g % sub_per_seq) * PP
        copies = []
        for j in range(PP):
            page = pt_ref[seq, first + j]
            copies.append(pltpu.make_async_copy(ck_ref.at[page], kbuf.at[slot, j], sem.at[slot]))
            copies.append(pltpu.make_async_copy(cv_ref.at[page], vbuf.at[slot, j], sem.at[slot]))
        return copies

    @pl.when(fs == 0)
    def _():
        for g in range(min(RING_SLOTS - 1, total_sub)):
            for c in page_copies(g, g):
                c.start()
        m_scr[...] = jnp.full_like(m_scr, NEG)
        l_scr[...] = jnp.zeros_like(l_scr)
        acc_scr[...] = jnp.zeros_like(acc_scr)

    @pl.when(fs % hsteps == 0)
    def _():
        s_scr[...] = jnp.zeros_like(s_scr)

    lb = _lower_bound(lg_ref[...], layer)
    ti = lax.broadcasted_iota(jnp.int32, (C, C), 0)
    si = lax.broadcasted_iota(jnp.int32, (C, C), 1)
    lmat = (si <= ti).astype(BF16)
    xor_lower = jnp.where(ti > si, ti ^ si, 0)
    row = lax.broadcasted_iota(jnp.int32, (C, 1), 0)

    qs = [_split_halves(q_ref[:, h * E:(h + 1) * E], DH) for h in range(HB)]
    head = lax.broadcasted_iota(jnp.int32, (HB * NR, 1), 0) // NR
    slope = jnp.zeros((HB * NR, 1), F32)
    for h in range(HB):
        slope = jnp.where(head == h, slopes[h], slope)
    col = lax.broadcasted_iota(jnp.int32, (1, PP * PAGE), 1)

    def attend(k_tile, v_tile, n_tiles, kpos, ok, fresh):
        s = jnp.concatenate(
            [jnp.concatenate([_dot_nt(qs[h], k_tile(h, j)) for j in range(n_tiles)], axis=1)
             for h in range(HB)], axis=0)
        s = s + slope * kpos
        if ok is not None:
            s = jnp.where(ok, s, NEG)
        m_prev, l_prev, acc_prev = m_scr[...], l_scr[...], acc_scr[...]
        if fresh is not None:
            m_prev = jnp.where(fresh, NEG, m_prev)
            l_prev = jnp.where(fresh, 0.0, l_prev)
            acc_prev = jnp.where(fresh, 0.0, acc_prev)
        m_new = jnp.maximum(m_prev, jnp.max(s, axis=-1, keepdims=True))
        alpha = jnp.exp2(m_prev - m_new)
        p = jnp.exp2(s - m_new)
        l_scr[...] = alpha * l_prev + jnp.sum(p, axis=-1, keepdims=True)
        pb = p.astype(BF16)
        pv = []
        for h in range(HB):
            out = _dot(pb[h * NR:(h + 1) * NR, 0:PAGE], v_tile(h, 0))
            for j in range(1, n_tiles):
                out = out + _dot(pb[h * NR:(h + 1) * NR, j * PAGE:(j + 1) * PAGE], v_tile(h, j))
            pv.append(out)
        acc_scr[...] = alpha * acc_prev + jnp.concatenate(pv, axis=0)
        m_scr[...] = m_new

    def hgrn_chunk(n):
        rows = slice(n * C, (n + 1) * C)
        f = lb + (1.0 - lb) * jax.nn.sigmoid(pa_ref[rows, W:2 * W])
        g = jnp.log2(f)
        kk = 1.0 - f
        q = pa_ref[rows, 0:W]
        g_hi = g.astype(BF16)
        r1 = g - g_hi.astype(F32)
        g_mid = r1.astype(BF16)
        g_lo = (r1 - g_mid.astype(F32)).astype(BF16)
        b = _dot(lmat, g_hi) + _dot(lmat, g_mid) + _dot(lmat, g_lo)
        b_scr[rows, :] = b
        bl = b_scr[(n + 1) * C - 1:(n + 1) * C, :]
        levels = []
        m = C // 2
        while m >= 1:
            upper = (row & (2 * m - 1)) >= m
            if m >= SUBLANES:
                side, d = [], []
                for blk in range(C // (2 * m)):
                    lo = slice(blk * 2 * m, blk * 2 * m + m)
                    hi = slice(blk * 2 * m + m, (blk + 1) * 2 * m)
                    bref = b_scr[n * C + blk * 2 * m + m - 1:n * C + blk * 2 * m + m, :]
                    side += [kk[lo], q[hi]]
                    d += [bref - b[lo], b[hi] - bref]
                side = jnp.concatenate(side, axis=0)
                d = jnp.concatenate(d, axis=0)
            else:
                side = jnp.where(upper, q, kk)
                if m == 4:
                    bref = jnp.concatenate(
                        [jnp.broadcast_to(b_scr[n * C + blk * 8 + 3:n * C + blk * 8 + 4, :], (8, W))
                         for blk in range(C // 8)], axis=0)
                    d = -jnp.abs(b - bref)
                elif m == 2:
                    d = jnp.where(upper, g + jnp.where((row & 3) == 3, pltpu.roll(g, 1, 0), 0.0),
                                  jnp.where((row & 3) == 0, pltpu.roll(g, C - 1, 0), 0.0))
                else:
                    d = jnp.where(upper, g, 0.0)
            levels.append((side * jnp.exp2(d)).astype(BF16))
            m //= 2
        q_bf = q.astype(BF16)
        k_bf = kk.astype(BF16)
        q_in = (q * jnp.exp2(b)).astype(BF16)
        k_out = (kk * jnp.exp2(bl - b)).astype(BF16)
        ebl = jnp.exp2(bl)
        for h in range(H):
            ks = slice(h * DK, (h + 1) * DK)
            a = jnp.where(ti == si, _dot_nt(q_bf[:, ks], k_bf[:, ks]), 0.0)
            m = 1
            for x in reversed(levels):
                a = jnp.where(xor_lower >= m, _dot_nt(x[:, ks], x[:, ks]), a)
                m *= 2
            v = pa_ref[rows, 2 * W + h * DV:2 * W + (h + 1) * DV].astype(BF16)
            st = s_scr[h]
            o = _dot_nt(q_in[:, ks], st.astype(BF16)) + _dot(a.astype(BF16), v)
            oa_ref[rows, h * DV:(h + 1) * DV] = _rms(o, hng_ref[...]).astype(oa_ref.dtype)
            s_scr[h] = ebl[:, ks] * st + _dot_tn(v, k_out[:, ks])

    chunk_every = SUB // NC
    for j in range(SUB):
        g = fs * SUB + j
        ahead = g + (RING_SLOTS - 1)

        def start_ahead(ahead=ahead):
            for c in page_copies(ahead, lax.rem(ahead, RING_SLOTS)):
                c.start()

        if j + (RING_SLOTS - 1) < SUB:
            start_ahead()
        else:
            pl.when(fs < n_fused - 1)(start_ahead)
        slot = lax.rem(g, RING_SLOTS)
        for c in page_copies(g, slot):
            c.wait()
        k_pages = kbuf.at[slot]
        v_pages = vbuf.at[slot]
        kpos = ((part * SUB + j) * (PP * PAGE) + col - past).astype(F32)
        attend(lambda h, t, kp=k_pages: kp[t, pl.ds(h, PAGE, stride=HB), :].astype(BF16),
               lambda h, t, vp=v_pages: vp[t, pl.ds(h, PAGE, stride=HB), :].astype(BF16),
               PP, kpos, None, (part == 0) if j == 0 else None)
        if j % chunk_every == 0:
            hgrn_chunk(j // chunk_every)

    @pl.when(part == fused_per_seq - 1)
    def _():
        tq = lax.broadcasted_iota(jnp.int32, (HB * NR, PAGE), 0) & (R - 1)
        coln = lax.broadcasted_iota(jnp.int32, (HB * NR, PAGE), 1)
        pad = jnp.zeros((PAGE - R, E), F32)
        attend(lambda h, t: jnp.concatenate([kn_ref[pl.ds(h, R, stride=HB), :], pad], axis=0).astype(BF16),
               lambda h, t: jnp.concatenate([vn_ref[pl.ds(h, R, stride=HB), :], pad], axis=0).astype(BF16),
               1, coln[0:1].astype(F32), (coln <= tq) & (coln < T), None)
        o2 = acc_scr[...] / l_scr[...]
        for h in range(HB):
            o = o2[h * NR:h * NR + R] - lam_ref[0] * o2[h * NR + R:(h + 1) * NR]
            ob_ref[:, h * E:(h + 1) * E] = _rms(o, dng_ref[...]) * out_scale

    @pl.when(fs % hsteps == hsteps - 1)
    def _():
        sout_ref[...] = s_scr[...]


def _hgrn_sample_attn(pa, lg, hng, layer, H, DK, DV, q, kn, vn, cache_k, cache_v, page_ids, lam, dng,
                      T, HB, DH, out_scale, C=128, NC=4, PP=16):
    B, L, wp = pa.shape
    W = H * DK
    RB = C * NC
    SB = q.shape[0]
    n_pages = page_ids.shape[1]
    PAGE = cache_k.shape[1] // HB
    E = 2 * DH
    R = SAMPLE_ROWS
    slopes = tuple(LOG2E * 2.0 ** (-8.0 * (i + 1) / HB) for i in range(HB))
    hsteps = L // RB
    n_fused = B * hsteps
    sub_per_seq = n_pages // PP
    total_sub = SB * sub_per_seq
    SUB = total_sub // n_fused
    assert L % RB == 0 and wp == 2 * W + H * DV and n_pages % PP == 0 and q.shape[1] == R
    assert total_sub % n_fused == 0 and sub_per_seq % SUB == 0 and SUB % NC == 0 and SUB >= RING_SLOTS
    fused_per_seq = sub_per_seq // SUB
    seq_rows = pl.BlockSpec((None, R, HB * E), lambda i, pt: (i // fused_per_seq, 0, 0))
    new_rows = pl.BlockSpec((None, R * HB, E), lambda i, pt: (i // fused_per_seq, 0, 0))
    page_buf = pltpu.VMEM((RING_SLOTS, PP, PAGE * HB, E), F32)
    grid_spec = pltpu.PrefetchScalarGridSpec(
        num_scalar_prefetch=1,
        grid=(n_fused,),
        in_specs=[pl.BlockSpec(memory_space=pltpu.SMEM),
                  pl.BlockSpec((None, RB, wp), lambda i, pt: (i // hsteps, i % hsteps, 0)),
                  pl.BlockSpec(lg.shape, lambda i, pt: (0, 0)),
                  pl.BlockSpec((1, DV), lambda i, pt: (0, 0)),
                  seq_rows, new_rows, new_rows,
                  pl.BlockSpec((1, E), lambda i, pt: (0, 0)),
                  pl.BlockSpec(memory_space=pl.ANY),
                  pl.BlockSpec(memory_space=pl.ANY)],
        out_specs=[pl.BlockSpec((None, RB, H * DV), lambda i, pt: (i // hsteps, i % hsteps, 0)),
                   pl.BlockSpec((None, H, DV, DK), lambda i, pt: (i // hsteps, 0, 0, 0)),
                   seq_rows],
        scratch_shapes=[pltpu.VMEM((H, DV, DK), F32), pltpu.VMEM((RB, W), F32),
                        page_buf, page_buf, pltpu.SemaphoreType.DMA((RING_SLOTS,)),
                        pltpu.VMEM((HB * 2 * R, 1), F32), pltpu.VMEM((HB * 2 * R, 1), F32),
                        pltpu.VMEM((HB * 2 * R, E), F32)])
    return pl.pallas_call(
        functools.partial(_hgrn_sample_attn_kernel, C=C, NC=NC, layer=layer, H=H, DK=DK, DV=DV,
                          SUB=SUB, sub_per_seq=sub_per_seq, total_sub=total_sub, hsteps=hsteps,
                          PP=PP, PAGE=PAGE, T=T, HB=HB, DH=DH, past=n_pages * PAGE, slopes=slopes,
                          out_scale=out_scale),
        grid_spec=grid_spec,
        out_shape=[jax.ShapeDtypeStruct((B, L, H * DV), BF16),
                   jax.ShapeDtypeStruct((B, H, DV, DK), F32),
                   jax.ShapeDtypeStruct((SB, R, HB * E), F32)],
        compiler_params=_params("arbitrary"),
        name="hgrn_prompt_diff_sample",
    )(page_ids, lam, pa, lg, hng.reshape(1, DV), q, kn, vn, dng.reshape(1, E), cache_k, cache_v)


def _mem_kv_kernel(x_ref, g_ref, wt_ref, kt_ref, vt_ref):
    h = _rms(x_ref[...], g_ref[...]).astype(BF16)
    kv = _dot_nt(wt_ref[...], h)
    w = kt_ref.shape[0]
    kt_ref[...] = kv[:w]
    vt_ref[...] = kv[w:]


def _mem_kv(mem, g, wt_bf16):
    B, N, D = mem.shape
    w = wt_bf16.shape[0] // 2
    out = jax.ShapeDtypeStruct((B, w, N), F32)
    blk = pl.BlockSpec((None, w, N), lambda b: (b, 0, 0))
    return pl.pallas_call(
        _mem_kv_kernel,
        grid=(B,),
        in_specs=[pl.BlockSpec((None, N, D), lambda b: (b, 0, 0)),
                  pl.BlockSpec((1, D), lambda b: (0, 0)),
                  pl.BlockSpec((2 * w, D), lambda b: (0, 0))],
        out_specs=[blk, blk],
        out_shape=[out, out],
        compiler_params=_params("parallel"),
        name="mem_kv",
    )(mem, g.reshape(1, D), wt_bf16)


def _mem_attend(q, kt, vt, H, DH):
    T = q.shape[0]
    q = q * (LOG2E * DH ** -0.5)
    lane_h = lax.broadcasted_iota(jnp.int32, q.shape, 1) // DH
    qs = jnp.concatenate([jnp.where(lane_h == h, q, 0.0) for h in range(H)], axis=0).astype(BF16)
    s = _dot(qs, kt.astype(BF16))
    p = jnp.exp2(s - jnp.max(s, axis=-1, keepdims=True))
    o4 = _dot_nt(p.astype(BF16), vt.astype(BF16)) / jnp.sum(p, axis=-1, keepdims=True)
    o = jnp.zeros_like(q)
    for h in range(H):
        o = jnp.where(lane_h == h, o4[h * T:(h + 1) * T], o)
    return o


def _mem_attn_kernel(q_ref, kt_ref, vt_ref, o_ref, *, NB, H, DH):
    for n in range(NB):
        o_ref[n] = _mem_attend(q_ref[n], kt_ref[n], vt_ref[n], H, DH)


def _mem_attn(q, kt, vt, H, DH, NB=8):
    B, T, W = q.shape
    N = kt.shape[2]
    assert B % NB == 0
    kv = pl.BlockSpec((NB, W, N), lambda b: (b, 0, 0))
    return pl.pallas_call(
        functools.partial(_mem_attn_kernel, NB=NB, H=H, DH=DH),
        grid=(B // NB,),
        in_specs=[pl.BlockSpec((NB, T, W), lambda b: (b, 0, 0)), kv, kv],
        out_specs=pl.BlockSpec((NB, T, W), lambda b: (b, 0, 0)),
        out_shape=jax.ShapeDtypeStruct((B, T, W), F32),
        compiler_params=_params("parallel"),
        name="mem_attn",
    )(q, kt, vt)


def _out_kernel(x_ref, oa_ref, ob_ref, m_ref, z_ref, w_ref, g_ref, *rest, final, mem_heads):
    if mem_heads is None:
        (y_ref,) = rest
        om = m_ref[...]
    else:
        kt_ref, vt_ref, y_ref = rest
        om = _mem_attend(m_ref[...], kt_ref[...], vt_ref[...], *mem_heads)
    o = jnp.concatenate([oa_ref[...].astype(F32), ob_ref[...].astype(F32), om], axis=-1)
    z = z_ref[...].astype(F32)
    o = (o * (z * jax.nn.sigmoid(z))).astype(BF16)
    y = x_ref[...] + _dot(o, w_ref[...])
    y_ref[...] = _rms(y, g_ref[...]) if final else y


def _out_proj(x2d, oa, ob, m, z, w_bf16, g, tm, final, mem=None):
    rows_total, d = x2d.shape
    mix = w_bf16.shape[0]

    def rows(a):
        return pl.BlockSpec((tm, a.shape[1]), lambda i: (i, 0))

    in_specs = [rows(x2d), rows(oa), rows(ob), rows(m), rows(z),
                pl.BlockSpec((mix, d), lambda i: (0, 0)),
                pl.BlockSpec((1, d), lambda i: (0, 0))]
    args = [x2d, oa, ob, m, z, w_bf16, g.reshape(1, d)]
    mem_heads = None
    if mem is not None:
        kt, vt, H, DH, rows_per_batch = mem
        assert rows_per_batch % tm == 0
        tiles = rows_per_batch // tm
        kv = pl.BlockSpec((None,) + kt.shape[1:], lambda i: (i // tiles, 0, 0))
        in_specs += [kv, kv]
        args += [kt, vt]
        mem_heads = (H, DH)
    return pl.pallas_call(
        functools.partial(_out_kernel, final=final, mem_heads=mem_heads),
        grid=(rows_total // tm,),
        in_specs=in_specs,
        out_specs=rows(x2d),
        out_shape=jax.ShapeDtypeStruct((rows_total, d), F32),
        compiler_params=_params("parallel"),
        name="out_proj",
    )(*args)


def kernel(x_prompt, x_sample, mem_prompt, cache_k, cache_v, state_hgrn, cache_mem_k, cache_mem_v, page_table, norm_g, w_in, hgrn_lb_logits, hgrn_norm_g, diff_norm_g, lambda_q1, lambda_k1, lambda_q2, lambda_k2, mem_norm_g, w_mem_kv, w_out, final_g):
    B, L, D = x_prompt.shape
    SB, T, _ = x_sample.shape
    depth, _, H_A, DK_A, DV_A = state_hgrn.shape
    _, n_pool, PAGE, H_B, E_B = cache_k.shape
    DH_B = E_B // 2
    _, _, N_MEM, H_M, DH_M = cache_mem_k.shape
    W_A, W_K, W_B, W_M = H_A * DV_A, H_A * DK_A, H_B * E_B, H_M * DH_M
    MIX = W_A + W_B + W_M
    R = SAMPLE_ROWS
    TM = ROW_TILE
    assert T <= R and L % TM == 0
    sizes = (2 * W_K + W_A, W_B, W_M, MIX)
    diff_scale = LOG2E * DH_B ** -0.5

    cache_k3 = cache_k.reshape(depth * n_pool, PAGE * H_B, E_B)
    cache_v3 = cache_v.reshape(depth * n_pool, PAGE * H_B, E_B)

    yp = x_prompt.reshape(B * L, D)
    ys = jnp.pad(x_sample, ((0, 0), (0, R - T), (0, 0))).reshape(SB * R, D)
    outs = {k: [] for k in ("kp", "vp", "sp", "mkp", "mvp", "ks", "vs", "ss")}
    for l in range(depth):
        lam_init = 0.8 - 0.6 * math.exp(-0.3 * l)
        lam = (jnp.exp(jnp.sum(lambda_q1[l] * lambda_k1[l])) - jnp.exp(jnp.sum(lambda_q2[l] * lambda_k2[l]))
               + lam_init).reshape(1).astype(F32)
        w_in_b = w_in[l].astype(BF16)
        w_out_b = w_out[l].astype(BF16)
        w_mem_t = w_mem_kv[l].T.astype(BF16)

        def channel_major(a):
            return jnp.transpose(a, (0, 2, 3, 1)).reshape(a.shape[0], W_M, N_MEM)

        def token_major(a):
            return jnp.transpose(a.reshape(a.shape[0], H_M, DH_M, N_MEM), (0, 3, 1, 2))

        mkt, mvt = _mem_kv(mem_prompt, mem_norm_g[l], w_mem_t)
        pa, dq, kb, vb, ko, vo, mq, z = _in_proj(yp, norm_g[l], w_in_b, sizes, H_B, TM, diff_scale, True,
                                                 "in_proj_prompt")
        pa_s, dq_s, ko_s, vo_s, mq_s, z_s = _in_proj(ys, norm_g[l], w_in_b, sizes, H_B, SB * R, diff_scale, False,
                                                     "in_proj_sample")
        oa, st, ob_s = _hgrn_sample_attn(
            pa.reshape(B, L, -1), hgrn_lb_logits, hgrn_norm_g[l], l, H_A, DK_A, DV_A,
            dq_s.reshape(SB, R, W_B), ko_s.reshape(SB, R * H_B, E_B), vo_s.reshape(SB, R * H_B, E_B),
            cache_k3, cache_v3, page_table + l * n_pool, lam, diff_norm_g[l], T, H_B, DH_B, 1.0 - lam_init)

        ob = _diff_prompt(dq.reshape(B, L, W_B), kb.reshape(B, L, W_B), vb.reshape(B, L, W_B),
                          lam, diff_norm_g[l], H_B, DH_B, 1.0 - lam_init)
        yp = _out_proj(yp, oa.reshape(B * L, W_A), ob.reshape(B * L, W_B), mq, z,
                       w_out_b, final_g, 2 * TM, l == depth - 1, mem=(mkt, mvt, H_M, DH_M, L))
        outs["kp"].append(ko.reshape(B, L, H_B, E_B))
        outs["vp"].append(vo.reshape(B, L, H_B, E_B))
        outs["sp"].append(jnp.swapaxes(st, -1, -2))
        outs["mkp"].append(token_major(mkt))
        outs["mvp"].append(token_major(mvt))

        oa_s, sst = _hgrn_sample(pa_s.reshape(SB, R, -1), hgrn_lb_logits, hgrn_norm_g[l],
                                 jnp.swapaxes(state_hgrn[l], -1, -2), l, T, H_A, DK_A, DV_A)
        om_s = _mem_attn(mq_s.reshape(SB, R, W_M), channel_major(cache_mem_k[l]), channel_major(cache_mem_v[l]),
                         H_M, DH_M)
        ys = _out_proj(ys, oa_s.reshape(SB * R, W_A), ob_s.reshape(SB * R, W_B), om_s.reshape(SB * R, W_M), z_s,
                       w_out_b, final_g, SB * R, l == depth - 1)
        outs["ks"].append(ko_s.reshape(SB, R, H_B, E_B)[:, :T])
        outs["vs"].append(vo_s.reshape(SB, R, H_B, E_B)[:, :T])
        outs["ss"].append(jnp.swapaxes(sst, -1, -2))

    y_prompt = yp.reshape(B, L, D)
    y_sample = ys.reshape(SB, R, D)[:, :T]
    return (y_prompt, y_sample, jnp.stack(outs["kp"]), jnp.stack(outs["vp"]), jnp.stack(outs["sp"]),
            jnp.stack(outs["mkp"]), jnp.stack(outs["mvp"]), jnp.stack(outs["ks"]), jnp.stack(outs["vs"]),
            jnp.stack(outs["ss"]))
```

```python
import functools
import math

import jax
import jax.numpy as jnp
from jax import lax
from jax.experimental import pallas as pl
from jax.experimental.pallas import tpu as pltpu

F32 = jnp.float32
BF16 = jnp.bfloat16
EPS = 1e-6
NEG = -1e30
LOG2E = math.log2(math.e)
VMEM_LIMIT = 48 * 1024 * 1024
SUBLANES = 8
SAMPLE_ROWS = SUBLANES
ROW_TILE = 512
RING_SLOTS = 4
PAGE_PREFETCH = 2

_NT = (((1,), (1,)), ((), ()))
_TN = (((0,), (0,)), ((), ()))


def _dot(a, b):
    return jnp.dot(a, b, preferred_element_type=F32)


def _dot_nt(a, b):
    return lax.dot_general(a, b, _NT, preferred_element_type=F32)


def _dot_tn(a, b):
    return lax.dot_general(a, b, _TN, preferred_element_type=F32)


def _rms(x, g):
    return x * lax.rsqrt(jnp.mean(x * x, axis=-1, keepdims=True) + EPS) * g


def _params(*sem):
    return pltpu.CompilerParams(dimension_semantics=sem, vmem_limit_bytes=VMEM_LIMIT)


def _emit_interleaved(streams):
    streams = list(streams)
    while streams:
        for gen in list(streams):
            try:
                next(gen)
            except StopIteration:
                streams.remove(gen)


def _split_halves(q, dh):
    lane = lax.broadcasted_iota(jnp.int32, q.shape, 1)
    zero = jnp.zeros_like(q)
    return jnp.concatenate([jnp.where(lane < dh, q, zero), jnp.where(lane < dh, zero, q)], axis=0)


def _in_proj_kernel(x_ref, g_ref, w_ref, *rest, sizes, H, prompt, scale):
    w_pa, w_b, w_m, mix = sizes
    if prompt:
        pa_ref, dq_ref, kb_ref, vb_ref, ko_ref, vo_ref, mq_ref, z_ref = rest
    else:
        pa_ref, dq_ref, ko_ref, vo_ref, mq_ref, z_ref = rest
    e = w_b // H
    h = _rms(x_ref[...], g_ref[...]).astype(BF16)
    c = 0
    pa_ref[...] = _dot(h, w_ref[:, c:c + w_pa])
    c += w_pa
    dq_ref[...] = (_dot(h, w_ref[:, c:c + w_b]) * scale).astype(BF16)
    c += w_b
    k = _dot(h, w_ref[:, c:c + w_b])
    c += w_b
    v = _dot(h, w_ref[:, c:c + w_b])
    c += w_b
    tm = k.shape[0]
    for hh in range(H):
        ko_ref[pl.ds(hh, tm, stride=H), :] = k[:, hh * e:(hh + 1) * e]
        vo_ref[pl.ds(hh, tm, stride=H), :] = v[:, hh * e:(hh + 1) * e]
    if prompt:
        kb_ref[...] = k.astype(BF16)
        vb_ref[...] = v.astype(BF16)
    mq_ref[...] = _dot(h, w_ref[:, c:c + w_m])
    c += w_m
    z_ref[...] = _dot(h, w_ref[:, c:c + mix]).astype(z_ref.dtype)


def _in_proj(x2d, g, w_bf16, sizes, H, tm, scale, prompt, name):
    m, d = x2d.shape
    n = w_bf16.shape[1]
    w_pa, w_b, w_m, mix = sizes
    e = w_b // H
    assert w_pa + 3 * w_b + w_m + mix == n and m % tm == 0

    def rows(width):
        return pl.BlockSpec((tm, width), lambda i: (i, 0))

    const = lambda i: (0, 0)
    in_specs = [rows(d), pl.BlockSpec((1, d), const), pl.BlockSpec((d, n), const)]
    args = [x2d, g.reshape(1, d), w_bf16]
    out_specs = [rows(w_pa), rows(w_b)]
    out_shape = [jax.ShapeDtypeStruct((m, w_pa), F32), jax.ShapeDtypeStruct((m, w_b), BF16)]
    if prompt:
        out_specs += [rows(w_b), rows(w_b)]
        out_shape += [jax.ShapeDtypeStruct((m, w_b), BF16), jax.ShapeDtypeStruct((m, w_b), BF16)]
    head_rows = pl.BlockSpec((tm * H, e), lambda i: (i, 0))
    out_specs += [head_rows, head_rows, rows(w_m), rows(mix)]
    out_shape += [jax.ShapeDtypeStruct((m * H, e), F32), jax.ShapeDtypeStruct((m * H, e), F32),
                  jax.ShapeDtypeStruct((m, w_m), F32), jax.ShapeDtypeStruct((m, mix), BF16)]
    return pl.pallas_call(
        functools.partial(_in_proj_kernel, sizes=sizes, H=H, prompt=prompt, scale=scale),
        grid=(m // tm,),
        in_specs=in_specs,
        out_specs=out_specs,
        out_shape=out_shape,
        compiler_params=_params("parallel"),
        name=name,
    )(*args)


def _lower_bound(lg, layer, axis=0):
    e = jnp.exp(lg - jnp.max(lg, axis=axis, keepdims=True))
    p = e / jnp.sum(e, axis=axis, keepdims=True)
    head = p[:layer + 1] if axis == 0 else p[:, :layer + 1]
    return jnp.sum(head, axis=axis, keepdims=True)


def _hgrn_sample_kernel(pa_ref, lg_ref, ng_ref, s0_ref, o_ref, sout_ref, *, NB, T, layer, H, DK, DV):
    W = H * DK
    R = pa_ref.shape[1]
    lb = _lower_bound(lg_ref[...], layer)
    row = lax.broadcasted_iota(jnp.int32, (R, 1), 0)
    valid = row < T
    for n in range(NB):
        f = lb + (1.0 - lb) * jax.nn.sigmoid(pa_ref[n, :, W:2 * W])
        g = jnp.where(valid, jnp.log2(f), 0.0)
        kk = jnp.where(valid, 1.0 - f, 0.0)
        q = pa_ref[n, :, 0:W]
        b = g
        sh = 1
        while sh < R:
            b = b + jnp.where(row >= sh, pltpu.roll(b, sh, 0), 0.0)
            sh *= 2
        bl = b[R - 1:R]
        q_in = (q * jnp.exp2(b)).astype(BF16)
        k_out = (kk * jnp.exp2(bl - b)).astype(BF16)
        ebl = jnp.exp2(bl)
        for h in range(H):
            ks = slice(h * DK, (h + 1) * DK)
            v = pa_ref[n, :, 2 * W + h * DV:2 * W + (h + 1) * DV]
            st = s0_ref[n, h]
            o = _dot_nt(q_in[:, ks], st.astype(BF16))
            for s in range(T):
                later = row >= s
                decay = jnp.exp2(jnp.where(later, b[:, ks] - b[s:s + 1, ks], 0.0))
                a_s = jnp.sum(jnp.where(later, q[:, ks] * kk[s:s + 1, ks] * decay, 0.0), axis=1, keepdims=True)
                o = o + a_s * v[s:s + 1, :]
            o_ref[n, :, h * DV:(h + 1) * DV] = _rms(o, ng_ref[...])
            sout_ref[n, h] = ebl[:, ks] * st + _dot_tn(v.astype(BF16), k_out[:, ks])


def _hgrn_sample(pa, lg, ng, s0t, layer, T, H, DK, DV, NB=4):
    B, R, wp = pa.shape
    assert B % NB == 0
    return pl.pallas_call(
        functools.partial(_hgrn_sample_kernel, NB=NB, T=T, layer=layer, H=H, DK=DK, DV=DV),
        grid=(B // NB,),
        in_specs=[pl.BlockSpec((NB, R, wp), lambda b: (b, 0, 0)),
                  pl.BlockSpec(lg.shape, lambda b: (0, 0)),
                  pl.BlockSpec((1, DV), lambda b: (0, 0)),
                  pl.BlockSpec((NB, H, DV, DK), lambda b: (b, 0, 0, 0))],
        out_specs=[pl.BlockSpec((NB, R, H * DV), lambda b: (b, 0, 0)),
                   pl.BlockSpec((NB, H, DV, DK), lambda b: (b, 0, 0, 0))],
        out_shape=[jax.ShapeDtypeStruct((B, R, H * DV), F32),
                   jax.ShapeDtypeStruct((B, H, DV, DK), F32)],
        compiler_params=_params("parallel"),
        name="hgrn_sample",
    )(pa, lg, ng.reshape(1, DV), s0t)


def _diff_prompt_kernel(slope_ref, lam_ref, q_ref, k_ref, v_ref, g_ref, o_ref, m_scr, l_scr, acc_scr,
                        *, TQ, DH, HP, out_scale):
    hg = pl.program_id(1)
    qi = pl.program_id(2)
    E = 2 * DH
    LANES = m_scr.shape[2]
    trel = lax.broadcasted_iota(jnp.int32, (2 * TQ, TQ), 0) & (TQ - 1)
    srel = lax.broadcasted_iota(jnp.int32, (2 * TQ, TQ), 1)
    key_pos = lax.broadcasted_iota(jnp.int32, (1, TQ), 1).astype(F32)
    slopes = [slope_ref[hg * HP + j] for j in range(HP)]
    qs = [_split_halves(q_ref[:, j * E:(j + 1) * E], DH) for j in range(HP)]

    m_scr[...] = jnp.full_like(m_scr, NEG)
    l_scr[...] = jnp.zeros_like(l_scr)
    acc_scr[...] = jnp.zeros_like(acc_scr)

    def block(kj, masked):
        rows = pl.ds(pl.multiple_of(kj * TQ, TQ), TQ)
        for j in range(HP):
            cols = slice(j * E, (j + 1) * E)
            s = _dot_nt(qs[j], k_ref[rows, cols])
            s = s + slopes[j] * (key_pos + ((kj - qi) * TQ).astype(F32))
            if masked:
                s = jnp.where(srel <= trel, s, NEG)
            m_prev = m_scr[j]
            m_new = jnp.maximum(m_prev, jnp.max(s, axis=1, keepdims=True))
            alpha = jnp.exp2(m_prev - m_new)
            p = jnp.exp2(s - jnp.tile(m_new, (1, TQ // LANES)))
            l_scr[j] = alpha * l_scr[j] + jnp.sum(p, axis=1, keepdims=True)
            acc_scr[j] = alpha * acc_scr[j] + _dot(p.astype(BF16), v_ref[rows, cols])
            m_scr[j] = m_new

    def body(kj, carry):
        block(kj, False)
        return carry

    lax.fori_loop(0, qi, body, 0)
    block(qi, True)

    for j in range(HP):
        o2 = acc_scr[j] / l_scr[j]
        o = o2[:TQ] - lam_ref[0] * o2[TQ:]
        o_ref[:, j * E:(j + 1) * E] = (_rms(o, g_ref[...]) * out_scale).astype(o_ref.dtype)


def _diff_prompt(q, kb, vb, lam, ng, H, DH, out_scale, TQ=512, HP=4):
    B, L, W = q.shape
    E = 2 * DH
    slopes = jnp.asarray([LOG2E * 2.0 ** (-8.0 * (i + 1) / H) for i in range(H)], F32)
    assert L % TQ == 0 and TQ & (TQ - 1) == 0 and H % HP == 0
    return pl.pallas_call(
        functools.partial(_diff_prompt_kernel, TQ=TQ, DH=DH, HP=HP, out_scale=out_scale),
        grid=(B, H // HP, L // TQ),
        in_specs=[pl.BlockSpec(memory_space=pltpu.SMEM),
                  pl.BlockSpec(memory_space=pltpu.SMEM),
                  pl.BlockSpec((None, TQ, HP * E), lambda b, h, i: (b, i, h)),
                  pl.BlockSpec((None, L, HP * E), lambda b, h, i: (b, 0, h)),
                  pl.BlockSpec((None, L, HP * E), lambda b, h, i: (b, 0, h)),
                  pl.BlockSpec((1, E), lambda b, h, i: (0, 0))],
        out_specs=pl.BlockSpec((None, TQ, HP * E), lambda b, h, i: (b, i, h)),
        out_shape=jax.ShapeDtypeStruct((B, L, W), BF16),
        scratch_shapes=[pltpu.VMEM((HP, 2 * TQ, E), F32), pltpu.VMEM((HP, 2 * TQ, E), F32),
                        pltpu.VMEM((HP, 2 * TQ, E), F32)],
        compiler_params=_params("parallel", "parallel", "arbitrary"),
        name="diff_prompt",
    )(slopes, lam, q, kb, vb, ng.reshape(1, E))


def _hgrn_sample_attn_kernel(pt_ref, lam_ref, pa_ref, lg_ref, hng_ref, q_ref, kn_ref, vn_ref, dng_ref,
                             ck_ref, cv_ref, oa_ref, sout_ref, ob_ref,
                             s_scr, b_scr, kbuf, vbuf, sem, m_scr, l_scr, acc_scr,
                             *, C, NC, layer, H, DK, DV, SUB, sub_per_seq, total_sub, hsteps,
                             PP, PAGE, T, HB, DH, past, slopes, out_scale):
    fs = pl.program_id(0)
    n_fused = pl.num_programs(0)
    R = SAMPLE_ROWS
    NR = 2 * R
    E = 2 * DH
    W = H * DK
    fused_per_seq = sub_per_seq // SUB
    part = fs % fused_per_seq

    def page_copies(g, slot):
        seq = g // sub_per_seq
        first = (g % sub_per_seq) * PP
        copies = []
        for j in range(PP):
            page = pt_ref[seq, first + j]
            copies.append(pltpu.make_async_copy(ck_ref.at[page], kbuf.at[slot, j], sem.at[slot]))
            copies.append(pltpu.make_async_copy(cv_ref.at[page], vbuf.at[slot, j], sem.at[slot]))
        return copies

    @pl.when(fs == 0)
    def _():
        for g in range(min(PAGE_PREFETCH, total_sub)):
            for c in page_copies(g, g):
                c.start()
        m_scr[...] = jnp.full_like(m_scr, NEG)
        l_scr[...] = jnp.zeros_like(l_scr)
        acc_scr[...] = jnp.zeros_like(acc_scr)

    @pl.when(fs % hsteps == 0)
    def _():
        s_scr[...] = jnp.zeros_like(s_scr)

    lb = _lower_bound(lg_ref[...], layer)
    ti = lax.broadcasted_iota(jnp.int32, (C, C), 0)
    si = lax.broadcasted_iota(jnp.int32, (C, C), 1)
    lmat = (si <= ti).astype(BF16)
    xor_lower = jnp.where(ti > si, ti ^ si, 0)
    row = lax.broadcasted_iota(jnp.int32, (C, 1), 0)

    qs = [_split_halves(q_ref[:, h * E:(h + 1) * E], DH) for h in range(HB)]
    head = lax.broadcasted_iota(jnp.int32, (HB * NR, 1), 0) // NR
    slope = jnp.zeros((HB * NR, 1), F32)
    for h in range(HB):
        slope = jnp.where(head == h, slopes[h], slope)
    col = lax.broadcasted_iota(jnp.int32, (1, PP * PAGE), 1)

    def attend(k_tile, v_tile, n_tiles, kpos, ok, fresh):
        s = jnp.concatenate(
            [jnp.concatenate([_dot_nt(qs[h], k_tile(h, j)) for j in range(n_tiles)], axis=1)
             for h in range(HB)], axis=0)
        yield
        s = s + slope * kpos
        if ok is not None:
            s = jnp.where(ok, s, NEG)
        m_prev, l_prev, acc_prev = m_scr[...], l_scr[...], acc_scr[...]
        if fresh is not None:
            m_prev = jnp.where(fresh, NEG, m_prev)
            l_prev = jnp.where(fresh, 0.0, l_prev)
            acc_prev = jnp.where(fresh, 0.0, acc_prev)
        m_new = jnp.maximum(m_prev, jnp.max(s, axis=-1, keepdims=True))
        alpha = jnp.exp2(m_prev - m_new)
        p = jnp.exp2(s - m_new)
        l_scr[...] = alpha * l_prev + jnp.sum(p, axis=-1, keepdims=True)
        pb = p.astype(BF16)
        yield
        pv = []
        for h in range(HB):
            out = _dot(pb[h * NR:(h + 1) * NR, 0:PAGE], v_tile(h, 0))
            for j in range(1, n_tiles):
                out = out + _dot(pb[h * NR:(h + 1) * NR, j * PAGE:(j + 1) * PAGE], v_tile(h, j))
            pv.append(out)
        acc_scr[...] = alpha * acc_prev + jnp.concatenate(pv, axis=0)
        m_scr[...] = m_new

    def hgrn_chunk(n):
        rows = slice(n * C, (n + 1) * C)
        f = lb + (1.0 - lb) * jax.nn.sigmoid(pa_ref[rows, W:2 * W])
        g = jnp.log2(f)
        kk = 1.0 - f
        q = pa_ref[rows, 0:W]
        g_hi = g.astype(BF16)
        r1 = g - g_hi.astype(F32)
        g_mid = r1.astype(BF16)
        g_lo = (r1 - g_mid.astype(F32)).astype(BF16)
        b = _dot(lmat, g_hi) + _dot(lmat, g_mid) + _dot(lmat, g_lo)
        b_scr[rows, :] = b
        bl = b_scr[(n + 1) * C - 1:(n + 1) * C, :]
        yield
        levels = []
        m = C // 2
        while m >= 1:
            upper = (row & (2 * m - 1)) >= m
            if m >= SUBLANES:
                side, d = [], []
                for blk in range(C // (2 * m)):
                    lo = slice(blk * 2 * m, blk * 2 * m + m)
                    hi = slice(blk * 2 * m + m, (blk + 1) * 2 * m)
                    bref = b_scr[n * C + blk * 2 * m + m - 1:n * C + blk * 2 * m + m, :]
                    side += [kk[lo], q[hi]]
                    d += [bref - b[lo], b[hi] - bref]
                side = jnp.concatenate(side, axis=0)
                d = jnp.concatenate(d, axis=0)
            else:
                side = jnp.where(upper, q, kk)
                if m == 4:
                    bref = jnp.concatenate(
                        [jnp.broadcast_to(b_scr[n * C + blk * 8 + 3:n * C + blk * 8 + 4, :], (8, W))
                         for blk in range(C // 8)], axis=0)
                    d = -jnp.abs(b - bref)
                elif m == 2:
                    d = jnp.where(upper, g + jnp.where((row & 3) == 3, pltpu.roll(g, 1, 0), 0.0),
                                  jnp.where((row & 3) == 0, pltpu.roll(g, C - 1, 0), 0.0))
                else:
                    d = jnp.where(upper, g, 0.0)
            levels.append((side * jnp.exp2(d)).astype(BF16))
            m //= 2
        q_bf = q.astype(BF16)
        k_bf = kk.astype(BF16)
        q_in = (q * jnp.exp2(b)).astype(BF16)
        k_out = (kk * jnp.exp2(bl - b)).astype(BF16)
        ebl = jnp.exp2(bl)
        yield
        for h in range(H):
            ks = slice(h * DK, (h + 1) * DK)
            a = jnp.where(ti == si, _dot_nt(q_bf[:, ks], k_bf[:, ks]), 0.0)
            m = 1
            for x in reversed(levels):
                a = jnp.where(xor_lower >= m, _dot_nt(x[:, ks], x[:, ks]), a)
                m *= 2
            v = pa_ref[rows, 2 * W + h * DV:2 * W + (h + 1) * DV].astype(BF16)
            st = s_scr[h]
            o = _dot_nt(q_in[:, ks], st.astype(BF16)) + _dot(a.astype(BF16), v)
            oa_ref[rows, h * DV:(h + 1) * DV] = _rms(o, hng_ref[...]).astype(oa_ref.dtype)
            s_scr[h] = ebl[:, ks] * st + _dot_tn(v, k_out[:, ks])

    chunk_every = SUB // NC
    for j in range(SUB):
        g = fs * SUB + j
        ahead = g + PAGE_PREFETCH

        def start_ahead(ahead=ahead, slot=(j + PAGE_PREFETCH) % RING_SLOTS):
            for c in page_copies(ahead, slot):
                c.start()

        if j + PAGE_PREFETCH < SUB:
            start_ahead()
        else:
            pl.when(fs < n_fused - 1)(start_ahead)
        slot = j % RING_SLOTS
        for c in page_copies(g, slot):
            c.wait()
        k_pages = kbuf.at[slot]
        v_pages = vbuf.at[slot]
        kpos = ((part * SUB + j) * (PP * PAGE) + col - past).astype(F32)
        streams = [attend(lambda h, t, kp=k_pages: kp[t, pl.ds(h, PAGE, stride=HB), :].astype(BF16),
                          lambda h, t, vp=v_pages: vp[t, pl.ds(h, PAGE, stride=HB), :].astype(BF16),
                          PP, kpos, None, (part == 0) if j == 0 else None)]
        if j % chunk_every == 0:
            streams.insert(0, hgrn_chunk(j // chunk_every))
        _emit_interleaved(streams)

    @pl.when(part == fused_per_seq - 1)
    def _():
        tq = lax.broadcasted_iota(jnp.int32, (HB * NR, PAGE), 0) & (R - 1)
        coln = lax.broadcasted_iota(jnp.int32, (HB * NR, PAGE), 1)
        pad = jnp.zeros((PAGE - R, E), F32)
        _emit_interleaved([attend(
            lambda h, t: jnp.concatenate([kn_ref[pl.ds(h, R, stride=HB), :], pad], axis=0).astype(BF16),
            lambda h, t: jnp.concatenate([vn_ref[pl.ds(h, R, stride=HB), :], pad], axis=0).astype(BF16),
            1, coln[0:1].astype(F32), (coln <= tq) & (coln < T), None)])
        o2 = acc_scr[...] / l_scr[...]
        for h in range(HB):
            o = o2[h * NR:h * NR + R] - lam_ref[0] * o2[h * NR + R:(h + 1) * NR]
            ob_ref[:, h * E:(h + 1) * E] = _rms(o, dng_ref[...]) * out_scale

    @pl.when(fs % hsteps == hsteps - 1)
    def _():
        sout_ref[...] = s_scr[...]


def _hgrn_sample_attn(pa, lg, hng, layer, H, DK, DV, q, kn, vn, cache_k, cache_v, page_ids, lam, dng,
                      T, HB, DH, out_scale, C=128, NC=4, PP=16):
    B, L, wp = pa.shape
    W = H * DK
    RB = C * NC
    SB = q.shape[0]
    n_pages = page_ids.shape[1]
    PAGE = cache_k.shape[1] // HB
    E = 2 * DH
    R = SAMPLE_ROWS
    slopes = tuple(LOG2E * 2.0 ** (-8.0 * (i + 1) / HB) for i in range(HB))
    hsteps = L // RB
    n_fused = B * hsteps
    sub_per_seq = n_pages // PP
    total_sub = SB * sub_per_seq
    SUB = total_sub // n_fused
    assert L % RB == 0 and wp == 2 * W + H * DV and n_pages % PP == 0 and q.shape[1] == R
    assert total_sub % n_fused == 0 and sub_per_seq % SUB == 0 and SUB % NC == 0 and SUB % RING_SLOTS == 0
    fused_per_seq = sub_per_seq // SUB
    seq_rows = pl.BlockSpec((None, R, HB * E), lambda i, pt: (i // fused_per_seq, 0, 0))
    new_rows = pl.BlockSpec((None, R * HB, E), lambda i, pt: (i // fused_per_seq, 0, 0))
    page_buf = pltpu.VMEM((RING_SLOTS, PP, PAGE * HB, E), F32)
    grid_spec = pltpu.PrefetchScalarGridSpec(
        num_scalar_prefetch=1,
        grid=(n_fused,),
        in_specs=[pl.BlockSpec(memory_space=pltpu.SMEM),
                  pl.BlockSpec((None, RB, wp), lambda i, pt: (i // hsteps, i % hsteps, 0)),
                  pl.BlockSpec(lg.shape, lambda i, pt: (0, 0)),
                  pl.BlockSpec((1, DV), lambda i, pt: (0, 0)),
                  seq_rows, new_rows, new_rows,
                  pl.BlockSpec((1, E), lambda i, pt: (0, 0)),
                  pl.BlockSpec(memory_space=pl.ANY),
                  pl.BlockSpec(memory_space=pl.ANY)],
        out_specs=[pl.BlockSpec((None, RB, H * DV), lambda i, pt: (i // hsteps, i % hsteps, 0)),
                   pl.BlockSpec((None, H, DV, DK), lambda i, pt: (i // hsteps, 0, 0, 0)),
                   seq_rows],
        scratch_shapes=[pltpu.VMEM((H, DV, DK), F32), pltpu.VMEM((RB, W), F32),
                        page_buf, page_buf, pltpu.SemaphoreType.DMA((RING_SLOTS,)),
                        pltpu.VMEM((HB * 2 * R, 1), F32), pltpu.VMEM((HB * 2 * R, 1), F32),
                        pltpu.VMEM((HB * 2 * R, E), F32)])
    return pl.pallas_call(
        functools.partial(_hgrn_sample_attn_kernel, C=C, NC=NC, layer=layer, H=H, DK=DK, DV=DV,
                          SUB=SUB, sub_per_seq=sub_per_seq, total_sub=total_sub, hsteps=hsteps,
                          PP=PP, PAGE=PAGE, T=T, HB=HB, DH=DH, past=n_pages * PAGE, slopes=slopes,
                          out_scale=out_scale),
        grid_spec=grid_spec,
        out_shape=[jax.ShapeDtypeStruct((B, L, H * DV), BF16),
                   jax.ShapeDtypeStruct((B, H, DV, DK), F32),
                   jax.ShapeDtypeStruct((SB, R, HB * E), F32)],
        compiler_params=_params("arbitrary"),
        name="hgrn_prompt_diff_sample",
    )(page_ids, lam, pa, lg, hng.reshape(1, DV), q, kn, vn, dng.reshape(1, E), cache_k, cache_v)


def _mem_kv_kernel(x_ref, g_ref, wt_ref, kt_ref, vt_ref):
    h = _rms(x_ref[...], g_ref[...]).astype(BF16)
    kv = _dot_nt(wt_ref[...], h)
    w = kt_ref.shape[0]
    kt_ref[...] = kv[:w]
    vt_ref[...] = kv[w:]


def _mem_kv(mem, g, wt_bf16):
    B, N, D = mem.shape
    w = wt_bf16.shape[0] // 2
    out = jax.ShapeDtypeStruct((B, w, N), F32)
    blk = pl.BlockSpec((None, w, N), lambda b: (b, 0, 0))
    return pl.pallas_call(
        _mem_kv_kernel,
        grid=(B,),
        in_specs=[pl.BlockSpec((None, N, D), lambda b: (b, 0, 0)),
                  pl.BlockSpec((1, D), lambda b: (0, 0)),
                  pl.BlockSpec((2 * w, D), lambda b: (0, 0))],
        out_specs=[blk, blk],
        out_shape=[out, out],
        compiler_params=_params("parallel"),
        name="mem_kv",
    )(mem, g.reshape(1, D), wt_bf16)


def _mem_attend(q, kt, vt, H, DH):
    T = q.shape[0]
    q = q * (LOG2E * DH ** -0.5)
    lane_h = lax.broadcasted_iota(jnp.int32, q.shape, 1) // DH
    qs = jnp.concatenate([jnp.where(lane_h == h, q, 0.0) for h in range(H)], axis=0).astype(BF16)
    s = _dot(qs, kt.astype(BF16))
    p = jnp.exp2(s - jnp.max(s, axis=-1, keepdims=True))
    o4 = _dot_nt(p.astype(BF16), vt.astype(BF16)) / jnp.sum(p, axis=-1, keepdims=True)
    o = jnp.zeros_like(q)
    for h in range(H):
        o = jnp.where(lane_h == h, o4[h * T:(h + 1) * T], o)
    return o


def _mem_attn_kernel(q_ref, kt_ref, vt_ref, o_ref, *, NB, H, DH):
    for n in range(NB):
        o_ref[n] = _mem_attend(q_ref[n], kt_ref[n], vt_ref[n], H, DH)


def _mem_attn(q, kt, vt, H, DH, NB=8):
    B, T, W = q.shape
    N = kt.shape[2]
    assert B % NB == 0
    kv = pl.BlockSpec((NB, W, N), lambda b: (b, 0, 0))
    return pl.pallas_call(
        functools.partial(_mem_attn_kernel, NB=NB, H=H, DH=DH),
        grid=(B // NB,),
        in_specs=[pl.BlockSpec((NB, T, W), lambda b: (b, 0, 0)), kv, kv],
        out_specs=pl.BlockSpec((NB, T, W), lambda b: (b, 0, 0)),
        out_shape=jax.ShapeDtypeStruct((B, T, W), F32),
        compiler_params=_params("parallel"),
        name="mem_attn",
    )(q, kt, vt)


def _out_kernel(x_ref, oa_ref, ob_ref, m_ref, z_ref, w_ref, g_ref, *rest, final, mem_heads):
    if mem_heads is None:
        (y_ref,) = rest
        om = m_ref[...]
    else:
        kt_ref, vt_ref, y_ref = rest
        om = _mem_attend(m_ref[...], kt_ref[...], vt_ref[...], *mem_heads)
    o = jnp.concatenate([oa_ref[...].astype(F32), ob_ref[...].astype(F32), om], axis=-1)
    z = z_ref[...].astype(F32)
    o = (o * (z * jax.nn.sigmoid(z))).astype(BF16)
    y = x_ref[...] + _dot(o, w_ref[...])
    y_ref[...] = _rms(y, g_ref[...]) if final else y


def _out_proj(x2d, oa, ob, m, z, w_bf16, g, tm, final, mem=None):
    rows_total, d = x2d.shape
    mix = w_bf16.shape[0]

    def rows(a):
        return pl.BlockSpec((tm, a.shape[1]), lambda i: (i, 0))

    in_specs = [rows(x2d), rows(oa), rows(ob), rows(m), rows(z),
                pl.BlockSpec((mix, d), lambda i: (0, 0)),
                pl.BlockSpec((1, d), lambda i: (0, 0))]
    args = [x2d, oa, ob, m, z, w_bf16, g.reshape(1, d)]
    mem_heads = None
    if mem is not None:
        kt, vt, H, DH, rows_per_batch = mem
        assert rows_per_batch % tm == 0
        tiles = rows_per_batch // tm
        kv = pl.BlockSpec((None,) + kt.shape[1:], lambda i: (i // tiles, 0, 0))
        in_specs += [kv, kv]
        args += [kt, vt]
        mem_heads = (H, DH)
    return pl.pallas_call(
        functools.partial(_out_kernel, final=final, mem_heads=mem_heads),
        grid=(rows_total // tm,),
        in_specs=in_specs,
        out_specs=rows(x2d),
        out_shape=jax.ShapeDtypeStruct((rows_total, d), F32),
        compiler_params=_params("parallel"),
        name="out_proj",
    )(*args)


def kernel(x_prompt, x_sample, mem_prompt, cache_k, cache_v, state_hgrn, cache_mem_k, cache_mem_v, page_table, norm_g, w_in, hgrn_lb_logits, hgrn_norm_g, diff_norm_g, lambda_q1, lambda_k1, lambda_q2, lambda_k2, mem_norm_g, w_mem_kv, w_out, final_g):
    B, L, D = x_prompt.shape
    SB, T, _ = x_sample.shape
    depth, _, H_A, DK_A, DV_A = state_hgrn.shape
    _, n_pool, PAGE, H_B, E_B = cache_k.shape
    DH_B = E_B // 2
    _, _, N_MEM, H_M, DH_M = cache_mem_k.shape
    W_A, W_K, W_B, W_M = H_A * DV_A, H_A * DK_A, H_B * E_B, H_M * DH_M
    MIX = W_A + W_B + W_M
    R = SAMPLE_ROWS
    TM = ROW_TILE
    assert T <= R and L % (2 * TM) == 0
    sizes = (2 * W_K + W_A, W_B, W_M, MIX)
    diff_scale = LOG2E * DH_B ** -0.5

    cache_k3 = cache_k.reshape(depth * n_pool, PAGE * H_B, E_B)
    cache_v3 = cache_v.reshape(depth * n_pool, PAGE * H_B, E_B)

    yp = x_prompt.reshape(B * L, D)
    ys = jnp.pad(x_sample, ((0, 0), (0, R - T), (0, 0))).reshape(SB * R, D)
    outs = {k: [] for k in ("kp", "vp", "sp", "mkp", "mvp", "ks", "vs", "ss")}
    for l in range(depth):
        lam_init = 0.8 - 0.6 * math.exp(-0.3 * l)
        lam = (jnp.exp(jnp.sum(lambda_q1[l] * lambda_k1[l])) - jnp.exp(jnp.sum(lambda_q2[l] * lambda_k2[l]))
               + lam_init).reshape(1).astype(F32)
        w_in_b = w_in[l].astype(BF16)
        w_out_b = w_out[l].astype(BF16)
        w_mem_t = w_mem_kv[l].T.astype(BF16)

        def channel_major(a):
            return jnp.transpose(a, (0, 2, 3, 1)).reshape(a.shape[0], W_M, N_MEM)

        def token_major(a):
            return jnp.transpose(a.reshape(a.shape[0], H_M, DH_M, N_MEM), (0, 3, 1, 2))

        mkt, mvt = _mem_kv(mem_prompt, mem_norm_g[l], w_mem_t)
        pa, dq, kb, vb, ko, vo, mq, z = _in_proj(yp, norm_g[l], w_in_b, sizes, H_B, TM, diff_scale, True,
                                                 "in_proj_prompt")
        pa_s, dq_s, ko_s, vo_s, mq_s, z_s = _in_proj(ys, norm_g[l], w_in_b, sizes, H_B, SB * R, diff_scale, False,
                                                     "in_proj_sample")
        oa, st, ob_s = _hgrn_sample_attn(
            pa.reshape(B, L, -1), hgrn_lb_logits, hgrn_norm_g[l], l, H_A, DK_A, DV_A,
            dq_s.reshape(SB, R, W_B), ko_s.reshape(SB, R * H_B, E_B), vo_s.reshape(SB, R * H_B, E_B),
            cache_k3, cache_v3, page_table + l * n_pool, lam, diff_norm_g[l], T, H_B, DH_B, 1.0 - lam_init)

        ob = _diff_prompt(dq.reshape(B, L, W_B), kb.reshape(B, L, W_B), vb.reshape(B, L, W_B),
                          lam, diff_norm_g[l], H_B, DH_B, 1.0 - lam_init)
        yp = _out_proj(yp, oa.reshape(B * L, W_A), ob.reshape(B * L, W_B), mq, z,
                       w_out_b, final_g, 2 * TM, l == depth - 1, mem=(mkt, mvt, H_M, DH_M, L))
        outs["kp"].append(ko.reshape(B, L, H_B, E_B))
        outs["vp"].append(vo.reshape(B, L, H_B, E_B))
        outs["sp"].append(jnp.swapaxes(st, -1, -2))
        outs["mkp"].append(token_major(mkt))
        outs["mvp"].append(token_major(mvt))

        oa_s, sst = _hgrn_sample(pa_s.reshape(SB, R, -1), hgrn_lb_logits, hgrn_norm_g[l],
                                 jnp.swapaxes(state_hgrn[l], -1, -2), l, T, H_A, DK_A, DV_A)
        om_s = _mem_attn(mq_s.reshape(SB, R, W_M), channel_major(cache_mem_k[l]), channel_major(cache_mem_v[l]),
                         H_M, DH_M)
        ys = _out_proj(ys, oa_s.reshape(SB * R, W_A), ob_s.reshape(SB * R, W_B), om_s.reshape(SB * R, W_M), z_s,
                       w_out_b, final_g, SB * R, l == depth - 1)
        outs["ks"].append(ko_s.reshape(SB, R, H_B, E_B)[:, :T])
        outs["vs"].append(vo_s.reshape(SB, R, H_B, E_B)[:, :T])
        outs["ss"].append(jnp.swapaxes(sst, -1, -2))

    y_prompt = yp.reshape(B, L, D)
    y_sample = ys.reshape(SB, R, D)[:, :T]
    return (y_prompt, y_sample, jnp.stack(outs["kp"]), jnp.stack(outs["vp"]), jnp.stack(outs["sp"]),
            jnp.stack(outs["mkp"]), jnp.stack(outs["mvp"]), jnp.stack(outs["ks"]), jnp.stack(outs["vs"]),
            jnp.stack(outs["ss"]))
```

```python
import functools
import math

import jax
import jax.numpy as jnp
from jax import lax
from jax.experimental import pallas as pl
from jax.experimental.pallas import tpu as pltpu

F32 = jnp.float32
BF16 = jnp.bfloat16
EPS = 1e-6
NEG = -1e30
LOG2E = math.log2(math.e)
VMEM_LIMIT = 48 * 1024 * 1024
SUBLANES = 8
SAMPLE_ROWS = SUBLANES
ROW_TILE = 512
RING_SLOTS = 4
PAGE_PREFETCH = 2

_NT = (((1,), (1,)), ((), ()))
_TN = (((0,), (0,)), ((), ()))


def _dot(a, b):
    return jnp.dot(a, b, preferred_element_type=F32)


def _dot_nt(a, b):
    return lax.dot_general(a, b, _NT, preferred_element_type=F32)


def _dot_tn(a, b):
    return lax.dot_general(a, b, _TN, preferred_element_type=F32)


def _rms(x, g):
    return x * lax.rsqrt(jnp.mean(x * x, axis=-1, keepdims=True) + EPS) * g


def _params(*sem):
    return pltpu.CompilerParams(dimension_semantics=sem, vmem_limit_bytes=VMEM_LIMIT)


def _emit_interleaved(streams):
    streams = list(streams)
    while streams:
        for gen in list(streams):
            try:
                next(gen)
            except StopIteration:
                streams.remove(gen)


def _split_halves(q, dh):
    lane = lax.broadcasted_iota(jnp.int32, q.shape, 1)
    zero = jnp.zeros_like(q)
    return jnp.concatenate([jnp.where(lane < dh, q, zero), jnp.where(lane < dh, zero, q)], axis=0)


def _in_proj_kernel(x_ref, g_ref, w_ref, *rest, sizes, H, prompt, scale):
    w_pa, w_b, w_m, mix = sizes
    if prompt:
        pa_ref, dq_ref, kb_ref, vb_ref, ko_ref, vo_ref, mq_ref, z_ref = rest
    else:
        pa_ref, dq_ref, ko_ref, vo_ref, mq_ref, z_ref = rest
    e = w_b // H
    h = _rms(x_ref[...], g_ref[...]).astype(BF16)
    c = 0
    pa_ref[...] = _dot(h, w_ref[:, c:c + w_pa])
    c += w_pa
    dq_ref[...] = (_dot(h, w_ref[:, c:c + w_b]) * scale).astype(BF16)
    c += w_b
    k = _dot(h, w_ref[:, c:c + w_b])
    c += w_b
    v = _dot(h, w_ref[:, c:c + w_b])
    c += w_b
    tm = k.shape[0]
    for hh in range(H):
        ko_ref[pl.ds(hh, tm, stride=H), :] = k[:, hh * e:(hh + 1) * e]
        vo_ref[pl.ds(hh, tm, stride=H), :] = v[:, hh * e:(hh + 1) * e]
    if prompt:
        kb_ref[...] = k.astype(BF16)
        vb_ref[...] = v.astype(BF16)
    mq_ref[...] = _dot(h, w_ref[:, c:c + w_m])
    c += w_m
    z_ref[...] = _dot(h, w_ref[:, c:c + mix]).astype(z_ref.dtype)


def _in_proj(x2d, g, w_bf16, sizes, H, tm, scale, prompt, name):
    m, d = x2d.shape
    n = w_bf16.shape[1]
    w_pa, w_b, w_m, mix = sizes
    e = w_b // H
    assert w_pa + 3 * w_b + w_m + mix == n and m % tm == 0

    def rows(width):
        return pl.BlockSpec((tm, width), lambda i: (i, 0))

    const = lambda i: (0, 0)
    in_specs = [rows(d), pl.BlockSpec((1, d), const), pl.BlockSpec((d, n), const)]
    args = [x2d, g.reshape(1, d), w_bf16]
    out_specs = [rows(w_pa), rows(w_b)]
    out_shape = [jax.ShapeDtypeStruct((m, w_pa), F32), jax.ShapeDtypeStruct((m, w_b), BF16)]
    if prompt:
        out_specs += [rows(w_b), rows(w_b)]
        out_shape += [jax.ShapeDtypeStruct((m, w_b), BF16), jax.ShapeDtypeStruct((m, w_b), BF16)]
    head_rows = pl.BlockSpec((tm * H, e), lambda i: (i, 0))
    out_specs += [head_rows, head_rows, rows(w_m), rows(mix)]
    out_shape += [jax.ShapeDtypeStruct((m * H, e), F32), jax.ShapeDtypeStruct((m * H, e), F32),
                  jax.ShapeDtypeStruct((m, w_m), F32), jax.ShapeDtypeStruct((m, mix), BF16)]
    return pl.pallas_call(
        functools.partial(_in_proj_kernel, sizes=sizes, H=H, prompt=prompt, scale=scale),
        grid=(m // tm,),
        in_specs=in_specs,
        out_specs=out_specs,
        out_shape=out_shape,
        compiler_params=_params("parallel"),
        name=name,
    )(*args)


def _lower_bound(lg, layer, axis=0):
    e = jnp.exp(lg - jnp.max(lg, axis=axis, keepdims=True))
    p = e / jnp.sum(e, axis=axis, keepdims=True)
    head = p[:layer + 1] if axis == 0 else p[:, :layer + 1]
    return jnp.sum(head, axis=axis, keepdims=True)


def _hgrn_sample_kernel(pa_ref, lg_ref, ng_ref, s0_ref, o_ref, sout_ref, *, NB, T, layer, H, DK, DV):
    W = H * DK
    R = pa_ref.shape[1]
    lb = _lower_bound(lg_ref[...], layer)
    row = lax.broadcasted_iota(jnp.int32, (R, 1), 0)
    valid = row < T
    for n in range(NB):
        f = lb + (1.0 - lb) * jax.nn.sigmoid(pa_ref[n, :, W:2 * W])
        g = jnp.where(valid, jnp.log2(f), 0.0)
        kk = jnp.where(valid, 1.0 - f, 0.0)
        q = pa_ref[n, :, 0:W]
        b = g
        sh = 1
        while sh < R:
            b = b + jnp.where(row >= sh, pltpu.roll(b, sh, 0), 0.0)
            sh *= 2
        bl = b[R - 1:R]
        q_in = (q * jnp.exp2(b)).astype(BF16)
        k_out = (kk * jnp.exp2(bl - b)).astype(BF16)
        ebl = jnp.exp2(bl)
        for h in range(H):
            ks = slice(h * DK, (h + 1) * DK)
            v = pa_ref[n, :, 2 * W + h * DV:2 * W + (h + 1) * DV]
            st = s0_ref[n, h]
            o = _dot_nt(q_in[:, ks], st.astype(BF16))
            for s in range(T):
                later = row >= s
                decay = jnp.exp2(jnp.where(later, b[:, ks] - b[s:s + 1, ks], 0.0))
                a_s = jnp.sum(jnp.where(later, q[:, ks] * kk[s:s + 1, ks] * decay, 0.0), axis=1, keepdims=True)
                o = o + a_s * v[s:s + 1, :]
            o_ref[n, :, h * DV:(h + 1) * DV] = _rms(o, ng_ref[...])
            sout_ref[n, h] = ebl[:, ks] * st + _dot_tn(v.astype(BF16), k_out[:, ks])


def _hgrn_sample(pa, lg, ng, s0t, layer, T, H, DK, DV, NB=4):
    B, R, wp = pa.shape
    assert B % NB == 0
    return pl.pallas_call(
        functools.partial(_hgrn_sample_kernel, NB=NB, T=T, layer=layer, H=H, DK=DK, DV=DV),
        grid=(B // NB,),
        in_specs=[pl.BlockSpec((NB, R, wp), lambda b: (b, 0, 0)),
                  pl.BlockSpec(lg.shape, lambda b: (0, 0)),
                  pl.BlockSpec((1, DV), lambda b: (0, 0)),
                  pl.BlockSpec((NB, H, DV, DK), lambda b: (b, 0, 0, 0))],
        out_specs=[pl.BlockSpec((NB, R, H * DV), lambda b: (b, 0, 0)),
                   pl.BlockSpec((NB, H, DV, DK), lambda b: (b, 0, 0, 0))],
        out_shape=[jax.ShapeDtypeStruct((B, R, H * DV), F32),
                   jax.ShapeDtypeStruct((B, H, DV, DK), F32)],
        compiler_params=_params("parallel"),
        name="hgrn_sample",
    )(pa, lg, ng.reshape(1, DV), s0t)


def _diff_prompt_kernel(slope_ref, lam_ref, q_ref, k_ref, v_ref, g_ref, o_ref, m_scr, l_scr, acc_scr,
                        *, TQ, DH, HP, out_scale):
    hg = pl.program_id(1)
    qi = pl.program_id(2)
    E = 2 * DH
    LANES = m_scr.shape[2]
    HT = TQ // 2
    slopes = [slope_ref[hg * HP + j] for j in range(HP)]
    qs = [_split_halves(q_ref[:, j * E:(j + 1) * E], DH) for j in range(HP)]

    m_scr[...] = jnp.full_like(m_scr, NEG)
    l_scr[...] = jnp.zeros_like(l_scr)
    acc_scr[...] = jnp.zeros_like(acc_scr)

    def update(row_slices, key0, nk, krel, masked):
        keys = pl.ds(pl.multiple_of(key0, nk), nk)
        key_pos = (lax.broadcasted_iota(jnp.int32, (1, nk), 1) + krel).astype(F32)
        for j in range(HP):
            cols = slice(j * E, (j + 1) * E)
            for rs in row_slices:
                nr = rs.stop - rs.start
                s = _dot_nt(qs[j][rs], k_ref[keys, cols]) + slopes[j] * key_pos
                if masked:
                    t = (rs.start + lax.broadcasted_iota(jnp.int32, (nr, nk), 0)) & (TQ - 1)
                    s = jnp.where(lax.broadcasted_iota(jnp.int32, (nr, nk), 1) + krel <= t, s, NEG)
                m_prev = m_scr[j, rs]
                m_new = jnp.maximum(m_prev, jnp.max(s, axis=1, keepdims=True))
                alpha = jnp.exp2(m_prev - m_new)
                p = jnp.exp2(s - jnp.tile(m_new, (1, nk // LANES)))
                l_scr[j, rs] = alpha * l_scr[j, rs] + jnp.sum(p, axis=1, keepdims=True)
                acc_scr[j, rs] = alpha * acc_scr[j, rs] + _dot(p.astype(BF16), v_ref[keys, cols])
                m_scr[j, rs] = m_new

    def body(kj, carry):
        update([slice(0, 2 * TQ)], kj * TQ, TQ, (kj - qi) * TQ, False)
        return carry

    lax.fori_loop(0, qi, body, 0)
    update([slice(0, 2 * TQ)], qi * TQ, HT, 0, True)
    update([slice(HT, TQ), slice(TQ + HT, 2 * TQ)], qi * TQ + HT, HT, HT, True)

    for j in range(HP):
        o2 = acc_scr[j] / l_scr[j]
        o = o2[:TQ] - lam_ref[0] * o2[TQ:]
        o_ref[:, j * E:(j + 1) * E] = (_rms(o, g_ref[...]) * out_scale).astype(o_ref.dtype)


def _diff_prompt(q, kb, vb, lam, ng, H, DH, out_scale, TQ=512, HP=4):
    B, L, W = q.shape
    E = 2 * DH
    slopes = jnp.asarray([LOG2E * 2.0 ** (-8.0 * (i + 1) / H) for i in range(H)], F32)
    assert L % TQ == 0 and TQ & (TQ - 1) == 0 and H % HP == 0
    return pl.pallas_call(
        functools.partial(_diff_prompt_kernel, TQ=TQ, DH=DH, HP=HP, out_scale=out_scale),
        grid=(B, H // HP, L // TQ),
        in_specs=[pl.BlockSpec(memory_space=pltpu.SMEM),
                  pl.BlockSpec(memory_space=pltpu.SMEM),
                  pl.BlockSpec((None, TQ, HP * E), lambda b, h, i: (b, i, h)),
                  pl.BlockSpec((None, L, HP * E), lambda b, h, i: (b, 0, h)),
                  pl.BlockSpec((None, L, HP * E), lambda b, h, i: (b, 0, h)),
                  pl.BlockSpec((1, E), lambda b, h, i: (0, 0))],
        out_specs=pl.BlockSpec((None, TQ, HP * E), lambda b, h, i: (b, i, h)),
        out_shape=jax.ShapeDtypeStruct((B, L, W), BF16),
        scratch_shapes=[pltpu.VMEM((HP, 2 * TQ, E), F32), pltpu.VMEM((HP, 2 * TQ, E), F32),
                        pltpu.VMEM((HP, 2 * TQ, E), F32)],
        compiler_params=_params("parallel", "parallel", "arbitrary"),
        name="diff_prompt",
    )(slopes, lam, q, kb, vb, ng.reshape(1, E))


def _hgrn_sample_attn_kernel(pt_ref, lam_ref, pa_ref, lg_ref, hng_ref, q_ref, kn_ref, vn_ref, dng_ref,
                             ck_ref, cv_ref, oa_ref, sout_ref, ob_ref,
                             s_scr, b_scr, kbuf, vbuf, sem, m_scr, l_scr, acc_scr,
                             *, C, NC, layer, H, DK, DV, SUB, sub_per_seq, total_sub, hsteps,
                             PP, PAGE, T, HB, DH, past, slopes, out_scale):
    fs = pl.program_id(0)
    n_fused = pl.num_programs(0)
    R = SAMPLE_ROWS
    NR = 2 * R
    E = 2 * DH
    W = H * DK
    fused_per_seq = sub_per_seq // SUB
    part = fs % fused_per_seq

    def page_copies(g, slot):
        seq = g // sub_per_seq
        first = (g % sub_per_seq) * PP
        copies = []
        for j in range(PP):
            page = pt_ref[seq, first + j]
            copies.append(pltpu.make_async_copy(ck_ref.at[page], kbuf.at[slot, j], sem.at[slot]))
            copies.append(pltpu.make_async_copy(cv_ref.at[page], vbuf.at[slot, j], sem.at[slot]))
        return copies

    @pl.when(fs == 0)
    def _():
        for g in range(min(PAGE_PREFETCH, total_sub)):
            for c in page_copies(g, g):
                c.start()
        m_scr[...] = jnp.full_like(m_scr, NEG)
        l_scr[...] = jnp.zeros_like(l_scr)
        acc_scr[...] = jnp.zeros_like(acc_scr)

    @pl.when(fs % hsteps == 0)
    def _():
        s_scr[...] = jnp.zeros_like(s_scr)

    lb = _lower_bound(lg_ref[...], layer)
    ti = lax.broadcasted_iota(jnp.int32, (C, C), 0)
    si = lax.broadcasted_iota(jnp.int32, (C, C), 1)
    lmat = (si <= ti).astype(BF16)
    xor_lower = jnp.where(ti > si, ti ^ si, 0)
    row = lax.broadcasted_iota(jnp.int32, (C, 1), 0)

    qs = [_split_halves(q_ref[:, h * E:(h + 1) * E], DH) for h in range(HB)]
    head = lax.broadcasted_iota(jnp.int32, (HB * NR, 1), 0) // NR
    slope = jnp.zeros((HB * NR, 1), F32)
    for h in range(HB):
        slope = jnp.where(head == h, slopes[h], slope)
    col = lax.broadcasted_iota(jnp.int32, (1, PP * PAGE), 1)

    def attend(k_tile, v_tile, n_tiles, kpos, ok, fresh):
        s = jnp.concatenate(
            [jnp.concatenate([_dot_nt(qs[h], k_tile(h, j)) for j in range(n_tiles)], axis=1)
             for h in range(HB)], axis=0)
        yield
        s = s + slope * kpos
        if ok is not None:
            s = jnp.where(ok, s, NEG)
        m_prev, l_prev, acc_prev = m_scr[...], l_scr[...], acc_scr[...]
        if fresh is not None:
            m_prev = jnp.where(fresh, NEG, m_prev)
            l_prev = jnp.where(fresh, 0.0, l_prev)
            acc_prev = jnp.where(fresh, 0.0, acc_prev)
        m_new = jnp.maximum(m_prev, jnp.max(s, axis=-1, keepdims=True))
        alpha = jnp.exp2(m_prev - m_new)
        p = jnp.exp2(s - m_new)
        l_scr[...] = alpha * l_prev + jnp.sum(p, axis=-1, keepdims=True)
        pb = p.astype(BF16)
        yield
        pv = []
        for h in range(HB):
            out = _dot(pb[h * NR:(h + 1) * NR, 0:PAGE], v_tile(h, 0))
            for j in range(1, n_tiles):
                out = out + _dot(pb[h * NR:(h + 1) * NR, j * PAGE:(j + 1) * PAGE], v_tile(h, j))
            pv.append(out)
        acc_scr[...] = alpha * acc_prev + jnp.concatenate(pv, axis=0)
        m_scr[...] = m_new

    def hgrn_chunk(n):
        rows = slice(n * C, (n + 1) * C)
        f = lb + (1.0 - lb) * jax.nn.sigmoid(pa_ref[rows, W:2 * W])
        g = jnp.log2(f)
        kk = 1.0 - f
        q = pa_ref[rows, 0:W]
        g_hi = g.astype(BF16)
        r1 = g - g_hi.astype(F32)
        g_mid = r1.astype(BF16)
        g_lo = (r1 - g_mid.astype(F32)).astype(BF16)
        b = _dot(lmat, g_hi) + _dot(lmat, g_mid) + _dot(lmat, g_lo)
        b_scr[rows, :] = b
        bl = b_scr[(n + 1) * C - 1:(n + 1) * C, :]
        yield
        levels = []
        m = C // 2
        while m >= 1:
            upper = (row & (2 * m - 1)) >= m
            if m >= SUBLANES:
                side, d = [], []
                for blk in range(C // (2 * m)):
                    lo = slice(blk * 2 * m, blk * 2 * m + m)
                    hi = slice(blk * 2 * m + m, (blk + 1) * 2 * m)
                    bref = b_scr[n * C + blk * 2 * m + m - 1:n * C + blk * 2 * m + m, :]
                    side += [kk[lo], q[hi]]
                    d += [bref - b[lo], b[hi] - bref]
                side = jnp.concatenate(side, axis=0)
                d = jnp.concatenate(d, axis=0)
            else:
                side = jnp.where(upper, q, kk)
                if m == 4:
                    bref = jnp.concatenate(
                        [jnp.broadcast_to(b_scr[n * C + blk * 8 + 3:n * C + blk * 8 + 4, :], (8, W))
                         for blk in range(C // 8)], axis=0)
                    d = -jnp.abs(b - bref)
                elif m == 2:
                    d = jnp.where(upper, g + jnp.where((row & 3) == 3, pltpu.roll(g, 1, 0), 0.0),
                                  jnp.where((row & 3) == 0, pltpu.roll(g, C - 1, 0), 0.0))
                else:
                    d = jnp.where(upper, g, 0.0)
            levels.append((side * jnp.exp2(d)).astype(BF16))
            m //= 2
        q_bf = q.astype(BF16)
        k_bf = kk.astype(BF16)
        q_in = (q * jnp.exp2(b)).astype(BF16)
        k_out = (kk * jnp.exp2(bl - b)).astype(BF16)
        ebl = jnp.exp2(bl)
        yield
        for h in range(H):
            ks = slice(h * DK, (h + 1) * DK)
            a = jnp.where(ti == si, _dot_nt(q_bf[:, ks], k_bf[:, ks]), 0.0)
            m = 1
            for x in reversed(levels):
                a = jnp.where(xor_lower >= m, _dot_nt(x[:, ks], x[:, ks]), a)
                m *= 2
            v = pa_ref[rows, 2 * W + h * DV:2 * W + (h + 1) * DV].astype(BF16)
            st = s_scr[h]
            o = _dot_nt(q_in[:, ks], st.astype(BF16)) + _dot(a.astype(BF16), v)
            oa_ref[rows, h * DV:(h + 1) * DV] = _rms(o, hng_ref[...]).astype(oa_ref.dtype)
            s_scr[h] = ebl[:, ks] * st + _dot_tn(v, k_out[:, ks])

    chunk_every = SUB // NC
    for j in range(SUB):
        g = fs * SUB + j
        ahead = g + PAGE_PREFETCH

        def start_ahead(ahead=ahead, slot=(j + PAGE_PREFETCH) % RING_SLOTS):
            for c in page_copies(ahead, slot):
                c.start()

        if j + PAGE_PREFETCH < SUB:
            start_ahead()
        else:
            pl.when(fs < n_fused - 1)(start_ahead)
        slot = j % RING_SLOTS
        for c in page_copies(g, slot):
            c.wait()
        k_pages = kbuf.at[slot]
        v_pages = vbuf.at[slot]
        kpos = ((part * SUB + j) * (PP * PAGE) + col - past).astype(F32)
        streams = [attend(lambda h, t, kp=k_pages: kp[t, pl.ds(h, PAGE, stride=HB), :].astype(BF16),
                          lambda h, t, vp=v_pages: vp[t, pl.ds(h, PAGE, stride=HB), :].astype(BF16),
                          PP, kpos, None, (part == 0) if j == 0 else None)]
        if j % chunk_every == 0:
            streams.insert(0, hgrn_chunk(j // chunk_every))
        _emit_interleaved(streams)

    @pl.when(part == fused_per_seq - 1)
    def _():
        tq = lax.broadcasted_iota(jnp.int32, (HB * NR, PAGE), 0) & (R - 1)
        coln = lax.broadcasted_iota(jnp.int32, (HB * NR, PAGE), 1)
        pad = jnp.zeros((PAGE - R, E), F32)
        _emit_interleaved([attend(
            lambda h, t: jnp.concatenate([kn_ref[pl.ds(h, R, stride=HB), :], pad], axis=0).astype(BF16),
            lambda h, t: jnp.concatenate([vn_ref[pl.ds(h, R, stride=HB), :], pad], axis=0).astype(BF16),
            1, coln[0:1].astype(F32), (coln <= tq) & (coln < T), None)])
        o2 = acc_scr[...] / l_scr[...]
        for h in range(HB):
            o = o2[h * NR:h * NR + R] - lam_ref[0] * o2[h * NR + R:(h + 1) * NR]
            ob_ref[:, h * E:(h + 1) * E] = _rms(o, dng_ref[...]) * out_scale

    @pl.when(fs % hsteps == hsteps - 1)
    def _():
        sout_ref[...] = s_scr[...]


def _hgrn_sample_attn(pa, lg, hng, layer, H, DK, DV, q, kn, vn, cache_k, cache_v, page_ids, lam, dng,
                      T, HB, DH, out_scale, C=128, NC=4, PP=16):
    B, L, wp = pa.shape
    W = H * DK
    RB = C * NC
    SB = q.shape[0]
    n_pages = page_ids.shape[1]
    PAGE = cache_k.shape[1] // HB
    E = 2 * DH
    R = SAMPLE_ROWS
    slopes = tuple(LOG2E * 2.0 ** (-8.0 * (i + 1) / HB) for i in range(HB))
    hsteps = L // RB
    n_fused = B * hsteps
    sub_per_seq = n_pages // PP
    total_sub = SB * sub_per_seq
    SUB = total_sub // n_fused
    assert L % RB == 0 and wp == 2 * W + H * DV and n_pages % PP == 0 and q.shape[1] == R
    assert total_sub % n_fused == 0 and sub_per_seq % SUB == 0 and SUB % NC == 0 and SUB % RING_SLOTS == 0
    fused_per_seq = sub_per_seq // SUB
    seq_rows = pl.BlockSpec((None, R, HB * E), lambda i, pt: (i // fused_per_seq, 0, 0))
    new_rows = pl.BlockSpec((None, R * HB, E), lambda i, pt: (i // fused_per_seq, 0, 0))
    page_buf = pltpu.VMEM((RING_SLOTS, PP, PAGE * HB, E), F32)
    grid_spec = pltpu.PrefetchScalarGridSpec(
        num_scalar_prefetch=1,
        grid=(n_fused,),
        in_specs=[pl.BlockSpec(memory_space=pltpu.SMEM),
                  pl.BlockSpec((None, RB, wp), lambda i, pt: (i // hsteps, i % hsteps, 0)),
                  pl.BlockSpec(lg.shape, lambda i, pt: (0, 0)),
                  pl.BlockSpec((1, DV), lambda i, pt: (0, 0)),
                  seq_rows, new_rows, new_rows,
                  pl.BlockSpec((1, E), lambda i, pt: (0, 0)),
                  pl.BlockSpec(memory_space=pl.ANY),
                  pl.BlockSpec(memory_space=pl.ANY)],
        out_specs=[pl.BlockSpec((None, RB, H * DV), lambda i, pt: (i // hsteps, i % hsteps, 0)),
                   pl.BlockSpec((None, H, DV, DK), lambda i, pt: (i // hsteps, 0, 0, 0)),
                   seq_rows],
        scratch_shapes=[pltpu.VMEM((H, DV, DK), F32), pltpu.VMEM((RB, W), F32),
                        page_buf, page_buf, pltpu.SemaphoreType.DMA((RING_SLOTS,)),
                        pltpu.VMEM((HB * 2 * R, 1), F32), pltpu.VMEM((HB * 2 * R, 1), F32),
                        pltpu.VMEM((HB * 2 * R, E), F32)])
    return pl.pallas_call(
        functools.partial(_hgrn_sample_attn_kernel, C=C, NC=NC, layer=layer, H=H, DK=DK, DV=DV,
                          SUB=SUB, sub_per_seq=sub_per_seq, total_sub=total_sub, hsteps=hsteps,
                          PP=PP, PAGE=PAGE, T=T, HB=HB, DH=DH, past=n_pages * PAGE, slopes=slopes,
                          out_scale=out_scale),
        grid_spec=grid_spec,
        out_shape=[jax.ShapeDtypeStruct((B, L, H * DV), BF16),
                   jax.ShapeDtypeStruct((B, H, DV, DK), F32),
                   jax.ShapeDtypeStruct((SB, R, HB * E), F32)],
        compiler_params=_params("arbitrary"),
        name="hgrn_prompt_diff_sample",
    )(page_ids, lam, pa, lg, hng.reshape(1, DV), q, kn, vn, dng.reshape(1, E), cache_k, cache_v)


def _mem_kv_kernel(x_ref, g_ref, wt_ref, kt_ref, vt_ref):
    h = _rms(x_ref[...], g_ref[...]).astype(BF16)
    kv = _dot_nt(wt_ref[...], h)
    w = kt_ref.shape[0]
    kt_ref[...] = kv[:w]
    vt_ref[...] = kv[w:]


def _mem_kv(mem, g, wt_bf16):
    B, N, D = mem.shape
    w = wt_bf16.shape[0] // 2
    out = jax.ShapeDtypeStruct((B, w, N), F32)
    blk = pl.BlockSpec((None, w, N), lambda b: (b, 0, 0))
    return pl.pallas_call(
        _mem_kv_kernel,
        grid=(B,),
        in_specs=[pl.BlockSpec((None, N, D), lambda b: (b, 0, 0)),
                  pl.BlockSpec((1, D), lambda b: (0, 0)),
                  pl.BlockSpec((2 * w, D), lambda b: (0, 0))],
        out_specs=[blk, blk],
        out_shape=[out, out],
        compiler_params=_params("parallel"),
        name="mem_kv",
    )(mem, g.reshape(1, D), wt_bf16)


def _mem_attend(q, kt, vt, H, DH):
    T = q.shape[0]
    q = q * (LOG2E * DH ** -0.5)
    lane_h = lax.broadcasted_iota(jnp.int32, q.shape, 1) // DH
    qs = jnp.concatenate([jnp.where(lane_h == h, q, 0.0) for h in range(H)], axis=0).astype(BF16)
    s = _dot(qs, kt.astype(BF16))
    p = jnp.exp2(s - jnp.max(s, axis=-1, keepdims=True))
    o4 = _dot_nt(p.astype(BF16), vt.astype(BF16)) / jnp.sum(p, axis=-1, keepdims=True)
    o = jnp.zeros_like(q)
    for h in range(H):
        o = jnp.where(lane_h == h, o4[h * T:(h + 1) * T], o)
    return o


def _mem_attn_kernel(q_ref, kt_ref, vt_ref, o_ref, *, NB, H, DH):
    for n in range(NB):
        o_ref[n] = _mem_attend(q_ref[n], kt_ref[n], vt_ref[n], H, DH)


def _mem_attn(q, kt, vt, H, DH, NB=8):
    B, T, W = q.shape
    N = kt.shape[2]
    assert B % NB == 0
    kv = pl.BlockSpec((NB, W, N), lambda b: (b, 0, 0))
    return pl.pallas_call(
        functools.partial(_mem_attn_kernel, NB=NB, H=H, DH=DH),
        grid=(B // NB,),
        in_specs=[pl.BlockSpec((NB, T, W), lambda b: (b, 0, 0)), kv, kv],
        out_specs=pl.BlockSpec((NB, T, W), lambda b: (b, 0, 0)),
        out_shape=jax.ShapeDtypeStruct((B, T, W), F32),
        compiler_params=_params("parallel"),
        name="mem_attn",
    )(q, kt, vt)


def _out_kernel(x_ref, oa_ref, ob_ref, m_ref, z_ref, w_ref, g_ref, *rest, final, mem_heads):
    if mem_heads is None:
        (y_ref,) = rest
        om = m_ref[...]
    else:
        kt_ref, vt_ref, y_ref = rest
        om = _mem_attend(m_ref[...], kt_ref[...], vt_ref[...], *mem_heads)
    o = jnp.concatenate([oa_ref[...].astype(F32), ob_ref[...].astype(F32), om], axis=-1)
    z = z_ref[...].astype(F32)
    o = (o * (z * jax.nn.sigmoid(z))).astype(BF16)
    y = x_ref[...] + _dot(o, w_ref[...])
    y_ref[...] = _rms(y, g_ref[...]) if final else y


def _out_proj(x2d, oa, ob, m, z, w_bf16, g, tm, final, mem=None):
    rows_total, d = x2d.shape
    mix = w_bf16.shape[0]

    def rows(a):
        return pl.BlockSpec((tm, a.shape[1]), lambda i: (i, 0))

    in_specs = [rows(x2d), rows(oa), rows(ob), rows(m), rows(z),
                pl.BlockSpec((mix, d), lambda i: (0, 0)),
                pl.BlockSpec((1, d), lambda i: (0, 0))]
    args = [x2d, oa, ob, m, z, w_bf16, g.reshape(1, d)]
    mem_heads = None
    if mem is not None:
        kt, vt, H, DH, rows_per_batch = mem
        assert rows_per_batch % tm == 0
        tiles = rows_per_batch // tm
        kv = pl.BlockSpec((None,) + kt.shape[1:], lambda i: (i // tiles, 0, 0))
        in_specs += [kv, kv]
        args += [kt, vt]
        mem_heads = (H, DH)
    return pl.pallas_call(
        functools.partial(_out_kernel, final=final, mem_heads=mem_heads),
        grid=(rows_total // tm,),
        in_specs=in_specs,
        out_specs=rows(x2d),
        out_shape=jax.ShapeDtypeStruct((rows_total, d), F32),
        compiler_params=_params("parallel"),
        name="out_proj",
    )(*args)


def kernel(x_prompt, x_sample, mem_prompt, cache_k, cache_v, state_hgrn, cache_mem_k, cache_mem_v, page_table, norm_g, w_in, hgrn_lb_logits, hgrn_norm_g, diff_norm_g, lambda_q1, lambda_k1, lambda_q2, lambda_k2, mem_norm_g, w_mem_kv, w_out, final_g):
    B, L, D = x_prompt.shape
    SB, T, _ = x_sample.shape
    depth, _, H_A, DK_A, DV_A = state_hgrn.shape
    _, n_pool, PAGE, H_B, E_B = cache_k.shape
    DH_B = E_B // 2
    _, _, N_MEM, H_M, DH_M = cache_mem_k.shape
    W_A, W_K, W_B, W_M = H_A * DV_A, H_A * DK_A, H_B * E_B, H_M * DH_M
    MIX = W_A + W_B + W_M
    R = SAMPLE_ROWS
    TM = ROW_TILE
    assert T <= R and L % (2 * TM) == 0
    sizes = (2 * W_K + W_A, W_B, W_M, MIX)
    diff_scale = LOG2E * DH_B ** -0.5

    cache_k3 = cache_k.reshape(depth * n_pool, PAGE * H_B, E_B)
    cache_v3 = cache_v.reshape(depth * n_pool, PAGE * H_B, E_B)

    yp = x_prompt.reshape(B * L, D)
    ys = jnp.pad(x_sample, ((0, 0), (0, R - T), (0, 0))).reshape(SB * R, D)
    outs = {k: [] for k in ("kp", "vp", "sp", "mkp", "mvp", "ks", "vs", "ss")}
    for l in range(depth):
        lam_init = 0.8 - 0.6 * math.exp(-0.3 * l)
        lam = (jnp.exp(jnp.sum(lambda_q1[l] * lambda_k1[l])) - jnp.exp(jnp.sum(lambda_q2[l] * lambda_k2[l]))
               + lam_init).reshape(1).astype(F32)
        w_in_b = w_in[l].astype(BF16)
        w_out_b = w_out[l].astype(BF16)
        w_mem_t = w_mem_kv[l].T.astype(BF16)

        def channel_major(a):
            return jnp.transpose(a, (0, 2, 3, 1)).reshape(a.shape[0], W_M, N_MEM)

        def token_major(a):
            return jnp.transpose(a.reshape(a.shape[0], H_M, DH_M, N_MEM), (0, 3, 1, 2))

        mkt, mvt = _mem_kv(mem_prompt, mem_norm_g[l], w_mem_t)
        pa, dq, kb, vb, ko, vo, mq, z = _in_proj(yp, norm_g[l], w_in_b, sizes, H_B, TM, diff_scale, True,
                                                 "in_proj_prompt")
        pa_s, dq_s, ko_s, vo_s, mq_s, z_s = _in_proj(ys, norm_g[l], w_in_b, sizes, H_B, SB * R, diff_scale, False,
                                                     "in_proj_sample")
        oa, st, ob_s = _hgrn_sample_attn(
            pa.reshape(B, L, -1), hgrn_lb_logits, hgrn_norm_g[l], l, H_A, DK_A, DV_A,
            dq_s.reshape(SB, R, W_B), ko_s.reshape(SB, R * H_B, E_B), vo_s.reshape(SB, R * H_B, E_B),
            cache_k3, cache_v3, page_table + l * n_pool, lam, diff_norm_g[l], T, H_B, DH_B, 1.0 - lam_init)

        ob = _diff_prompt(dq.reshape(B, L, W_B), kb.reshape(B, L, W_B), vb.reshape(B, L, W_B),
                          lam, diff_norm_g[l], H_B, DH_B, 1.0 - lam_init)
        yp = _out_proj(yp, oa.reshape(B * L, W_A), ob.reshape(B * L, W_B), mq, z,
                       w_out_b, final_g, 2 * TM, l == depth - 1, mem=(mkt, mvt, H_M, DH_M, L))
        outs["kp"].append(ko.reshape(B, L, H_B, E_B))
        outs["vp"].append(vo.reshape(B, L, H_B, E_B))
        outs["sp"].append(jnp.swapaxes(st, -1, -2))
        outs["mkp"].append(token_major(mkt))
        outs["mvp"].append(token_major(mvt))

        oa_s, sst = _hgrn_sample(pa_s.reshape(SB, R, -1), hgrn_lb_logits, hgrn_norm_g[l],
                                 jnp.swapaxes(state_hgrn[l], -1, -2), l, T, H_A, DK_A, DV_A)
        om_s = _mem_attn(mq_s.reshape(SB, R, W_M), channel_major(cache_mem_k[l]), channel_major(cache_mem_v[l]),
                         H_M, DH_M)
        ys = _out_proj(ys, oa_s.reshape(SB * R, W_A), ob_s.reshape(SB * R, W_B), om_s.reshape(SB * R, W_M), z_s,
                       w_out_b, final_g, SB * R, l == depth - 1)
        outs["ks"].append(ko_s.reshape(SB, R, H_B, E_B)[:, :T])
        outs["vs"].append(vo_s.reshape(SB, R, H_B, E_B)[:, :T])
        outs["ss"].append(jnp.swapaxes(sst, -1, -2))

    y_prompt = yp.reshape(B, L, D)
    y_sample = ys.reshape(SB, R, D)[:, :T]
    return (y_prompt, y_sample, jnp.stack(outs["kp"]), jnp.stack(outs["vp"]), jnp.stack(outs["sp"]),
            jnp.stack(outs["mkp"]), jnp.stack(outs["mvp"]), jnp.stack(outs["ks"]), jnp.stack(outs["vs"]),
            jnp.stack(outs["ss"]))
```

```python
import functools
import math

import jax
import jax.numpy as jnp
from jax import lax
from jax.experimental import pallas as pl
from jax.experimental.pallas import tpu as pltpu

F32 = jnp.float32
BF16 = jnp.bfloat16
EPS = 1e-6
NEG = -1e30
LOG2E = math.log2(math.e)
VMEM_LIMIT = 48 * 1024 * 1024
SUBLANES = 8
SAMPLE_ROWS = SUBLANES
ROW_TILE = 512
RING_SLOTS = 4
PAGE_PREFETCH = 2

_NT = (((1,), (1,)), ((), ()))
_TN = (((0,), (0,)), ((), ()))


def _dot(a, b):
    return jnp.dot(a, b, preferred_element_type=F32)


def _dot_nt(a, b):
    return lax.dot_general(a, b, _NT, preferred_element_type=F32)


def _dot_tn(a, b):
    return lax.dot_general(a, b, _TN, preferred_element_type=F32)


def _rms(x, g):
    return x * lax.rsqrt(jnp.mean(x * x, axis=-1, keepdims=True) + EPS) * g


def _params(*sem):
    return pltpu.CompilerParams(dimension_semantics=sem, vmem_limit_bytes=VMEM_LIMIT)


def _emit_interleaved(streams):
    streams = list(streams)
    while streams:
        for gen in list(streams):
            try:
                next(gen)
            except StopIteration:
                streams.remove(gen)


def _split_halves(q, dh):
    lane = lax.broadcasted_iota(jnp.int32, q.shape, 1)
    zero = jnp.zeros_like(q)
    return jnp.concatenate([jnp.where(lane < dh, q, zero), jnp.where(lane < dh, zero, q)], axis=0)


def _in_proj_kernel(x_ref, g_ref, w_ref, *rest, sizes, H, prompt, scale):
    w_pa, w_b, w_m, mix = sizes
    if prompt:
        pa_ref, dq_ref, kb_ref, vb_ref, ko_ref, vo_ref, mq_ref, z_ref = rest
    else:
        pa_ref, dq_ref, ko_ref, vo_ref, mq_ref, z_ref = rest
    e = w_b // H
    h = _rms(x_ref[...], g_ref[...]).astype(BF16)
    c = 0
    pa_ref[...] = _dot(h, w_ref[:, c:c + w_pa])
    c += w_pa
    dq_ref[...] = (_dot(h, w_ref[:, c:c + w_b]) * scale).astype(BF16)
    c += w_b
    k = _dot(h, w_ref[:, c:c + w_b])
    c += w_b
    v = _dot(h, w_ref[:, c:c + w_b])
    c += w_b
    tm = k.shape[0]
    for hh in range(H):
        ko_ref[pl.ds(hh, tm, stride=H), :] = k[:, hh * e:(hh + 1) * e]
        vo_ref[pl.ds(hh, tm, stride=H), :] = v[:, hh * e:(hh + 1) * e]
    if prompt:
        kb_ref[...] = k.astype(BF16)
        vb_ref[...] = v.astype(BF16)
    mq_ref[...] = _dot(h, w_ref[:, c:c + w_m])
    c += w_m
    z_ref[...] = _dot(h, w_ref[:, c:c + mix]).astype(z_ref.dtype)


def _in_proj(x2d, g, w_bf16, sizes, H, tm, scale, prompt, name):
    m, d = x2d.shape
    n = w_bf16.shape[1]
    w_pa, w_b, w_m, mix = sizes
    e = w_b // H
    assert w_pa + 3 * w_b + w_m + mix == n and m % tm == 0

    def rows(width):
        return pl.BlockSpec((tm, width), lambda i: (i, 0))

    const = lambda i: (0, 0)
    in_specs = [rows(d), pl.BlockSpec((1, d), const), pl.BlockSpec((d, n), const)]
    args = [x2d, g.reshape(1, d), w_bf16]
    out_specs = [rows(w_pa), rows(w_b)]
    out_shape = [jax.ShapeDtypeStruct((m, w_pa), F32), jax.ShapeDtypeStruct((m, w_b), BF16)]
    if prompt:
        out_specs += [rows(w_b), rows(w_b)]
        out_shape += [jax.ShapeDtypeStruct((m, w_b), BF16), jax.ShapeDtypeStruct((m, w_b), BF16)]
    head_rows = pl.BlockSpec((tm * H, e), lambda i: (i, 0))
    out_specs += [head_rows, head_rows, rows(w_m), rows(mix)]
    out_shape += [jax.ShapeDtypeStruct((m * H, e), F32), jax.ShapeDtypeStruct((m * H, e), F32),
                  jax.ShapeDtypeStruct((m, w_m), F32), jax.ShapeDtypeStruct((m, mix), BF16)]
    return pl.pallas_call(
        functools.partial(_in_proj_kernel, sizes=sizes, H=H, prompt=prompt, scale=scale),
        grid=(m // tm,),
        in_specs=in_specs,
        out_specs=out_specs,
        out_shape=out_shape,
        compiler_params=_params("parallel"),
        name=name,
    )(*args)


def _lower_bound(lg, layer, axis=0):
    e = jnp.exp(lg - jnp.max(lg, axis=axis, keepdims=True))
    p = e / jnp.sum(e, axis=axis, keepdims=True)
    head = p[:layer + 1] if axis == 0 else p[:, :layer + 1]
    return jnp.sum(head, axis=axis, keepdims=True)


def _hgrn_sample_kernel(pa_ref, lg_ref, ng_ref, s0_ref, o_ref, sout_ref, *, NB, T, layer, H, DK, DV):
    W = H * DK
    R = pa_ref.shape[1]
    lb = _lower_bound(lg_ref[...], layer)
    row = lax.broadcasted_iota(jnp.int32, (R, 1), 0)
    valid = row < T
    for n in range(NB):
        f = lb + (1.0 - lb) * jax.nn.sigmoid(pa_ref[n, :, W:2 * W])
        g = jnp.where(valid, jnp.log2(f), 0.0)
        kk = jnp.where(valid, 1.0 - f, 0.0)
        q = pa_ref[n, :, 0:W]
        b = g
        sh = 1
        while sh < R:
            b = b + jnp.where(row >= sh, pltpu.roll(b, sh, 0), 0.0)
            sh *= 2
        bl = b[R - 1:R]
        q_in = (q * jnp.exp2(b)).astype(BF16)
        k_out = (kk * jnp.exp2(bl - b)).astype(BF16)
        ebl = jnp.exp2(bl)
        for h in range(H):
            ks = slice(h * DK, (h + 1) * DK)
            v = pa_ref[n, :, 2 * W + h * DV:2 * W + (h + 1) * DV]
            st = s0_ref[n, h]
            o = _dot_nt(q_in[:, ks], st.astype(BF16))
            for s in range(T):
                later = row >= s
                decay = jnp.exp2(jnp.where(later, b[:, ks] - b[s:s + 1, ks], 0.0))
                a_s = jnp.sum(jnp.where(later, q[:, ks] * kk[s:s + 1, ks] * decay, 0.0), axis=1, keepdims=True)
                o = o + a_s * v[s:s + 1, :]
            o_ref[n, :, h * DV:(h + 1) * DV] = _rms(o, ng_ref[...])
            sout_ref[n, h] = ebl[:, ks] * st + _dot_tn(v.astype(BF16), k_out[:, ks])


def _hgrn_sample(pa, lg, ng, s0t, layer, T, H, DK, DV, NB=4):
    B, R, wp = pa.shape
    assert B % NB == 0
    return pl.pallas_call(
        functools.partial(_hgrn_sample_kernel, NB=NB, T=T, layer=layer, H=H, DK=DK, DV=DV),
        grid=(B // NB,),
        in_specs=[pl.BlockSpec((NB, R, wp), lambda b: (b, 0, 0)),
                  pl.BlockSpec(lg.shape, lambda b: (0, 0)),
                  pl.BlockSpec((1, DV), lambda b: (0, 0)),
                  pl.BlockSpec((NB, H, DV, DK), lambda b: (b, 0, 0, 0))],
        out_specs=[pl.BlockSpec((NB, R, H * DV), lambda b: (b, 0, 0)),
                   pl.BlockSpec((NB, H, DV, DK), lambda b: (b, 0, 0, 0))],
        out_shape=[jax.ShapeDtypeStruct((B, R, H * DV), F32),
                   jax.ShapeDtypeStruct((B, H, DV, DK), F32)],
        compiler_params=_params("parallel"),
        name="hgrn_sample",
    )(pa, lg, ng.reshape(1, DV), s0t)


def _diff_prompt_kernel(slope_ref, lam_ref, q_ref, k_ref, v_ref, g_ref, o_ref, m_scr, l_scr, acc_scr,
                        *, TQ, DH, HP, out_scale):
    hg = pl.program_id(1)
    qi = pl.program_id(2)
    E = 2 * DH
    LANES = m_scr.shape[2]
    HT = TQ // 2
    slopes = [slope_ref[hg * HP + j] for j in range(HP)]
    qs = [_split_halves(q_ref[:, j * E:(j + 1) * E], DH) for j in range(HP)]

    m_scr[...] = jnp.full_like(m_scr, NEG)
    l_scr[...] = jnp.zeros_like(l_scr)
    acc_scr[...] = jnp.zeros_like(acc_scr)

    def update(row_slices, key0, nk, krel, masked):
        keys = pl.ds(pl.multiple_of(key0, nk), nk)
        key_pos = (lax.broadcasted_iota(jnp.int32, (1, nk), 1) + krel).astype(F32)
        for j in range(HP):
            cols = slice(j * E, (j + 1) * E)
            for rs in row_slices:
                nr = rs.stop - rs.start
                s = _dot_nt(qs[j][rs], k_ref[keys, cols]) + slopes[j] * key_pos
                if masked:
                    t = (rs.start + lax.broadcasted_iota(jnp.int32, (nr, nk), 0)) & (TQ - 1)
                    s = jnp.where(lax.broadcasted_iota(jnp.int32, (nr, nk), 1) + krel <= t, s, NEG)
                m_prev = m_scr[j, rs]
                m_new = jnp.maximum(m_prev, jnp.max(s, axis=1, keepdims=True))
                alpha = jnp.exp2(m_prev - m_new)
                p = jnp.exp2(s - jnp.tile(m_new, (1, nk // LANES)))
                l_scr[j, rs] = alpha * l_scr[j, rs] + jnp.sum(p, axis=1, keepdims=True)
                acc_scr[j, rs] = alpha * acc_scr[j, rs] + _dot(p.astype(BF16), v_ref[keys, cols])
                m_scr[j, rs] = m_new

    def body(kj, carry):
        update([slice(0, 2 * TQ)], kj * TQ, TQ, (kj - qi) * TQ, False)
        return carry

    lax.fori_loop(0, qi, body, 0)
    update([slice(0, 2 * TQ)], qi * TQ, HT, 0, True)
    update([slice(HT, TQ), slice(TQ + HT, 2 * TQ)], qi * TQ + HT, HT, HT, True)

    for j in range(HP):
        o2 = acc_scr[j] / l_scr[j]
        o = o2[:TQ] - lam_ref[0] * o2[TQ:]
        o_ref[:, j * E:(j + 1) * E] = (_rms(o, g_ref[...]) * out_scale).astype(o_ref.dtype)


def _diff_prompt(q, kb, vb, lam, ng, H, DH, out_scale, TQ=512, HP=4):
    B, L, W = q.shape
    E = 2 * DH
    slopes = jnp.asarray([LOG2E * 2.0 ** (-8.0 * (i + 1) / H) for i in range(H)], F32)
    assert L % TQ == 0 and TQ & (TQ - 1) == 0 and H % HP == 0
    return pl.pallas_call(
        functools.partial(_diff_prompt_kernel, TQ=TQ, DH=DH, HP=HP, out_scale=out_scale),
        grid=(B, H // HP, L // TQ),
        in_specs=[pl.BlockSpec(memory_space=pltpu.SMEM),
                  pl.BlockSpec(memory_space=pltpu.SMEM),
                  pl.BlockSpec((None, TQ, HP * E), lambda b, h, i: (b, i, h)),
                  pl.BlockSpec((None, L, HP * E), lambda b, h, i: (b, 0, h)),
                  pl.BlockSpec((None, L, HP * E), lambda b, h, i: (b, 0, h)),
                  pl.BlockSpec((1, E), lambda b, h, i: (0, 0))],
        out_specs=pl.BlockSpec((None, TQ, HP * E), lambda b, h, i: (b, i, h)),
        out_shape=jax.ShapeDtypeStruct((B, L, W), BF16),
        scratch_shapes=[pltpu.VMEM((HP, 2 * TQ, E), F32), pltpu.VMEM((HP, 2 * TQ, E), F32),
                        pltpu.VMEM((HP, 2 * TQ, E), F32)],
        compiler_params=_params("parallel", "parallel", "arbitrary"),
        name="diff_prompt",
    )(slopes, lam, q, kb, vb, ng.reshape(1, E))


def _hgrn_sample_attn_kernel(pt_ref, lam_ref, pa_ref, lg_ref, hng_ref, q_ref, kn_ref, vn_ref, dng_ref,
                             ck_ref, cv_ref, oa_ref, sout_ref, ob_ref,
                             s_scr, b_scr, kbuf, vbuf, sem, m_scr, l_scr, acc_scr,
                             *, C, NC, layer, H, DK, DV, SUB, sub_per_seq, total_sub, hsteps,
                             PP, PAGE, T, HB, DH, past, slopes, out_scale):
    fs = pl.program_id(0)
    n_fused = pl.num_programs(0)
    R = SAMPLE_ROWS
    NR = 2 * R
    E = 2 * DH
    W = H * DK
    fused_per_seq = sub_per_seq // SUB
    part = fs % fused_per_seq

    def page_copies(g, slot):
        seq = g // sub_per_seq
        first = (g % sub_per_seq) * PP
        copies = []
        for j in range(PP):
            page = pt_ref[seq, first + j]
            copies.append(pltpu.make_async_copy(ck_ref.at[page], kbuf.at[slot, j], sem.at[slot]))
            copies.append(pltpu.make_async_copy(cv_ref.at[page], vbuf.at[slot, j], sem.at[slot]))
        return copies

    @pl.when(fs == 0)
    def _():
        for g in range(min(PAGE_PREFETCH, total_sub)):
            for c in page_copies(g, g):
                c.start()
        m_scr[...] = jnp.full_like(m_scr, NEG)
        l_scr[...] = jnp.zeros_like(l_scr)
        acc_scr[...] = jnp.zeros_like(acc_scr)

    @pl.when(fs % hsteps == 0)
    def _():
        s_scr[...] = jnp.zeros_like(s_scr)

    lb = _lower_bound(lg_ref[...], layer)
    ti = lax.broadcasted_iota(jnp.int32, (C, C), 0)
    si = lax.broadcasted_iota(jnp.int32, (C, C), 1)
    lmat = (si <= ti).astype(BF16)
    xor_lower = jnp.where(ti > si, ti ^ si, 0)
    row = lax.broadcasted_iota(jnp.int32, (C, 1), 0)

    qs = [_split_halves(q_ref[:, h * E:(h + 1) * E], DH) for h in range(HB)]
    head = lax.broadcasted_iota(jnp.int32, (HB * NR, 1), 0) // NR
    slope = jnp.zeros((HB * NR, 1), F32)
    for h in range(HB):
        slope = jnp.where(head == h, slopes[h], slope)
    col = lax.broadcasted_iota(jnp.int32, (1, PP * PAGE), 1)

    def attend(k_tile, v_tile, n_tiles, kpos, ok, fresh):
        s = jnp.concatenate(
            [_dot_nt(qs[h], jnp.concatenate([k_tile(h, j) for j in range(n_tiles)], axis=0))
             for h in range(HB)], axis=0)
        yield
        s = s + slope * kpos
        if ok is not None:
            s = jnp.where(ok, s, NEG)
        m_prev, l_prev, acc_prev = m_scr[...], l_scr[...], acc_scr[...]
        if fresh is not None:
            m_prev = jnp.where(fresh, NEG, m_prev)
            l_prev = jnp.where(fresh, 0.0, l_prev)
            acc_prev = jnp.where(fresh, 0.0, acc_prev)
        m_new = jnp.maximum(m_prev, jnp.max(s, axis=-1, keepdims=True))
        alpha = jnp.exp2(m_prev - m_new)
        p = jnp.exp2(s - m_new)
        l_scr[...] = alpha * l_prev + jnp.sum(p, axis=-1, keepdims=True)
        pb = p.astype(BF16)
        yield
        pv = [_dot(pb[h * NR:(h + 1) * NR], jnp.concatenate([v_tile(h, j) for j in range(n_tiles)], axis=0))
              for h in range(HB)]
        acc_scr[...] = alpha * acc_prev + jnp.concatenate(pv, axis=0)
        m_scr[...] = m_new

    def hgrn_chunk(n):
        rows = slice(n * C, (n + 1) * C)
        f = lb + (1.0 - lb) * jax.nn.sigmoid(pa_ref[rows, W:2 * W])
        g = jnp.log2(f)
        kk = 1.0 - f
        q = pa_ref[rows, 0:W]
        g_hi = g.astype(BF16)
        r1 = g - g_hi.astype(F32)
        g_mid = r1.astype(BF16)
        g_lo = (r1 - g_mid.astype(F32)).astype(BF16)
        b = _dot(lmat, g_hi) + _dot(lmat, g_mid) + _dot(lmat, g_lo)
        b_scr[rows, :] = b
        bl = b_scr[(n + 1) * C - 1:(n + 1) * C, :]
        yield
        levels = []
        m = C // 2
        while m >= 1:
            upper = (row & (2 * m - 1)) >= m
            if m >= SUBLANES:
                side, d = [], []
                for blk in range(C // (2 * m)):
                    lo = slice(blk * 2 * m, blk * 2 * m + m)
                    hi = slice(blk * 2 * m + m, (blk + 1) * 2 * m)
                    bref = b_scr[n * C + blk * 2 * m + m - 1:n * C + blk * 2 * m + m, :]
                    side += [kk[lo], q[hi]]
                    d += [bref - b[lo], b[hi] - bref]
                side = jnp.concatenate(side, axis=0)
                d = jnp.concatenate(d, axis=0)
            else:
                side = jnp.where(upper, q, kk)
                if m == 4:
                    bref = jnp.concatenate(
                        [jnp.broadcast_to(b_scr[n * C + blk * 8 + 3:n * C + blk * 8 + 4, :], (8, W))
                         for blk in range(C // 8)], axis=0)
                    d = -jnp.abs(b - bref)
                elif m == 2:
                    d = jnp.where(upper, g + jnp.where((row & 3) == 3, pltpu.roll(g, 1, 0), 0.0),
                                  jnp.where((row & 3) == 0, pltpu.roll(g, C - 1, 0), 0.0))
                else:
                    d = jnp.where(upper, g, 0.0)
            levels.append((side * jnp.exp2(d)).astype(BF16))
            m //= 2
        q_bf = q.astype(BF16)
        k_bf = kk.astype(BF16)
        q_in = (q * jnp.exp2(b)).astype(BF16)
        k_out = (kk * jnp.exp2(bl - b)).astype(BF16)
        ebl = jnp.exp2(bl)
        yield
        for h in range(H):
            ks = slice(h * DK, (h + 1) * DK)
            a = jnp.where(ti == si, _dot_nt(q_bf[:, ks], k_bf[:, ks]), 0.0)
            m = 1
            for x in reversed(levels):
                a = jnp.where(xor_lower >= m, _dot_nt(x[:, ks], x[:, ks]), a)
                m *= 2
            v = pa_ref[rows, 2 * W + h * DV:2 * W + (h + 1) * DV].astype(BF16)
            st = s_scr[h]
            o = _dot_nt(q_in[:, ks], st.astype(BF16)) + _dot(a.astype(BF16), v)
            oa_ref[rows, h * DV:(h + 1) * DV] = _rms(o, hng_ref[...]).astype(oa_ref.dtype)
            s_scr[h] = ebl[:, ks] * st + _dot_tn(v, k_out[:, ks])

    chunk_every = SUB // NC
    for j in range(SUB):
        g = fs * SUB + j
        ahead = g + PAGE_PREFETCH

        def start_ahead(ahead=ahead, slot=(j + PAGE_PREFETCH) % RING_SLOTS):
            for c in page_copies(ahead, slot):
                c.start()

        if j + PAGE_PREFETCH < SUB:
            start_ahead()
        else:
            pl.when(fs < n_fused - 1)(start_ahead)
        slot = j % RING_SLOTS
        for c in page_copies(g, slot):
            c.wait()
        k_pages = kbuf.at[slot]
        v_pages = vbuf.at[slot]
        kpos = ((part * SUB + j) * (PP * PAGE) + col - past).astype(F32)
        streams = [attend(lambda h, t, kp=k_pages: kp[t, pl.ds(h, PAGE, stride=HB), :].astype(BF16),
                          lambda h, t, vp=v_pages: vp[t, pl.ds(h, PAGE, stride=HB), :].astype(BF16),
                          PP, kpos, None, (part == 0) if j == 0 else None)]
        if j % chunk_every == 0:
            streams.insert(0, hgrn_chunk(j // chunk_every))
        _emit_interleaved(streams)

    @pl.when(part == fused_per_seq - 1)
    def _():
        tq = lax.broadcasted_iota(jnp.int32, (HB * NR, PAGE), 0) & (R - 1)
        coln = lax.broadcasted_iota(jnp.int32, (HB * NR, PAGE), 1)
        pad = jnp.zeros((PAGE - R, E), F32)
        _emit_interleaved([attend(
            lambda h, t: jnp.concatenate([kn_ref[pl.ds(h, R, stride=HB), :], pad], axis=0).astype(BF16),
            lambda h, t: jnp.concatenate([vn_ref[pl.ds(h, R, stride=HB), :], pad], axis=0).astype(BF16),
            1, coln[0:1].astype(F32), (coln <= tq) & (coln < T), None)])
        o2 = acc_scr[...] / l_scr[...]
        for h in range(HB):
            o = o2[h * NR:h * NR + R] - lam_ref[0] * o2[h * NR + R:(h + 1) * NR]
            ob_ref[:, h * E:(h + 1) * E] = _rms(o, dng_ref[...]) * out_scale

    @pl.when(fs % hsteps == hsteps - 1)
    def _():
        sout_ref[...] = s_scr[...]


def _hgrn_sample_attn(pa, lg, hng, layer, H, DK, DV, q, kn, vn, cache_k, cache_v, page_ids, lam, dng,
                      T, HB, DH, out_scale, C=128, NC=4, PP=16):
    B, L, wp = pa.shape
    W = H * DK
    RB = C * NC
    SB = q.shape[0]
    n_pages = page_ids.shape[1]
    PAGE = cache_k.shape[1] // HB
    E = 2 * DH
    R = SAMPLE_ROWS
    slopes = tuple(LOG2E * 2.0 ** (-8.0 * (i + 1) / HB) for i in range(HB))
    hsteps = L // RB
    n_fused = B * hsteps
    sub_per_seq = n_pages // PP
    total_sub = SB * sub_per_seq
    SUB = total_sub // n_fused
    assert L % RB == 0 and wp == 2 * W + H * DV and n_pages % PP == 0 and q.shape[1] == R
    assert total_sub % n_fused == 0 and sub_per_seq % SUB == 0 and SUB % NC == 0 and SUB % RING_SLOTS == 0
    fused_per_seq = sub_per_seq // SUB
    seq_rows = pl.BlockSpec((None, R, HB * E), lambda i, pt: (i // fused_per_seq, 0, 0))
    new_rows = pl.BlockSpec((None, R * HB, E), lambda i, pt: (i // fused_per_seq, 0, 0))
    page_buf = pltpu.VMEM((RING_SLOTS, PP, PAGE * HB, E), F32)
    grid_spec = pltpu.PrefetchScalarGridSpec(
        num_scalar_prefetch=1,
        grid=(n_fused,),
        in_specs=[pl.BlockSpec(memory_space=pltpu.SMEM),
                  pl.BlockSpec((None, RB, wp), lambda i, pt: (i // hsteps, i % hsteps, 0)),
                  pl.BlockSpec(lg.shape, lambda i, pt: (0, 0)),
                  pl.BlockSpec((1, DV), lambda i, pt: (0, 0)),
                  seq_rows, new_rows, new_rows,
                  pl.BlockSpec((1, E), lambda i, pt: (0, 0)),
                  pl.BlockSpec(memory_space=pl.ANY),
                  pl.BlockSpec(memory_space=pl.ANY)],
        out_specs=[pl.BlockSpec((None, RB, H * DV), lambda i, pt: (i // hsteps, i % hsteps, 0)),
                   pl.BlockSpec((None, H, DV, DK), lambda i, pt: (i // hsteps, 0, 0, 0)),
                   seq_rows],
        scratch_shapes=[pltpu.VMEM((H, DV, DK), F32), pltpu.VMEM((RB, W), F32),
                        page_buf, page_buf, pltpu.SemaphoreType.DMA((RING_SLOTS,)),
                        pltpu.VMEM((HB * 2 * R, 1), F32), pltpu.VMEM((HB * 2 * R, 1), F32),
                        pltpu.VMEM((HB * 2 * R, E), F32)])
    return pl.pallas_call(
        functools.partial(_hgrn_sample_attn_kernel, C=C, NC=NC, layer=layer, H=H, DK=DK, DV=DV,
                          SUB=SUB, sub_per_seq=sub_per_seq, total_sub=total_sub, hsteps=hsteps,
                          PP=PP, PAGE=PAGE, T=T, HB=HB, DH=DH, past=n_pages * PAGE, slopes=slopes,
                          out_scale=out_scale),
        grid_spec=grid_spec,
        out_shape=[jax.ShapeDtypeStruct((B, L, H * DV), BF16),
                   jax.ShapeDtypeStruct((B, H, DV, DK), F32),
                   jax.ShapeDtypeStruct((SB, R, HB * E), F32)],
        compiler_params=_params("arbitrary"),
        name="hgrn_prompt_diff_sample",
    )(page_ids, lam, pa, lg, hng.reshape(1, DV), q, kn, vn, dng.reshape(1, E), cache_k, cache_v)


def _mem_kv_kernel(x_ref, g_ref, wt_ref, kt_ref, vt_ref):
    h = _rms(x_ref[...], g_ref[...]).astype(BF16)
    kv = _dot_nt(wt_ref[...], h)
    w = kt_ref.shape[0]
    kt_ref[...] = kv[:w]
    vt_ref[...] = kv[w:]


def _mem_kv(mem, g, wt_bf16):
    B, N, D = mem.shape
    w = wt_bf16.shape[0] // 2
    out = jax.ShapeDtypeStruct((B, w, N), F32)
    blk = pl.BlockSpec((None, w, N), lambda b: (b, 0, 0))
    return pl.pallas_call(
        _mem_kv_kernel,
        grid=(B,),
        in_specs=[pl.BlockSpec((None, N, D), lambda b: (b, 0, 0)),
                  pl.BlockSpec((1, D), lambda b: (0, 0)),
                  pl.BlockSpec((2 * w, D), lambda b: (0, 0))],
        out_specs=[blk, blk],
        out_shape=[out, out],
        compiler_params=_params("parallel"),
        name="mem_kv",
    )(mem, g.reshape(1, D), wt_bf16)


def _mem_attend(q, kt, vt, H, DH):
    T = q.shape[0]
    q = q * (LOG2E * DH ** -0.5)
    lane_h = lax.broadcasted_iota(jnp.int32, q.shape, 1) // DH
    qs = jnp.concatenate([jnp.where(lane_h == h, q, 0.0) for h in range(H)], axis=0).astype(BF16)
    s = _dot(qs, kt.astype(BF16))
    p = jnp.exp2(s - jnp.max(s, axis=-1, keepdims=True))
    o4 = _dot_nt(p.astype(BF16), vt.astype(BF16)) / jnp.sum(p, axis=-1, keepdims=True)
    o = jnp.zeros_like(q)
    for h in range(H):
        o = jnp.where(lane_h == h, o4[h * T:(h + 1) * T], o)
    return o


def _mem_attn_kernel(q_ref, kt_ref, vt_ref, o_ref, *, NB, H, DH):
    for n in range(NB):
        o_ref[n] = _mem_attend(q_ref[n], kt_ref[n], vt_ref[n], H, DH)


def _mem_attn(q, kt, vt, H, DH, NB=8):
    B, T, W = q.shape
    N = kt.shape[2]
    assert B % NB == 0
    kv = pl.BlockSpec((NB, W, N), lambda b: (b, 0, 0))
    return pl.pallas_call(
        functools.partial(_mem_attn_kernel, NB=NB, H=H, DH=DH),
        grid=(B // NB,),
        in_specs=[pl.BlockSpec((NB, T, W), lambda b: (b, 0, 0)), kv, kv],
        out_specs=pl.BlockSpec((NB, T, W), lambda b: (b, 0, 0)),
        out_shape=jax.ShapeDtypeStruct((B, T, W), F32),
        compiler_params=_params("parallel"),
        name="mem_attn",
    )(q, kt, vt)


def _out_kernel(x_ref, oa_ref, ob_ref, m_ref, z_ref, w_ref, g_ref, *rest, final, mem_heads):
    if mem_heads is None:
        (y_ref,) = rest
        om = m_ref[...]
    else:
        kt_ref, vt_ref, y_ref = rest
        om = _mem_attend(m_ref[...], kt_ref[...], vt_ref[...], *mem_heads)
    o = jnp.concatenate([oa_ref[...].astype(F32), ob_ref[...].astype(F32), om], axis=-1)
    z = z_ref[...].astype(F32)
    o = (o * (z * jax.nn.sigmoid(z))).astype(BF16)
    y = x_ref[...] + _dot(o, w_ref[...])
    y_ref[...] = _rms(y, g_ref[...]) if final else y


def _out_proj(x2d, oa, ob, m, z, w_bf16, g, tm, final, mem=None):
    rows_total, d = x2d.shape
    mix = w_bf16.shape[0]

    def rows(a):
        return pl.BlockSpec((tm, a.shape[1]), lambda i: (i, 0))

    in_specs = [rows(x2d), rows(oa), rows(ob), rows(m), rows(z),
                pl.BlockSpec((mix, d), lambda i: (0, 0)),
                pl.BlockSpec((1, d), lambda i: (0, 0))]
    args = [x2d, oa, ob, m, z, w_bf16, g.reshape(1, d)]
    mem_heads = None
    if mem is not None:
        kt, vt, H, DH, rows_per_batch = mem
        assert rows_per_batch % tm == 0
        tiles = rows_per_batch // tm
        kv = pl.BlockSpec((None,) + kt.shape[1:], lambda i: (i // tiles, 0, 0))
        in_specs += [kv, kv]
        args += [kt, vt]
        mem_heads = (H, DH)
    return pl.pallas_call(
        functools.partial(_out_kernel, final=final, mem_heads=mem_heads),
        grid=(rows_total // tm,),
        in_specs=in_specs,
        out_specs=rows(x2d),
        out_shape=jax.ShapeDtypeStruct((rows_total, d), F32),
        compiler_params=_params("parallel"),
        name="out_proj",
    )(*args)


def kernel(x_prompt, x_sample, mem_prompt, cache_k, cache_v, state_hgrn, cache_mem_k, cache_mem_v, page_table, norm_g, w_in, hgrn_lb_logits, hgrn_norm_g, diff_norm_g, lambda_q1, lambda_k1, lambda_q2, lambda_k2, mem_norm_g, w_mem_kv, w_out, final_g):
    B, L, D = x_prompt.shape
    SB, T, _ = x_sample.shape
    depth, _, H_A, DK_A, DV_A = state_hgrn.shape
    _, n_pool, PAGE, H_B, E_B = cache_k.shape
    DH_B = E_B // 2
    _, _, N_MEM, H_M, DH_M = cache_mem_k.shape
    W_A, W_K, W_B, W_M = H_A * DV_A, H_A * DK_A, H_B * E_B, H_M * DH_M
    MIX = W_A + W_B + W_M
    R = SAMPLE_ROWS
    TM = ROW_TILE
    assert T <= R and L % (2 * TM) == 0
    sizes = (2 * W_K + W_A, W_B, W_M, MIX)
    diff_scale = LOG2E * DH_B ** -0.5

    cache_k3 = cache_k.reshape(depth * n_pool, PAGE * H_B, E_B)
    cache_v3 = cache_v.reshape(depth * n_pool, PAGE * H_B, E_B)

    yp = x_prompt.reshape(B * L, D)
    ys = jnp.pad(x_sample, ((0, 0), (0, R - T), (0, 0))).reshape(SB * R, D)
    outs = {k: [] for k in ("kp", "vp", "sp", "mkp", "mvp", "ks", "vs", "ss")}
    for l in range(depth):
        lam_init = 0.8 - 0.6 * math.exp(-0.3 * l)
        lam = (jnp.exp(jnp.sum(lambda_q1[l] * lambda_k1[l])) - jnp.exp(jnp.sum(lambda_q2[l] * lambda_k2[l]))
               + lam_init).reshape(1).astype(F32)
        w_in_b = w_in[l].astype(BF16)
        w_out_b = w_out[l].astype(BF16)
        w_mem_t = w_mem_kv[l].T.astype(BF16)

        def channel_major(a):
            return jnp.transpose(a, (0, 2, 3, 1)).reshape(a.shape[0], W_M, N_MEM)

        def token_major(a):
            return jnp.transpose(a.reshape(a.shape[0], H_M, DH_M, N_MEM), (0, 3, 1, 2))

        mkt, mvt = _mem_kv(mem_prompt, mem_norm_g[l], w_mem_t)
        pa, dq, kb, vb, ko, vo, mq, z = _in_proj(yp, norm_g[l], w_in_b, sizes, H_B, TM, diff_scale, True,
                                                 "in_proj_prompt")
        pa_s, dq_s, ko_s, vo_s, mq_s, z_s = _in_proj(ys, norm_g[l], w_in_b, sizes, H_B, SB * R, diff_scale, False,
                                                     "in_proj_sample")
        oa, st, ob_s = _hgrn_sample_attn(
            pa.reshape(B, L, -1), hgrn_lb_logits, hgrn_norm_g[l], l, H_A, DK_A, DV_A,
            dq_s.reshape(SB, R, W_B), ko_s.reshape(SB, R * H_B, E_B), vo_s.reshape(SB, R * H_B, E_B),
            cache_k3, cache_v3, page_table + l * n_pool, lam, diff_norm_g[l], T, H_B, DH_B, 1.0 - lam_init)

        ob = _diff_prompt(dq.reshape(B, L, W_B), kb.reshape(B, L, W_B), vb.reshape(B, L, W_B),
                          lam, diff_norm_g[l], H_B, DH_B, 1.0 - lam_init)
        yp = _out_proj(yp, oa.reshape(B * L, W_A), ob.reshape(B * L, W_B), mq, z,
                       w_out_b, final_g, 2 * TM, l == depth - 1, mem=(mkt, mvt, H_M, DH_M, L))
        outs["kp"].append(ko.reshape(B, L, H_B, E_B))
        outs["vp"].append(vo.reshape(B, L, H_B, E_B))
        outs["sp"].append(jnp.swapaxes(st, -1, -2))
        outs["mkp"].append(token_major(mkt))
        outs["mvp"].append(token_major(mvt))

        oa_s, sst = _hgrn_sample(pa_s.reshape(SB, R, -1), hgrn_lb_logits, hgrn_norm_g[l],
                                 jnp.swapaxes(state_hgrn[l], -1, -2), l, T, H_A, DK_A, DV_A)
        om_s = _mem_attn(mq_s.reshape(SB, R, W_M), channel_major(cache_mem_k[l]), channel_major(cache_mem_v[l]),
                         H_M, DH_M)
        ys = _out_proj(ys, oa_s.reshape(SB * R, W_A), ob_s.reshape(SB * R, W_B), om_s.reshape(SB * R, W_M), z_s,
                       w_out_b, final_g, SB * R, l == depth - 1)
        outs["ks"].append(ko_s.reshape(SB, R, H_B, E_B)[:, :T])
        outs["vs"].append(vo_s.reshape(SB, R, H_B, E_B)[:, :T])
        outs["ss"].append(jnp.swapaxes(sst, -1, -2))

    y_prompt = yp.reshape(B, L, D)
    y_sample = ys.reshape(SB, R, D)[:, :T]
    return (y_prompt, y_sample, jnp.stack(outs["kp"]), jnp.stack(outs["vp"]), jnp.stack(outs["sp"]),
            jnp.stack(outs["mkp"]), jnp.stack(outs["mvp"]), jnp.stack(outs["ks"]), jnp.stack(outs["vs"]),
            jnp.stack(outs["ss"]))
```

```python
import functools
import math

import jax
import jax.numpy as jnp
from jax import lax
from jax.experimental import pallas as pl
from jax.experimental.pallas import tpu as pltpu

F32 = jnp.float32
BF16 = jnp.bfloat16
EPS = 1e-6
NEG = -1e30
LOG2E = math.log2(math.e)
VMEM_LIMIT = 48 * 1024 * 1024
SUBLANES = 8
SAMPLE_ROWS = SUBLANES
ROW_TILE = 512
RING_SLOTS = 4
PAGE_PREFETCH = 3

_NT = (((1,), (1,)), ((), ()))
_TN = (((0,), (0,)), ((), ()))


def _dot(a, b):
    return jnp.dot(a, b, preferred_element_type=F32)


def _dot_nt(a, b):
    return lax.dot_general(a, b, _NT, preferred_element_type=F32)


def _dot_tn(a, b):
    return lax.dot_general(a, b, _TN, preferred_element_type=F32)


def _rms(x, g):
    return x * lax.rsqrt(jnp.mean(x * x, axis=-1, keepdims=True) + EPS) * g


def _params(*sem):
    return pltpu.CompilerParams(dimension_semantics=sem, vmem_limit_bytes=VMEM_LIMIT)


def _emit_interleaved(streams):
    streams = list(streams)
    while streams:
        for gen in list(streams):
            try:
                next(gen)
            except StopIteration:
                streams.remove(gen)


def _split_halves(q, dh):
    lane = lax.broadcasted_iota(jnp.int32, q.shape, 1)
    zero = jnp.zeros_like(q)
    return jnp.concatenate([jnp.where(lane < dh, q, zero), jnp.where(lane < dh, zero, q)], axis=0)


def _in_proj_kernel(x_ref, g_ref, w_ref, *rest, sizes, H, prompt, scale):
    w_pa, w_b, w_m, mix = sizes
    if prompt:
        pa_ref, dq_ref, kb_ref, vb_ref, ko_ref, vo_ref, mq_ref, z_ref = rest
    else:
        pa_ref, dq_ref, ko_ref, vo_ref, mq_ref, z_ref = rest
    e = w_b // H
    h = _rms(x_ref[...], g_ref[...]).astype(BF16)
    c = 0
    pa_ref[...] = _dot(h, w_ref[:, c:c + w_pa])
    c += w_pa
    dq_ref[...] = (_dot(h, w_ref[:, c:c + w_b]) * scale).astype(BF16)
    c += w_b
    k = _dot(h, w_ref[:, c:c + w_b])
    c += w_b
    v = _dot(h, w_ref[:, c:c + w_b])
    c += w_b
    tm = k.shape[0]
    for hh in range(H):
        ko_ref[pl.ds(hh, tm, stride=H), :] = k[:, hh * e:(hh + 1) * e]
        vo_ref[pl.ds(hh, tm, stride=H), :] = v[:, hh * e:(hh + 1) * e]
    if prompt:
        kb_ref[...] = k.astype(BF16)
        vb_ref[...] = v.astype(BF16)
    mq_ref[...] = _dot(h, w_ref[:, c:c + w_m])
    c += w_m
    z_ref[...] = _dot(h, w_ref[:, c:c + mix]).astype(z_ref.dtype)


def _in_proj(x2d, g, w_bf16, sizes, H, tm, scale, prompt, name):
    m, d = x2d.shape
    n = w_bf16.shape[1]
    w_pa, w_b, w_m, mix = sizes
    e = w_b // H
    assert w_pa + 3 * w_b + w_m + mix == n and m % tm == 0

    def rows(width):
        return pl.BlockSpec((tm, width), lambda i: (i, 0))

    const = lambda i: (0, 0)
    in_specs = [rows(d), pl.BlockSpec((1, d), const), pl.BlockSpec((d, n), const)]
    args = [x2d, g.reshape(1, d), w_bf16]
    out_specs = [rows(w_pa), rows(w_b)]
    out_shape = [jax.ShapeDtypeStruct((m, w_pa), F32), jax.ShapeDtypeStruct((m, w_b), BF16)]
    if prompt:
        out_specs += [rows(w_b), rows(w_b)]
        out_shape += [jax.ShapeDtypeStruct((m, w_b), BF16), jax.ShapeDtypeStruct((m, w_b), BF16)]
    head_rows = pl.BlockSpec((tm * H, e), lambda i: (i, 0))
    out_specs += [head_rows, head_rows, rows(w_m), rows(mix)]
    out_shape += [jax.ShapeDtypeStruct((m * H, e), F32), jax.ShapeDtypeStruct((m * H, e), F32),
                  jax.ShapeDtypeStruct((m, w_m), F32), jax.ShapeDtypeStruct((m, mix), BF16)]
    return pl.pallas_call(
        functools.partial(_in_proj_kernel, sizes=sizes, H=H, prompt=prompt, scale=scale),
        grid=(m // tm,),
        in_specs=in_specs,
        out_specs=out_specs,
        out_shape=out_shape,
        compiler_params=_params("parallel"),
        name=name,
    )(*args)


def _lower_bound(lg, layer, axis=0):
    e = jnp.exp(lg - jnp.max(lg, axis=axis, keepdims=True))
    p = e / jnp.sum(e, axis=axis, keepdims=True)
    head = p[:layer + 1] if axis == 0 else p[:, :layer + 1]
    return jnp.sum(head, axis=axis, keepdims=True)


def _hgrn_sample_kernel(pa_ref, lg_ref, ng_ref, s0_ref, o_ref, sout_ref, *, NB, T, layer, H, DK, DV):
    W = H * DK
    R = pa_ref.shape[1]
    lb = _lower_bound(lg_ref[...], layer)
    row = lax.broadcasted_iota(jnp.int32, (R, 1), 0)
    valid = row < T
    for n in range(NB):
        f = lb + (1.0 - lb) * jax.nn.sigmoid(pa_ref[n, :, W:2 * W])
        g = jnp.where(valid, jnp.log2(f), 0.0)
        kk = jnp.where(valid, 1.0 - f, 0.0)
        q = pa_ref[n, :, 0:W]
        b = g
        sh = 1
        while sh < R:
            b = b + jnp.where(row >= sh, pltpu.roll(b, sh, 0), 0.0)
            sh *= 2
        bl = b[R - 1:R]
        q_in = (q * jnp.exp2(b)).astype(BF16)
        k_out = (kk * jnp.exp2(bl - b)).astype(BF16)
        ebl = jnp.exp2(bl)
        for h in range(H):
            ks = slice(h * DK, (h + 1) * DK)
            v = pa_ref[n, :, 2 * W + h * DV:2 * W + (h + 1) * DV]
            st = s0_ref[n, h]
            o = _dot_nt(q_in[:, ks], st.astype(BF16))
            for s in range(T):
                later = row >= s
                decay = jnp.exp2(jnp.where(later, b[:, ks] - b[s:s + 1, ks], 0.0))
                a_s = jnp.sum(jnp.where(later, q[:, ks] * kk[s:s + 1, ks] * decay, 0.0), axis=1, keepdims=True)
                o = o + a_s * v[s:s + 1, :]
            o_ref[n, :, h * DV:(h + 1) * DV] = _rms(o, ng_ref[...])
            sout_ref[n, h] = ebl[:, ks] * st + _dot_tn(v.astype(BF16), k_out[:, ks])


def _hgrn_sample(pa, lg, ng, s0t, layer, T, H, DK, DV, NB=4):
    B, R, wp = pa.shape
    assert B % NB == 0
    return pl.pallas_call(
        functools.partial(_hgrn_sample_kernel, NB=NB, T=T, layer=layer, H=H, DK=DK, DV=DV),
        grid=(B // NB,),
        in_specs=[pl.BlockSpec((NB, R, wp), lambda b: (b, 0, 0)),
                  pl.BlockSpec(lg.shape, lambda b: (0, 0)),
                  pl.BlockSpec((1, DV), lambda b: (0, 0)),
                  pl.BlockSpec((NB, H, DV, DK), lambda b: (b, 0, 0, 0))],
        out_specs=[pl.BlockSpec((NB, R, H * DV), lambda b: (b, 0, 0)),
                   pl.BlockSpec((NB, H, DV, DK), lambda b: (b, 0, 0, 0))],
        out_shape=[jax.ShapeDtypeStruct((B, R, H * DV), F32),
                   jax.ShapeDtypeStruct((B, H, DV, DK), F32)],
        compiler_params=_params("parallel"),
        name="hgrn_sample",
    )(pa, lg, ng.reshape(1, DV), s0t)


def _diff_prompt_kernel(slope_ref, lam_ref, q_ref, k_ref, v_ref, g_ref, o_ref, m_scr, l_scr, acc_scr,
                        *, TQ, DH, HP, out_scale):
    hg = pl.program_id(1)
    qi = pl.program_id(2)
    E = 2 * DH
    LANES = m_scr.shape[2]
    HT = TQ // 2
    slopes = [slope_ref[hg * HP + j] for j in range(HP)]
    qs = [_split_halves(q_ref[:, j * E:(j + 1) * E], DH) for j in range(HP)]

    m_scr[...] = jnp.full_like(m_scr, NEG)
    l_scr[...] = jnp.zeros_like(l_scr)
    acc_scr[...] = jnp.zeros_like(acc_scr)

    def update(row_slices, key0, nk, krel, masked):
        keys = pl.ds(pl.multiple_of(key0, nk), nk)
        key_pos = (lax.broadcasted_iota(jnp.int32, (1, nk), 1) + krel).astype(F32)
        for j in range(HP):
            cols = slice(j * E, (j + 1) * E)
            for rs in row_slices:
                nr = rs.stop - rs.start
                s = _dot_nt(qs[j][rs], k_ref[keys, cols]) + slopes[j] * key_pos
                if masked:
                    t = (rs.start + lax.broadcasted_iota(jnp.int32, (nr, nk), 0)) & (TQ - 1)
                    s = jnp.where(lax.broadcasted_iota(jnp.int32, (nr, nk), 1) + krel <= t, s, NEG)
                m_prev = m_scr[j, rs]
                m_new = jnp.maximum(m_prev, jnp.max(s, axis=1, keepdims=True))
                alpha = jnp.exp2(m_prev - m_new)
                p = jnp.exp2(s - jnp.tile(m_new, (1, nk // LANES)))
                l_scr[j, rs] = alpha * l_scr[j, rs] + jnp.sum(p, axis=1, keepdims=True)
                acc_scr[j, rs] = alpha * acc_scr[j, rs] + _dot(p.astype(BF16), v_ref[keys, cols])
                m_scr[j, rs] = m_new

    def body(kj, carry):
        update([slice(0, 2 * TQ)], kj * TQ, TQ, (kj - qi) * TQ, False)
        return carry

    lax.fori_loop(0, qi, body, 0)
    update([slice(0, 2 * TQ)], qi * TQ, HT, 0, True)
    update([slice(HT, TQ), slice(TQ + HT, 2 * TQ)], qi * TQ + HT, HT, HT, True)

    for j in range(HP):
        o2 = acc_scr[j] / l_scr[j]
        o = o2[:TQ] - lam_ref[0] * o2[TQ:]
        o_ref[:, j * E:(j + 1) * E] = (_rms(o, g_ref[...]) * out_scale).astype(o_ref.dtype)


def _diff_prompt(q, kb, vb, lam, ng, H, DH, out_scale, TQ=512, HP=4):
    B, L, W = q.shape
    E = 2 * DH
    slopes = jnp.asarray([LOG2E * 2.0 ** (-8.0 * (i + 1) / H) for i in range(H)], F32)
    assert L % TQ == 0 and TQ & (TQ - 1) == 0 and H % HP == 0
    return pl.pallas_call(
        functools.partial(_diff_prompt_kernel, TQ=TQ, DH=DH, HP=HP, out_scale=out_scale),
        grid=(B, H // HP, L // TQ),
        in_specs=[pl.BlockSpec(memory_space=pltpu.SMEM),
                  pl.BlockSpec(memory_space=pltpu.SMEM),
                  pl.BlockSpec((None, TQ, HP * E), lambda b, h, i: (b, i, h)),
                  pl.BlockSpec((None, L, HP * E), lambda b, h, i: (b, 0, h)),
                  pl.BlockSpec((None, L, HP * E), lambda b, h, i: (b, 0, h)),
                  pl.BlockSpec((1, E), lambda b, h, i: (0, 0))],
        out_specs=pl.BlockSpec((None, TQ, HP * E), lambda b, h, i: (b, i, h)),
        out_shape=jax.ShapeDtypeStruct((B, L, W), BF16),
        scratch_shapes=[pltpu.VMEM((HP, 2 * TQ, E), F32), pltpu.VMEM((HP, 2 * TQ, E), F32),
                        pltpu.VMEM((HP, 2 * TQ, E), F32)],
        compiler_params=_params("parallel", "parallel", "arbitrary"),
        name="diff_prompt",
    )(slopes, lam, q, kb, vb, ng.reshape(1, E))


def _hgrn_sample_attn_kernel(pt_ref, lam_ref, pa_ref, lg_ref, hng_ref, q_ref, kn_ref, vn_ref, dng_ref,
                             ck_ref, cv_ref, oa_ref, sout_ref, ob_ref,
                             s_scr, b_scr, kbuf, vbuf, sem, m_scr, l_scr, acc_scr,
                             *, C, NC, layer, H, DK, DV, SUB, sub_per_seq, total_sub, hsteps,
                             PP, PAGE, T, HB, DH, past, slopes, out_scale):
    fs = pl.program_id(0)
    n_fused = pl.num_programs(0)
    R = SAMPLE_ROWS
    NR = 2 * R
    E = 2 * DH
    W = H * DK
    fused_per_seq = sub_per_seq // SUB
    part = fs % fused_per_seq

    def page_copies(g, slot):
        seq = g // sub_per_seq
        first = (g % sub_per_seq) * PP
        copies = []
        for j in range(PP):
            page = pt_ref[seq, first + j]
            copies.append(pltpu.make_async_copy(ck_ref.at[page], kbuf.at[slot, j], sem.at[slot]))
            copies.append(pltpu.make_async_copy(cv_ref.at[page], vbuf.at[slot, j], sem.at[slot]))
        return copies

    @pl.when(fs == 0)
    def _():
        for g in range(min(PAGE_PREFETCH, total_sub)):
            for c in page_copies(g, g):
                c.start()
        m_scr[...] = jnp.full_like(m_scr, NEG)
        l_scr[...] = jnp.zeros_like(l_scr)
        acc_scr[...] = jnp.zeros_like(acc_scr)

    @pl.when(fs % hsteps == 0)
    def _():
        s_scr[...] = jnp.zeros_like(s_scr)

    lb = _lower_bound(lg_ref[...], layer)
    ti = lax.broadcasted_iota(jnp.int32, (C, C), 0)
    si = lax.broadcasted_iota(jnp.int32, (C, C), 1)
    lmat = (si <= ti).astype(BF16)
    xor_lower = jnp.where(ti > si, ti ^ si, 0)
    row = lax.broadcasted_iota(jnp.int32, (C, 1), 0)

    qs = [_split_halves(q_ref[:, h * E:(h + 1) * E], DH) for h in range(HB)]
    head = lax.broadcasted_iota(jnp.int32, (HB * NR, 1), 0) // NR
    slope = jnp.zeros((HB * NR, 1), F32)
    for h in range(HB):
        slope = jnp.where(head == h, slopes[h], slope)
    col = lax.broadcasted_iota(jnp.int32, (1, PP * PAGE), 1)

    def attend(k_tile, v_tile, n_tiles, kpos, ok, fresh):
        s = jnp.concatenate(
            [_dot_nt(qs[h], jnp.concatenate([k_tile(h, j) for j in range(n_tiles)], axis=0))
             for h in range(HB)], axis=0)
        yield
        s = s + slope * kpos
        if ok is not None:
            s = jnp.where(ok, s, NEG)
        m_prev, l_prev, acc_prev = m_scr[...], l_scr[...], acc_scr[...]
        if fresh is not None:
            m_prev = jnp.where(fresh, NEG, m_prev)
            l_prev = jnp.where(fresh, 0.0, l_prev)
            acc_prev = jnp.where(fresh, 0.0, acc_prev)
        m_new = jnp.maximum(m_prev, jnp.max(s, axis=-1, keepdims=True))
        alpha = jnp.exp2(m_prev - m_new)
        p = jnp.exp2(s - m_new)
        l_scr[...] = alpha * l_prev + jnp.sum(p, axis=-1, keepdims=True)
        pb = p.astype(BF16)
        yield
        pv = [_dot(pb[h * NR:(h + 1) * NR], jnp.concatenate([v_tile(h, j) for j in range(n_tiles)], axis=0))
              for h in range(HB)]
        acc_scr[...] = alpha * acc_prev + jnp.concatenate(pv, axis=0)
        m_scr[...] = m_new

    def hgrn_chunk(n):
        rows = slice(n * C, (n + 1) * C)
        f = lb + (1.0 - lb) * jax.nn.sigmoid(pa_ref[rows, W:2 * W])
        g = jnp.log2(f)
        kk = 1.0 - f
        q = pa_ref[rows, 0:W]
        g_hi = g.astype(BF16)
        r1 = g - g_hi.astype(F32)
        g_mid = r1.astype(BF16)
        g_lo = (r1 - g_mid.astype(F32)).astype(BF16)
        b = _dot(lmat, g_hi) + _dot(lmat, g_mid) + _dot(lmat, g_lo)
        b_scr[rows, :] = b
        bl = b_scr[(n + 1) * C - 1:(n + 1) * C, :]
        yield
        levels = []
        m = C // 2
        while m >= 1:
            upper = (row & (2 * m - 1)) >= m
            if m >= SUBLANES:
                side, d = [], []
                for blk in range(C // (2 * m)):
                    lo = slice(blk * 2 * m, blk * 2 * m + m)
                    hi = slice(blk * 2 * m + m, (blk + 1) * 2 * m)
                    bref = b_scr[n * C + blk * 2 * m + m - 1:n * C + blk * 2 * m + m, :]
                    side += [kk[lo], q[hi]]
                    d += [bref - b[lo], b[hi] - bref]
                side = jnp.concatenate(side, axis=0)
                d = jnp.concatenate(d, axis=0)
            else:
                side = jnp.where(upper, q, kk)
                if m == 4:
                    bref = jnp.concatenate(
                        [jnp.broadcast_to(b_scr[n * C + blk * 8 + 3:n * C + blk * 8 + 4, :], (8, W))
                         for blk in range(C // 8)], axis=0)
                    d = -jnp.abs(b - bref)
                elif m == 2:
                    d = jnp.where(upper, g + jnp.where((row & 3) == 3, pltpu.roll(g, 1, 0), 0.0),
                                  jnp.where((row & 3) == 0, pltpu.roll(g, C - 1, 0), 0.0))
                else:
                    d = jnp.where(upper, g, 0.0)
            levels.append((side * jnp.exp2(d)).astype(BF16))
            m //= 2
        q_bf = q.astype(BF16)
        k_bf = kk.astype(BF16)
        q_in = (q * jnp.exp2(b)).astype(BF16)
        k_out = (kk * jnp.exp2(bl - b)).astype(BF16)
        ebl = jnp.exp2(bl)
        yield
        for h in range(H):
            ks = slice(h * DK, (h + 1) * DK)
            a = jnp.where(ti == si, _dot_nt(q_bf[:, ks], k_bf[:, ks]), 0.0)
            m = 1
            for x in reversed(levels):
                a = jnp.where(xor_lower >= m, _dot_nt(x[:, ks], x[:, ks]), a)
                m *= 2
            v = pa_ref[rows, 2 * W + h * DV:2 * W + (h + 1) * DV].astype(BF16)
            st = s_scr[h]
            o = _dot_nt(q_in[:, ks], st.astype(BF16)) + _dot(a.astype(BF16), v)
            oa_ref[rows, h * DV:(h + 1) * DV] = _rms(o, hng_ref[...]).astype(oa_ref.dtype)
            s_scr[h] = ebl[:, ks] * st + _dot_tn(v, k_out[:, ks])

    chunk_every = SUB // NC
    for j in range(SUB):
        g = fs * SUB + j
        ahead = g + PAGE_PREFETCH

        def start_ahead(ahead=ahead, slot=(j + PAGE_PREFETCH) % RING_SLOTS):
            for c in page_copies(ahead, slot):
                c.start()

        if j + PAGE_PREFETCH < SUB:
            start_ahead()
        else:
            pl.when(fs < n_fused - 1)(start_ahead)
        slot = j % RING_SLOTS
        for c in page_copies(g, slot):
            c.wait()
        k_pages = kbuf.at[slot]
        v_pages = vbuf.at[slot]
        kpos = ((part * SUB + j) * (PP * PAGE) + col - past).astype(F32)
        streams = [attend(lambda h, t, kp=k_pages: kp[t, pl.ds(h, PAGE, stride=HB), :].astype(BF16),
                          lambda h, t, vp=v_pages: vp[t, pl.ds(h, PAGE, stride=HB), :].astype(BF16),
                          PP, kpos, None, (part == 0) if j == 0 else None)]
        if j % chunk_every == 0:
            streams.insert(0, hgrn_chunk(j // chunk_every))
        _emit_interleaved(streams)

    @pl.when(part == fused_per_seq - 1)
    def _():
        tq = lax.broadcasted_iota(jnp.int32, (HB * NR, PAGE), 0) & (R - 1)
        coln = lax.broadcasted_iota(jnp.int32, (HB * NR, PAGE), 1)
        pad = jnp.zeros((PAGE - R, E), F32)
        _emit_interleaved([attend(
            lambda h, t: jnp.concatenate([kn_ref[pl.ds(h, R, stride=HB), :], pad], axis=0).astype(BF16),
            lambda h, t: jnp.concatenate([vn_ref[pl.ds(h, R, stride=HB), :], pad], axis=0).astype(BF16),
            1, coln[0:1].astype(F32), (coln <= tq) & (coln < T), None)])
        o2 = acc_scr[...] / l_scr[...]
        for h in range(HB):
            o = o2[h * NR:h * NR + R] - lam_ref[0] * o2[h * NR + R:(h + 1) * NR]
            ob_ref[:, h * E:(h + 1) * E] = _rms(o, dng_ref[...]) * out_scale

    @pl.when(fs % hsteps == hsteps - 1)
    def _():
        sout_ref[...] = s_scr[...]


def _hgrn_sample_attn(pa, lg, hng, layer, H, DK, DV, q, kn, vn, cache_k, cache_v, page_ids, lam, dng,
                      T, HB, DH, out_scale, C=128, NC=4, PP=16):
    B, L, wp = pa.shape
    W = H * DK
    RB = C * NC
    SB = q.shape[0]
    n_pages = page_ids.shape[1]
    PAGE = cache_k.shape[1] // HB
    E = 2 * DH
    R = SAMPLE_ROWS
    slopes = tuple(LOG2E * 2.0 ** (-8.0 * (i + 1) / HB) for i in range(HB))
    hsteps = L // RB
    n_fused = B * hsteps
    sub_per_seq = n_pages // PP
    total_sub = SB * sub_per_seq
    SUB = total_sub // n_fused
    assert L % RB == 0 and wp == 2 * W + H * DV and n_pages % PP == 0 and q.shape[1] == R
    assert total_sub % n_fused == 0 and sub_per_seq % SUB == 0 and SUB % NC == 0 and SUB % RING_SLOTS == 0
    fused_per_seq = sub_per_seq // SUB
    seq_rows = pl.BlockSpec((None, R, HB * E), lambda i, pt: (i // fused_per_seq, 0, 0))
    new_rows = pl.BlockSpec((None, R * HB, E), lambda i, pt: (i // fused_per_seq, 0, 0))
    page_buf = pltpu.VMEM((RING_SLOTS, PP, PAGE * HB, E), F32)
    grid_spec = pltpu.PrefetchScalarGridSpec(
        num_scalar_prefetch=1,
        grid=(n_fused,),
        in_specs=[pl.BlockSpec(memory_space=pltpu.SMEM),
                  pl.BlockSpec((None, RB, wp), lambda i, pt: (i // hsteps, i % hsteps, 0)),
                  pl.BlockSpec(lg.shape, lambda i, pt: (0, 0)),
                  pl.BlockSpec((1, DV), lambda i, pt: (0, 0)),
                  seq_rows, new_rows, new_rows,
                  pl.BlockSpec((1, E), lambda i, pt: (0, 0)),
                  pl.BlockSpec(memory_space=pl.ANY),
                  pl.BlockSpec(memory_space=pl.ANY)],
        out_specs=[pl.BlockSpec((None, RB, H * DV), lambda i, pt: (i // hsteps, i % hsteps, 0)),
                   pl.BlockSpec((None, H, DV, DK), lambda i, pt: (i // hsteps, 0, 0, 0)),
                   seq_rows],
        scratch_shapes=[pltpu.VMEM((H, DV, DK), F32), pltpu.VMEM((RB, W), F32),
                        page_buf, page_buf, pltpu.SemaphoreType.DMA((RING_SLOTS,)),
                        pltpu.VMEM((HB * 2 * R, 1), F32), pltpu.VMEM((HB * 2 * R, 1), F32),
                        pltpu.VMEM((HB * 2 * R, E), F32)])
    return pl.pallas_call(
        functools.partial(_hgrn_sample_attn_kernel, C=C, NC=NC, layer=layer, H=H, DK=DK, DV=DV,
                          SUB=SUB, sub_per_seq=sub_per_seq, total_sub=total_sub, hsteps=hsteps,
                          PP=PP, PAGE=PAGE, T=T, HB=HB, DH=DH, past=n_pages * PAGE, slopes=slopes,
                          out_scale=out_scale),
        grid_spec=grid_spec,
        out_shape=[jax.ShapeDtypeStruct((B, L, H * DV), BF16),
                   jax.ShapeDtypeStruct((B, H, DV, DK), F32),
                   jax.ShapeDtypeStruct((SB, R, HB * E), F32)],
        compiler_params=_params("arbitrary"),
        name="hgrn_prompt_diff_sample",
    )(page_ids, lam, pa, lg, hng.reshape(1, DV), q, kn, vn, dng.reshape(1, E), cache_k, cache_v)


def _mem_kv_kernel(x_ref, g_ref, wt_ref, kt_ref, vt_ref):
    h = _rms(x_ref[...], g_ref[...]).astype(BF16)
    kv = _dot_nt(wt_ref[...], h)
    w = kt_ref.shape[0]
    kt_ref[...] = kv[:w]
    vt_ref[...] = kv[w:]


def _mem_kv(mem, g, wt_bf16):
    B, N, D = mem.shape
    w = wt_bf16.shape[0] // 2
    out = jax.ShapeDtypeStruct((B, w, N), F32)
    blk = pl.BlockSpec((None, w, N), lambda b: (b, 0, 0))
    return pl.pallas_call(
        _mem_kv_kernel,
        grid=(B,),
        in_specs=[pl.BlockSpec((None, N, D), lambda b: (b, 0, 0)),
                  pl.BlockSpec((1, D), lambda b: (0, 0)),
                  pl.BlockSpec((2 * w, D), lambda b: (0, 0))],
        out_specs=[blk, blk],
        out_shape=[out, out],
        compiler_params=_params("parallel"),
        name="mem_kv",
    )(mem, g.reshape(1, D), wt_bf16)


def _mem_attend(q, kt, vt, H, DH):
    T = q.shape[0]
    q = q * (LOG2E * DH ** -0.5)
    lane_h = lax.broadcasted_iota(jnp.int32, q.shape, 1) // DH
    qs = jnp.concatenate([jnp.where(lane_h == h, q, 0.0) for h in range(H)], axis=0).astype(BF16)
    s = _dot(qs, kt.astype(BF16))
    p = jnp.exp2(s - jnp.max(s, axis=-1, keepdims=True))
    o4 = _dot_nt(p.astype(BF16), vt.astype(BF16)) / jnp.sum(p, axis=-1, keepdims=True)
    o = jnp.zeros_like(q)
    for h in range(H):
        o = jnp.where(lane_h == h, o4[h * T:(h + 1) * T], o)
    return o


def _mem_attn_kernel(q_ref, kt_ref, vt_ref, o_ref, *, NB, H, DH):
    for n in range(NB):
        o_ref[n] = _mem_attend(q_ref[n], kt_ref[n], vt_ref[n], H, DH)


def _mem_attn(q, kt, vt, H, DH, NB=8):
    B, T, W = q.shape
    N = kt.shape[2]
    assert B % NB == 0
    kv = pl.BlockSpec((NB, W, N), lambda b: (b, 0, 0))
    return pl.pallas_call(
        functools.partial(_mem_attn_kernel, NB=NB, H=H, DH=DH),
        grid=(B // NB,),
        in_specs=[pl.BlockSpec((NB, T, W), lambda b: (b, 0, 0)), kv, kv],
        out_specs=pl.BlockSpec((NB, T, W), lambda b: (b, 0, 0)),
        out_shape=jax.ShapeDtypeStruct((B, T, W), F32),
        compiler_params=_params("parallel"),
        name="mem_attn",
    )(q, kt, vt)


def _out_kernel(x_ref, oa_ref, ob_ref, m_ref, z_ref, w_ref, g_ref, *rest, final, mem_heads):
    if mem_heads is None:
        (y_ref,) = rest
        om = m_ref[...]
    else:
        kt_ref, vt_ref, y_ref = rest
        om = _mem_attend(m_ref[...], kt_ref[...], vt_ref[...], *mem_heads)
    o = jnp.concatenate([oa_ref[...].astype(F32), ob_ref[...].astype(F32), om], axis=-1)
    z = z_ref[...].astype(F32)
    o = (o * (z * jax.nn.sigmoid(z))).astype(BF16)
    y = x_ref[...] + _dot(o, w_ref[...])
    y_ref[...] = _rms(y, g_ref[...]) if final else y


def _out_proj(x2d, oa, ob, m, z, w_bf16, g, tm, final, mem=None):
    rows_total, d = x2d.shape
    mix = w_bf16.shape[0]

    def rows(a):
        return pl.BlockSpec((tm, a.shape[1]), lambda i: (i, 0))

    in_specs = [rows(x2d), rows(oa), rows(ob), rows(m), rows(z),
                pl.BlockSpec((mix, d), lambda i: (0, 0)),
                pl.BlockSpec((1, d), lambda i: (0, 0))]
    args = [x2d, oa, ob, m, z, w_bf16, g.reshape(1, d)]
    mem_heads = None
    if mem is not None:
        kt, vt, H, DH, rows_per_batch = mem
        assert rows_per_batch % tm == 0
        tiles = rows_per_batch // tm
        kv = pl.BlockSpec((None,) + kt.shape[1:], lambda i: (i // tiles, 0, 0))
        in_specs += [kv, kv]
        args += [kt, vt]
        mem_heads = (H, DH)
    return pl.pallas_call(
        functools.partial(_out_kernel, final=final, mem_heads=mem_heads),
        grid=(rows_total // tm,),
        in_specs=in_specs,
        out_specs=rows(x2d),
        out_shape=jax.ShapeDtypeStruct((rows_total, d), F32),
        compiler_params=_params("parallel"),
        name="out_proj",
    )(*args)


def kernel(x_prompt, x_sample, mem_prompt, cache_k, cache_v, state_hgrn, cache_mem_k, cache_mem_v, page_table, norm_g, w_in, hgrn_lb_logits, hgrn_norm_g, diff_norm_g, lambda_q1, lambda_k1, lambda_q2, lambda_k2, mem_norm_g, w_mem_kv, w_out, final_g):
    B, L, D = x_prompt.shape
    SB, T, _ = x_sample.shape
    depth, _, H_A, DK_A, DV_A = state_hgrn.shape
    _, n_pool, PAGE, H_B, E_B = cache_k.shape
    DH_B = E_B // 2
    _, _, N_MEM, H_M, DH_M = cache_mem_k.shape
    W_A, W_K, W_B, W_M = H_A * DV_A, H_A * DK_A, H_B * E_B, H_M * DH_M
    MIX = W_A + W_B + W_M
    R = SAMPLE_ROWS
    TM = ROW_TILE
    assert T <= R and L % (2 * TM) == 0
    sizes = (2 * W_K + W_A, W_B, W_M, MIX)
    diff_scale = LOG2E * DH_B ** -0.5

    cache_k3 = cache_k.reshape(depth * n_pool, PAGE * H_B, E_B)
    cache_v3 = cache_v.reshape(depth * n_pool, PAGE * H_B, E_B)

    yp = x_prompt.reshape(B * L, D)
    ys = jnp.pad(x_sample, ((0, 0), (0, R - T), (0, 0))).reshape(SB * R, D)
    outs = {k: [] for k in ("kp", "vp", "sp", "mkp", "mvp", "ks", "vs", "ss")}
    for l in range(depth):
        lam_init = 0.8 - 0.6 * math.exp(-0.3 * l)
        lam = (jnp.exp(jnp.sum(lambda_q1[l] * lambda_k1[l])) - jnp.exp(jnp.sum(lambda_q2[l] * lambda_k2[l]))
               + lam_init).reshape(1).astype(F32)
        w_in_b = w_in[l].astype(BF16)
        w_out_b = w_out[l].astype(BF16)
        w_mem_t = w_mem_kv[l].T.astype(BF16)

        def channel_major(a):
            return jnp.transpose(a, (0, 2, 3, 1)).reshape(a.shape[0], W_M, N_MEM)

        def token_major(a):
            return jnp.transpose(a.reshape(a.shape[0], H_M, DH_M, N_MEM), (0, 3, 1, 2))

        mkt, mvt = _mem_kv(mem_prompt, mem_norm_g[l], w_mem_t)
        pa, dq, kb, vb, ko, vo, mq, z = _in_proj(yp, norm_g[l], w_in_b, sizes, H_B, TM, diff_scale, True,
                                                 "in_proj_prompt")
        pa_s, dq_s, ko_s, vo_s, mq_s, z_s = _in_proj(ys, norm_g[l], w_in_b, sizes, H_B, SB * R, diff_scale, False,
                                                     "in_proj_sample")
        oa, st, ob_s = _hgrn_sample_attn(
            pa.reshape(B, L, -1), hgrn_lb_logits, hgrn_norm_g[l], l, H_A, DK_A, DV_A,
            dq_s.reshape(SB, R, W_B), ko_s.reshape(SB, R * H_B, E_B), vo_s.reshape(SB, R * H_B, E_B),
            cache_k3, cache_v3, page_table + l * n_pool, lam, diff_norm_g[l], T, H_B, DH_B, 1.0 - lam_init)

        ob = _diff_prompt(dq.reshape(B, L, W_B), kb.reshape(B, L, W_B), vb.reshape(B, L, W_B),
                          lam, diff_norm_g[l], H_B, DH_B, 1.0 - lam_init)
        yp = _out_proj(yp, oa.reshape(B * L, W_A), ob.reshape(B * L, W_B), mq, z,
                       w_out_b, final_g, 2 * TM, l == depth - 1, mem=(mkt, mvt, H_M, DH_M, L))
        outs["kp"].append(ko.reshape(B, L, H_B, E_B))
        outs["vp"].append(vo.reshape(B, L, H_B, E_B))
        outs["sp"].append(jnp.swapaxes(st, -1, -2))
        outs["mkp"].append(token_major(mkt))
        outs["mvp"].append(token_major(mvt))

        oa_s, sst = _hgrn_sample(pa_s.reshape(SB, R, -1), hgrn_lb_logits, hgrn_norm_g[l],
                                 jnp.swapaxes(state_hgrn[l], -1, -2), l, T, H_A, DK_A, DV_A)
        om_s = _mem_attn(mq_s.reshape(SB, R, W_M), channel_major(cache_mem_k[l]), channel_major(cache_mem_v[l]),
                         H_M, DH_M)
        ys = _out_proj(ys, oa_s.reshape(SB * R, W_A), ob_s.reshape(SB * R, W_B), om_s.reshape(SB * R, W_M), z_s,
                       w_out_b, final_g, SB * R, l == depth - 1)
        outs["ks"].append(ko_s.reshape(SB, R, H_B, E_B)[:, :T])
        outs["vs"].append(vo_s.reshape(SB, R, H_B, E_B)[:, :T])
        outs["ss"].append(jnp.swapaxes(sst, -1, -2))

    y_prompt = yp.reshape(B, L, D)
    y_sample = ys.reshape(SB, R, D)[:, :T]
    return (y_prompt, y_sample, jnp.stack(outs["kp"]), jnp.stack(outs["vp"]), jnp.stack(outs["sp"]),
            jnp.stack(outs["mkp"]), jnp.stack(outs["mvp"]), jnp.stack(outs["ks"]), jnp.stack(outs["vs"]),
            jnp.stack(outs["ss"]))
```

```python
import functools
import math

import jax
import jax.numpy as jnp
from jax import lax
from jax.experimental import pallas as pl
from jax.experimental.pallas import tpu as pltpu

F32 = jnp.float32
BF16 = jnp.bfloat16
EPS = 1e-6
NEG = -1e30
LOG2E = math.log2(math.e)
VMEM_LIMIT = 48 * 1024 * 1024
SUBLANES = 8
SAMPLE_ROWS = SUBLANES
ROW_TILE = 512
RING_SLOTS = 4
PAGE_PREFETCH = 2

_NT = (((1,), (1,)), ((), ()))
_TN = (((0,), (0,)), ((), ()))


def _dot(a, b):
    return jnp.dot(a, b, preferred_element_type=F32)


def _dot_nt(a, b):
    return lax.dot_general(a, b, _NT, preferred_element_type=F32)


def _dot_tn(a, b):
    return lax.dot_general(a, b, _TN, preferred_element_type=F32)


def _rms(x, g):
    return x * lax.rsqrt(jnp.mean(x * x, axis=-1, keepdims=True) + EPS) * g


def _params(*sem):
    return pltpu.CompilerParams(dimension_semantics=sem, vmem_limit_bytes=VMEM_LIMIT)


def _emit_interleaved(streams):
    streams = list(streams)
    while streams:
        for gen in list(streams):
            try:
                next(gen)
            except StopIteration:
                streams.remove(gen)


def _split_halves(q, dh):
    lane = lax.broadcasted_iota(jnp.int32, q.shape, 1)
    zero = jnp.zeros_like(q)
    return jnp.concatenate([jnp.where(lane < dh, q, zero), jnp.where(lane < dh, zero, q)], axis=0)


def _in_proj_kernel(x_ref, g_ref, w_ref, *rest, sizes, H, prompt, scale):
    w_pa, w_b, w_m, mix = sizes
    if prompt:
        pa_ref, dq_ref, kb_ref, vb_ref, ko_ref, vo_ref, mq_ref, z_ref = rest
    else:
        pa_ref, dq_ref, ko_ref, vo_ref, mq_ref, z_ref = rest
    e = w_b // H
    h = _rms(x_ref[...], g_ref[...]).astype(BF16)
    c = 0
    pa_ref[...] = _dot(h, w_ref[:, c:c + w_pa])
    c += w_pa
    dq_ref[...] = (_dot(h, w_ref[:, c:c + w_b]) * scale).astype(BF16)
    c += w_b
    k = _dot(h, w_ref[:, c:c + w_b])
    c += w_b
    v = _dot(h, w_ref[:, c:c + w_b])
    c += w_b
    tm = k.shape[0]
    for hh in range(H):
        ko_ref[pl.ds(hh, tm, stride=H), :] = k[:, hh * e:(hh + 1) * e]
        vo_ref[pl.ds(hh, tm, stride=H), :] = v[:, hh * e:(hh + 1) * e]
    if prompt:
        kb_ref[...] = k.astype(BF16)
        vb_ref[...] = v.astype(BF16)
    mq_ref[...] = _dot(h, w_ref[:, c:c + w_m])
    c += w_m
    z_ref[...] = _dot(h, w_ref[:, c:c + mix]).astype(z_ref.dtype)


def _in_proj(x2d, g, w_bf16, sizes, H, tm, scale, prompt, name):
    m, d = x2d.shape
    n = w_bf16.shape[1]
    w_pa, w_b, w_m, mix = sizes
    e = w_b // H
    assert w_pa + 3 * w_b + w_m + mix == n and m % tm == 0

    def rows(width):
        return pl.BlockSpec((tm, width), lambda i: (i, 0))

    const = lambda i: (0, 0)
    in_specs = [rows(d), pl.BlockSpec((1, d), const), pl.BlockSpec((d, n), const)]
    args = [x2d, g.reshape(1, d), w_bf16]
    out_specs = [rows(w_pa), rows(w_b)]
    out_shape = [jax.ShapeDtypeStruct((m, w_pa), F32), jax.ShapeDtypeStruct((m, w_b), BF16)]
    if prompt:
        out_specs += [rows(w_b), rows(w_b)]
        out_shape += [jax.ShapeDtypeStruct((m, w_b), BF16), jax.ShapeDtypeStruct((m, w_b), BF16)]
    head_rows = pl.BlockSpec((tm * H, e), lambda i: (i, 0))
    out_specs += [head_rows, head_rows, rows(w_m), rows(mix)]
    out_shape += [jax.ShapeDtypeStruct((m * H, e), F32), jax.ShapeDtypeStruct((m * H, e), F32),
                  jax.ShapeDtypeStruct((m, w_m), F32), jax.ShapeDtypeStruct((m, mix), BF16)]
    return pl.pallas_call(
        functools.partial(_in_proj_kernel, sizes=sizes, H=H, prompt=prompt, scale=scale),
        grid=(m // tm,),
        in_specs=in_specs,
        out_specs=out_specs,
        out_shape=out_shape,
        compiler_params=_params("parallel"),
        name=name,
    )(*args)


def _lower_bound(lg, layer, axis=0):
    e = jnp.exp(lg - jnp.max(lg, axis=axis, keepdims=True))
    p = e / jnp.sum(e, axis=axis, keepdims=True)
    head = p[:layer + 1] if axis == 0 else p[:, :layer + 1]
    return jnp.sum(head, axis=axis, keepdims=True)


def _hgrn_sample_kernel(pa_ref, lg_ref, ng_ref, s0_ref, o_ref, sout_ref, *, NB, T, layer, H, DK, DV):
    W = H * DK
    R = pa_ref.shape[1]
    lb = _lower_bound(lg_ref[...], layer)
    row = lax.broadcasted_iota(jnp.int32, (R, 1), 0)
    valid = row < T
    for n in range(NB):
        f = lb + (1.0 - lb) * jax.nn.sigmoid(pa_ref[n, :, W:2 * W])
        g = jnp.where(valid, jnp.log2(f), 0.0)
        kk = jnp.where(valid, 1.0 - f, 0.0)
        q = pa_ref[n, :, 0:W]
        b = g
        sh = 1
        while sh < R:
            b = b + jnp.where(row >= sh, pltpu.roll(b, sh, 0), 0.0)
            sh *= 2
        bl = b[R - 1:R]
        q_in = (q * jnp.exp2(b)).astype(BF16)
        k_out = (kk * jnp.exp2(bl - b)).astype(BF16)
        ebl = jnp.exp2(bl)
        for h in range(H):
            ks = slice(h * DK, (h + 1) * DK)
            v = pa_ref[n, :, 2 * W + h * DV:2 * W + (h + 1) * DV]
            st = s0_ref[n, h]
            o = _dot_nt(q_in[:, ks], st.astype(BF16))
            for s in range(T):
                later = row >= s
                decay = jnp.exp2(jnp.where(later, b[:, ks] - b[s:s + 1, ks], 0.0))
                a_s = jnp.sum(jnp.where(later, q[:, ks] * kk[s:s + 1, ks] * decay, 0.0), axis=1, keepdims=True)
                o = o + a_s * v[s:s + 1, :]
            o_ref[n, :, h * DV:(h + 1) * DV] = _rms(o, ng_ref[...])
            sout_ref[n, h] = ebl[:, ks] * st + _dot_tn(v.astype(BF16), k_out[:, ks])


def _hgrn_sample(pa, lg, ng, s0t, layer, T, H, DK, DV, NB=4):
    B, R, wp = pa.shape
    assert B % NB == 0
    return pl.pallas_call(
        functools.partial(_hgrn_sample_kernel, NB=NB, T=T, layer=layer, H=H, DK=DK, DV=DV),
        grid=(B // NB,),
        in_specs=[pl.BlockSpec((NB, R, wp), lambda b: (b, 0, 0)),
                  pl.BlockSpec(lg.shape, lambda b: (0, 0)),
                  pl.BlockSpec((1, DV), lambda b: (0, 0)),
                  pl.BlockSpec((NB, H, DV, DK), lambda b: (b, 0, 0, 0))],
        out_specs=[pl.BlockSpec((NB, R, H * DV), lambda b: (b, 0, 0)),
                   pl.BlockSpec((NB, H, DV, DK), lambda b: (b, 0, 0, 0))],
        out_shape=[jax.ShapeDtypeStruct((B, R, H * DV), F32),
                   jax.ShapeDtypeStruct((B, H, DV, DK), F32)],
        compiler_params=_params("parallel"),
        name="hgrn_sample",
    )(pa, lg, ng.reshape(1, DV), s0t)


def _diff_prompt_kernel(slope_ref, lam_ref, q_ref, k_ref, v_ref, g_ref, o_ref, m_scr, l_scr, acc_scr,
                        *, TQ, DH, HP, out_scale):
    hg = pl.program_id(1)
    qi = pl.program_id(2)
    E = 2 * DH
    LANES = m_scr.shape[2]
    HT = TQ // 2
    slopes = [slope_ref[hg * HP + j] for j in range(HP)]
    qs = [_split_halves(q_ref[:, j * E:(j + 1) * E], DH) for j in range(HP)]

    m_scr[...] = jnp.full_like(m_scr, NEG)
    l_scr[...] = jnp.zeros_like(l_scr)
    acc_scr[...] = jnp.zeros_like(acc_scr)

    def update(row_slices, key0, nk, krel, masked):
        keys = pl.ds(pl.multiple_of(key0, nk), nk)
        key_pos = (lax.broadcasted_iota(jnp.int32, (1, nk), 1) + krel).astype(F32)
        for j in range(HP):
            cols = slice(j * E, (j + 1) * E)
            for rs in row_slices:
                nr = rs.stop - rs.start
                s = _dot_nt(qs[j][rs], k_ref[keys, cols]) + slopes[j] * key_pos
                if masked:
                    t = (rs.start + lax.broadcasted_iota(jnp.int32, (nr, nk), 0)) & (TQ - 1)
                    s = jnp.where(lax.broadcasted_iota(jnp.int32, (nr, nk), 1) + krel <= t, s, NEG)
                m_prev = m_scr[j, rs]
                m_new = jnp.maximum(m_prev, jnp.max(s, axis=1, keepdims=True))
                alpha = jnp.exp2(m_prev - m_new)
                p = jnp.exp2(s - jnp.tile(m_new, (1, nk // LANES)))
                l_scr[j, rs] = alpha * l_scr[j, rs] + jnp.sum(p, axis=1, keepdims=True)
                acc_scr[j, rs] = alpha * acc_scr[j, rs] + _dot(p.astype(BF16), v_ref[keys, cols])
                m_scr[j, rs] = m_new

    def body(kj, carry):
        update([slice(0, 2 * TQ)], kj * TQ, TQ, (kj - qi) * TQ, False)
        return carry

    lax.fori_loop(0, qi, body, 0)
    update([slice(0, 2 * TQ)], qi * TQ, HT, 0, True)
    update([slice(HT, TQ), slice(TQ + HT, 2 * TQ)], qi * TQ + HT, HT, HT, True)

    for j in range(HP):
        o2 = acc_scr[j] / l_scr[j]
        o = o2[:TQ] - lam_ref[0] * o2[TQ:]
        o_ref[:, j * E:(j + 1) * E] = (_rms(o, g_ref[...]) * out_scale).astype(o_ref.dtype)


def _diff_prompt(q, kb, vb, lam, ng, H, DH, out_scale, TQ=512, HP=4):
    B, L, W = q.shape
    E = 2 * DH
    slopes = jnp.asarray([LOG2E * 2.0 ** (-8.0 * (i + 1) / H) for i in range(H)], F32)
    assert L % TQ == 0 and TQ & (TQ - 1) == 0 and H % HP == 0
    return pl.pallas_call(
        functools.partial(_diff_prompt_kernel, TQ=TQ, DH=DH, HP=HP, out_scale=out_scale),
        grid=(B, H // HP, L // TQ),
        in_specs=[pl.BlockSpec(memory_space=pltpu.SMEM),
                  pl.BlockSpec(memory_space=pltpu.SMEM),
                  pl.BlockSpec((None, TQ, HP * E), lambda b, h, i: (b, i, h)),
                  pl.BlockSpec((None, L, HP * E), lambda b, h, i: (b, 0, h)),
                  pl.BlockSpec((None, L, HP * E), lambda b, h, i: (b, 0, h)),
                  pl.BlockSpec((1, E), lambda b, h, i: (0, 0))],
        out_specs=pl.BlockSpec((None, TQ, HP * E), lambda b, h, i: (b, i, h)),
        out_shape=jax.ShapeDtypeStruct((B, L, W), BF16),
        scratch_shapes=[pltpu.VMEM((HP, 2 * TQ, E), F32), pltpu.VMEM((HP, 2 * TQ, E), F32),
                        pltpu.VMEM((HP, 2 * TQ, E), F32)],
        compiler_params=_params("parallel", "parallel", "arbitrary"),
        name="diff_prompt",
    )(slopes, lam, q, kb, vb, ng.reshape(1, E))


def _hgrn_sample_attn_kernel(pt_ref, lam_ref, pa_ref, lg_ref, hng_ref, q_ref, kn_ref, vn_ref, dng_ref,
                             ck_ref, cv_ref, oa_ref, sout_ref, ob_ref,
                             s_scr, b_scr, kbuf, vbuf, sem, m_scr, l_scr, acc_scr,
                             *, C, NC, layer, H, DK, DV, SUB, sub_per_seq, total_sub, hsteps,
                             PP, PAGE, T, HB, DH, past, slopes, out_scale):
    fs = pl.program_id(0)
    n_fused = pl.num_programs(0)
    R = SAMPLE_ROWS
    NR = 2 * R
    E = 2 * DH
    W = H * DK
    fused_per_seq = sub_per_seq // SUB
    part = fs % fused_per_seq

    def page_copies(g, slot):
        seq = g // sub_per_seq
        first = (g % sub_per_seq) * PP
        copies = []
        for j in range(PP):
            page = pt_ref[seq, first + j]
            copies.append(pltpu.make_async_copy(ck_ref.at[page], kbuf.at[slot, j], sem.at[slot]))
            copies.append(pltpu.make_async_copy(cv_ref.at[page], vbuf.at[slot, j], sem.at[slot]))
        return copies

    @pl.when(fs == 0)
    def _():
        for g in range(min(PAGE_PREFETCH, total_sub)):
            for i, c in enumerate(page_copies(g, g)):
                c.start(priority=i % 2)
        m_scr[...] = jnp.full_like(m_scr, NEG)
        l_scr[...] = jnp.zeros_like(l_scr)
        acc_scr[...] = jnp.zeros_like(acc_scr)

    @pl.when(fs % hsteps == 0)
    def _():
        s_scr[...] = jnp.zeros_like(s_scr)

    lb = _lower_bound(lg_ref[...], layer)
    ti = lax.broadcasted_iota(jnp.int32, (C, C), 0)
    si = lax.broadcasted_iota(jnp.int32, (C, C), 1)
    lmat = (si <= ti).astype(BF16)
    xor_lower = jnp.where(ti > si, ti ^ si, 0)
    row = lax.broadcasted_iota(jnp.int32, (C, 1), 0)

    qs = [_split_halves(q_ref[:, h * E:(h + 1) * E], DH) for h in range(HB)]
    head = lax.broadcasted_iota(jnp.int32, (HB * NR, 1), 0) // NR
    slope = jnp.zeros((HB * NR, 1), F32)
    for h in range(HB):
        slope = jnp.where(head == h, slopes[h], slope)
    col = lax.broadcasted_iota(jnp.int32, (1, PP * PAGE), 1)

    def attend(k_tile, v_tile, n_tiles, kpos, ok, fresh):
        s = jnp.concatenate(
            [_dot_nt(qs[h], jnp.concatenate([k_tile(h, j) for j in range(n_tiles)], axis=0))
             for h in range(HB)], axis=0)
        yield
        s = s + slope * kpos
        if ok is not None:
            s = jnp.where(ok, s, NEG)
        m_prev, l_prev, acc_prev = m_scr[...], l_scr[...], acc_scr[...]
        if fresh is not None:
            m_prev = jnp.where(fresh, NEG, m_prev)
            l_prev = jnp.where(fresh, 0.0, l_prev)
            acc_prev = jnp.where(fresh, 0.0, acc_prev)
        m_new = jnp.maximum(m_prev, jnp.max(s, axis=-1, keepdims=True))
        alpha = jnp.exp2(m_prev - m_new)
        p = jnp.exp2(s - m_new)
        l_scr[...] = alpha * l_prev + jnp.sum(p, axis=-1, keepdims=True)
        pb = p.astype(BF16)
        yield
        pv = [_dot(pb[h * NR:(h + 1) * NR], jnp.concatenate([v_tile(h, j) for j in range(n_tiles)], axis=0))
              for h in range(HB)]
        acc_scr[...] = alpha * acc_prev + jnp.concatenate(pv, axis=0)
        m_scr[...] = m_new

    def hgrn_chunk(n):
        rows = slice(n * C, (n + 1) * C)
        f = lb + (1.0 - lb) * jax.nn.sigmoid(pa_ref[rows, W:2 * W])
        g = jnp.log2(f)
        kk = 1.0 - f
        q = pa_ref[rows, 0:W]
        g_hi = g.astype(BF16)
        r1 = g - g_hi.astype(F32)
        g_mid = r1.astype(BF16)
        g_lo = (r1 - g_mid.astype(F32)).astype(BF16)
        b = _dot(lmat, g_hi) + _dot(lmat, g_mid) + _dot(lmat, g_lo)
        b_scr[rows, :] = b
        bl = b_scr[(n + 1) * C - 1:(n + 1) * C, :]
        yield
        levels = []
        m = C // 2
        while m >= 1:
            upper = (row & (2 * m - 1)) >= m
            if m >= SUBLANES:
                side, d = [], []
                for blk in range(C // (2 * m)):
                    lo = slice(blk * 2 * m, blk * 2 * m + m)
                    hi = slice(blk * 2 * m + m, (blk + 1) * 2 * m)
                    bref = b_scr[n * C + blk * 2 * m + m - 1:n * C + blk * 2 * m + m, :]
                    side += [kk[lo], q[hi]]
                    d += [bref - b[lo], b[hi] - bref]
                side = jnp.concatenate(side, axis=0)
                d = jnp.concatenate(d, axis=0)
            else:
                side = jnp.where(upper, q, kk)
                if m == 4:
                    bref = jnp.concatenate(
                        [jnp.broadcast_to(b_scr[n * C + blk * 8 + 3:n * C + blk * 8 + 4, :], (8, W))
                         for blk in range(C // 8)], axis=0)
                    d = -jnp.abs(b - bref)
                elif m == 2:
                    d = jnp.where(upper, g + jnp.where((row & 3) == 3, pltpu.roll(g, 1, 0), 0.0),
                                  jnp.where((row & 3) == 0, pltpu.roll(g, C - 1, 0), 0.0))
                else:
                    d = jnp.where(upper, g, 0.0)
            levels.append((side * jnp.exp2(d)).astype(BF16))
            m //= 2
        q_bf = q.astype(BF16)
        k_bf = kk.astype(BF16)
        q_in = (q * jnp.exp2(b)).astype(BF16)
        k_out = (kk * jnp.exp2(bl - b)).astype(BF16)
        ebl = jnp.exp2(bl)
        yield
        for h in range(H):
            ks = slice(h * DK, (h + 1) * DK)
            a = jnp.where(ti == si, _dot_nt(q_bf[:, ks], k_bf[:, ks]), 0.0)
            m = 1
            for x in reversed(levels):
                a = jnp.where(xor_lower >= m, _dot_nt(x[:, ks], x[:, ks]), a)
                m *= 2
            v = pa_ref[rows, 2 * W + h * DV:2 * W + (h + 1) * DV].astype(BF16)
            st = s_scr[h]
            o = _dot_nt(q_in[:, ks], st.astype(BF16)) + _dot(a.astype(BF16), v)
            oa_ref[rows, h * DV:(h + 1) * DV] = _rms(o, hng_ref[...]).astype(oa_ref.dtype)
            s_scr[h] = ebl[:, ks] * st + _dot_tn(v, k_out[:, ks])

    chunk_every = SUB // NC
    for j in range(SUB):
        g = fs * SUB + j
        ahead = g + PAGE_PREFETCH

        def start_ahead(ahead=ahead, slot=(j + PAGE_PREFETCH) % RING_SLOTS):
            for i, c in enumerate(page_copies(ahead, slot)):
                c.start(priority=i % 2)

        if j + PAGE_PREFETCH < SUB:
            start_ahead()
        else:
            pl.when(fs < n_fused - 1)(start_ahead)
        slot = j % RING_SLOTS
        for c in page_copies(g, slot):
            c.wait()
        k_pages = kbuf.at[slot]
        v_pages = vbuf.at[slot]
        kpos = ((part * SUB + j) * (PP * PAGE) + col - past).astype(F32)
        streams = [attend(lambda h, t, kp=k_pages: kp[t, pl.ds(h, PAGE, stride=HB), :].astype(BF16),
                          lambda h, t, vp=v_pages: vp[t, pl.ds(h, PAGE, stride=HB), :].astype(BF16),
                          PP, kpos, None, (part == 0) if j == 0 else None)]
        if j % chunk_every == 0:
            streams.insert(0, hgrn_chunk(j // chunk_every))
        _emit_interleaved(streams)

    @pl.when(part == fused_per_seq - 1)
    def _():
        tq = lax.broadcasted_iota(jnp.int32, (HB * NR, PAGE), 0) & (R - 1)
        coln = lax.broadcasted_iota(jnp.int32, (HB * NR, PAGE), 1)
        pad = jnp.zeros((PAGE - R, E), F32)
        _emit_interleaved([attend(
            lambda h, t: jnp.concatenate([kn_ref[pl.ds(h, R, stride=HB), :], pad], axis=0).astype(BF16),
            lambda h, t: jnp.concatenate([vn_ref[pl.ds(h, R, stride=HB), :], pad], axis=0).astype(BF16),
            1, coln[0:1].astype(F32), (coln <= tq) & (coln < T), None)])
        o2 = acc_scr[...] / l_scr[...]
        for h in range(HB):
            o = o2[h * NR:h * NR + R] - lam_ref[0] * o2[h * NR + R:(h + 1) * NR]
            ob_ref[:, h * E:(h + 1) * E] = _rms(o, dng_ref[...]) * out_scale

    @pl.when(fs % hsteps == hsteps - 1)
    def _():
        sout_ref[...] = s_scr[...]


def _hgrn_sample_attn(pa, lg, hng, layer, H, DK, DV, q, kn, vn, cache_k, cache_v, page_ids, lam, dng,
                      T, HB, DH, out_scale, C=128, NC=4, PP=16):
    B, L, wp = pa.shape
    W = H * DK
    RB = C * NC
    SB = q.shape[0]
    n_pages = page_ids.shape[1]
    PAGE = cache_k.shape[1] // HB
    E = 2 * DH
    R = SAMPLE_ROWS
    slopes = tuple(LOG2E * 2.0 ** (-8.0 * (i + 1) / HB) for i in range(HB))
    hsteps = L // RB
    n_fused = B * hsteps
    sub_per_seq = n_pages // PP
    total_sub = SB * sub_per_seq
    SUB = total_sub // n_fused
    assert L % RB == 0 and wp == 2 * W + H * DV and n_pages % PP == 0 and q.shape[1] == R
    assert total_sub % n_fused == 0 and sub_per_seq % SUB == 0 and SUB % NC == 0 and SUB % RING_SLOTS == 0
    fused_per_seq = sub_per_seq // SUB
    seq_rows = pl.BlockSpec((None, R, HB * E), lambda i, pt: (i // fused_per_seq, 0, 0))
    new_rows = pl.BlockSpec((None, R * HB, E), lambda i, pt: (i // fused_per_seq, 0, 0))
    page_buf = pltpu.VMEM((RING_SLOTS, PP, PAGE * HB, E), F32)
    grid_spec = pltpu.PrefetchScalarGridSpec(
        num_scalar_prefetch=1,
        grid=(n_fused,),
        in_specs=[pl.BlockSpec(memory_space=pltpu.SMEM),
                  pl.BlockSpec((None, RB, wp), lambda i, pt: (i // hsteps, i % hsteps, 0)),
                  pl.BlockSpec(lg.shape, lambda i, pt: (0, 0)),
                  pl.BlockSpec((1, DV), lambda i, pt: (0, 0)),
                  seq_rows, new_rows, new_rows,
                  pl.BlockSpec((1, E), lambda i, pt: (0, 0)),
                  pl.BlockSpec(memory_space=pl.ANY),
                  pl.BlockSpec(memory_space=pl.ANY)],
        out_specs=[pl.BlockSpec((None, RB, H * DV), lambda i, pt: (i // hsteps, i % hsteps, 0)),
                   pl.BlockSpec((None, H, DV, DK), lambda i, pt: (i // hsteps, 0, 0, 0)),
                   seq_rows],
        scratch_shapes=[pltpu.VMEM((H, DV, DK), F32), pltpu.VMEM((RB, W), F32),
                        page_buf, page_buf, pltpu.SemaphoreType.DMA((RING_SLOTS,)),
                        pltpu.VMEM((HB * 2 * R, 1), F32), pltpu.VMEM((HB * 2 * R, 1), F32),
                        pltpu.VMEM((HB * 2 * R, E), F32)])
    return pl.pallas_call(
        functools.partial(_hgrn_sample_attn_kernel, C=C, NC=NC, layer=layer, H=H, DK=DK, DV=DV,
                          SUB=SUB, sub_per_seq=sub_per_seq, total_sub=total_sub, hsteps=hsteps,
                          PP=PP, PAGE=PAGE, T=T, HB=HB, DH=DH, past=n_pages * PAGE, slopes=slopes,
                          out_scale=out_scale),
        grid_spec=grid_spec,
        out_shape=[jax.ShapeDtypeStruct((B, L, H * DV), BF16),
                   jax.ShapeDtypeStruct((B, H, DV, DK), F32),
                   jax.ShapeDtypeStruct((SB, R, HB * E), F32)],
        compiler_params=_params("arbitrary"),
        name="hgrn_prompt_diff_sample",
    )(page_ids, lam, pa, lg, hng.reshape(1, DV), q, kn, vn, dng.reshape(1, E), cache_k, cache_v)


def _mem_kv_kernel(x_ref, g_ref, wt_ref, kt_ref, vt_ref):
    h = _rms(x_ref[...], g_ref[...]).astype(BF16)
    kv = _dot_nt(wt_ref[...], h)
    w = kt_ref.shape[0]
    kt_ref[...] = kv[:w]
    vt_ref[...] = kv[w:]


def _mem_kv(mem, g, wt_bf16):
    B, N, D = mem.shape
    w = wt_bf16.shape[0] // 2
    out = jax.ShapeDtypeStruct((B, w, N), F32)
    blk = pl.BlockSpec((None, w, N), lambda b: (b, 0, 0))
    return pl.pallas_call(
        _mem_kv_kernel,
        grid=(B,),
        in_specs=[pl.BlockSpec((None, N, D), lambda b: (b, 0, 0)),
                  pl.BlockSpec((1, D), lambda b: (0, 0)),
                  pl.BlockSpec((2 * w, D), lambda b: (0, 0))],
        out_specs=[blk, blk],
        out_shape=[out, out],
        compiler_params=_params("parallel"),
        name="mem_kv",
    )(mem, g.reshape(1, D), wt_bf16)


def _mem_attend(q, kt, vt, H, DH):
    T = q.shape[0]
    q = q * (LOG2E * DH ** -0.5)
    lane_h = lax.broadcasted_iota(jnp.int32, q.shape, 1) // DH
    qs = jnp.concatenate([jnp.where(lane_h == h, q, 0.0) for h in range(H)], axis=0).astype(BF16)
    s = _dot(qs, kt.astype(BF16))
    p = jnp.exp2(s - jnp.max(s, axis=-1, keepdims=True))
    o4 = _dot_nt(p.astype(BF16), vt.astype(BF16)) / jnp.sum(p, axis=-1, keepdims=True)
    o = jnp.zeros_like(q)
    for h in range(H):
        o = jnp.where(lane_h == h, o4[h * T:(h + 1) * T], o)
    return o


def _mem_attn_kernel(q_ref, kt_ref, vt_ref, o_ref, *, NB, H, DH):
    for n in range(NB):
        o_ref[n] = _mem_attend(q_ref[n], kt_ref[n], vt_ref[n], H, DH)


def _mem_attn(q, kt, vt, H, DH, NB=8):
    B, T, W = q.shape
    N = kt.shape[2]
    assert B % NB == 0
    kv = pl.BlockSpec((NB, W, N), lambda b: (b, 0, 0))
    return pl.pallas_call(
        functools.partial(_mem_attn_kernel, NB=NB, H=H, DH=DH),
        grid=(B // NB,),
        in_specs=[pl.BlockSpec((NB, T, W), lambda b: (b, 0, 0)), kv, kv],
        out_specs=pl.BlockSpec((NB, T, W), lambda b: (b, 0, 0)),
        out_shape=jax.ShapeDtypeStruct((B, T, W), F32),
        compiler_params=_params("parallel"),
        name="mem_attn",
    )(q, kt, vt)


def _out_kernel(x_ref, oa_ref, ob_ref, m_ref, z_ref, w_ref, g_ref, *rest, final, mem_heads):
    if mem_heads is None:
        (y_ref,) = rest
        om = m_ref[...]
    else:
        kt_ref, vt_ref, y_ref = rest
        om = _mem_attend(m_ref[...], kt_ref[...], vt_ref[...], *mem_heads)
    o = jnp.concatenate([oa_ref[...].astype(F32), ob_ref[...].astype(F32), om], axis=-1)
    z = z_ref[...].astype(F32)
    o = (o * (z * jax.nn.sigmoid(z))).astype(BF16)
    y = x_ref[...] + _dot(o, w_ref[...])
    y_ref[...] = _rms(y, g_ref[...]) if final else y


def _out_proj(x2d, oa, ob, m, z, w_bf16, g, tm, final, mem=None):
    rows_total, d = x2d.shape
    mix = w_bf16.shape[0]

    def rows(a):
        return pl.BlockSpec((tm, a.shape[1]), lambda i: (i, 0))

    in_specs = [rows(x2d), rows(oa), rows(ob), rows(m), rows(z),
                pl.BlockSpec((mix, d), lambda i: (0, 0)),
                pl.BlockSpec((1, d), lambda i: (0, 0))]
    args = [x2d, oa, ob, m, z, w_bf16, g.reshape(1, d)]
    mem_heads = None
    if mem is not None:
        kt, vt, H, DH, rows_per_batch = mem
        assert rows_per_batch % tm == 0
        tiles = rows_per_batch // tm
        kv = pl.BlockSpec((None,) + kt.shape[1:], lambda i: (i // tiles, 0, 0))
        in_specs += [kv, kv]
        args += [kt, vt]
        mem_heads = (H, DH)
    return pl.pallas_call(
        functools.partial(_out_kernel, final=final, mem_heads=mem_heads),
        grid=(rows_total // tm,),
        in_specs=in_specs,
        out_specs=rows(x2d),
        out_shape=jax.ShapeDtypeStruct((rows_total, d), F32),
        compiler_params=_params("parallel"),
        name="out_proj",
    )(*args)


def kernel(x_prompt, x_sample, mem_prompt, cache_k, cache_v, state_hgrn, cache_mem_k, cache_mem_v, page_table, norm_g, w_in, hgrn_lb_logits, hgrn_norm_g, diff_norm_g, lambda_q1, lambda_k1, lambda_q2, lambda_k2, mem_norm_g, w_mem_kv, w_out, final_g):
    B, L, D = x_prompt.shape
    SB, T, _ = x_sample.shape
    depth, _, H_A, DK_A, DV_A = state_hgrn.shape
    _, n_pool, PAGE, H_B, E_B = cache_k.shape
    DH_B = E_B // 2
    _, _, N_MEM, H_M, DH_M = cache_mem_k.shape
    W_A, W_K, W_B, W_M = H_A * DV_A, H_A * DK_A, H_B * E_B, H_M * DH_M
    MIX = W_A + W_B + W_M
    R = SAMPLE_ROWS
    TM = ROW_TILE
    assert T <= R and L % (2 * TM) == 0
    sizes = (2 * W_K + W_A, W_B, W_M, MIX)
    diff_scale = LOG2E * DH_B ** -0.5

    cache_k3 = cache_k.reshape(depth * n_pool, PAGE * H_B, E_B)
    cache_v3 = cache_v.reshape(depth * n_pool, PAGE * H_B, E_B)

    yp = x_prompt.reshape(B * L, D)
    ys = jnp.pad(x_sample, ((0, 0), (0, R - T), (0, 0))).reshape(SB * R, D)
    outs = {k: [] for k in ("kp", "vp", "sp", "mkp", "mvp", "ks", "vs", "ss")}
    for l in range(depth):
        lam_init = 0.8 - 0.6 * math.exp(-0.3 * l)
        lam = (jnp.exp(jnp.sum(lambda_q1[l] * lambda_k1[l])) - jnp.exp(jnp.sum(lambda_q2[l] * lambda_k2[l]))
               + lam_init).reshape(1).astype(F32)
        w_in_b = w_in[l].astype(BF16)
        w_out_b = w_out[l].astype(BF16)
        w_mem_t = w_mem_kv[l].T.astype(BF16)

        def channel_major(a):
            return jnp.transpose(a, (0, 2, 3, 1)).reshape(a.shape[0], W_M, N_MEM)

        def token_major(a):
            return jnp.transpose(a.reshape(a.shape[0], H_M, DH_M, N_MEM), (0, 3, 1, 2))

        mkt, mvt = _mem_kv(mem_prompt, mem_norm_g[l], w_mem_t)
        pa, dq, kb, vb, ko, vo, mq, z = _in_proj(yp, norm_g[l], w_in_b, sizes, H_B, TM, diff_scale, True,
                                                 "in_proj_prompt")
        pa_s, dq_s, ko_s, vo_s, mq_s, z_s = _in_proj(ys, norm_g[l], w_in_b, sizes, H_B, SB * R, diff_scale, False,
                                                     "in_proj_sample")
        oa, st, ob_s = _hgrn_sample_attn(
            pa.reshape(B, L, -1), hgrn_lb_logits, hgrn_norm_g[l], l, H_A, DK_A, DV_A,
            dq_s.reshape(SB, R, W_B), ko_s.reshape(SB, R * H_B, E_B), vo_s.reshape(SB, R * H_B, E_B),
            cache_k3, cache_v3, page_table + l * n_pool, lam, diff_norm_g[l], T, H_B, DH_B, 1.0 - lam_init)

        ob = _diff_prompt(dq.reshape(B, L, W_B), kb.reshape(B, L, W_B), vb.reshape(B, L, W_B),
                          lam, diff_norm_g[l], H_B, DH_B, 1.0 - lam_init)
        yp = _out_proj(yp, oa.reshape(B * L, W_A), ob.reshape(B * L, W_B), mq, z,
                       w_out_b, final_g, 2 * TM, l == depth - 1, mem=(mkt, mvt, H_M, DH_M, L))
        outs["kp"].append(ko.reshape(B, L, H_B, E_B))
        outs["vp"].append(vo.reshape(B, L, H_B, E_B))
        outs["sp"].append(jnp.swapaxes(st, -1, -2))
        outs["mkp"].append(token_major(mkt))
        outs["mvp"].append(token_major(mvt))

        oa_s, sst = _hgrn_sample(pa_s.reshape(SB, R, -1), hgrn_lb_logits, hgrn_norm_g[l],
                                 jnp.swapaxes(state_hgrn[l], -1, -2), l, T, H_A, DK_A, DV_A)
        om_s = _mem_attn(mq_s.reshape(SB, R, W_M), channel_major(cache_mem_k[l]), channel_major(cache_mem_v[l]),
                         H_M, DH_M)
        ys = _out_proj(ys, oa_s.reshape(SB * R, W_A), ob_s.reshape(SB * R, W_B), om_s.reshape(SB * R, W_M), z_s,
                       w_out_b, final_g, SB * R, l == depth - 1)
        outs["ks"].append(ko_s.reshape(SB, R, H_B, E_B)[:, :T])
        outs["vs"].append(vo_s.reshape(SB, R, H_B, E_B)[:, :T])
        outs["ss"].append(jnp.swapaxes(sst, -1, -2))

    y_prompt = yp.reshape(B, L, D)
    y_sample = ys.reshape(SB, R, D)[:, :T]
    return (y_prompt, y_sample, jnp.stack(outs["kp"]), jnp.stack(outs["vp"]), jnp.stack(outs["sp"]),
            jnp.stack(outs["mkp"]), jnp.stack(outs["mvp"]), jnp.stack(outs["ks"]), jnp.stack(outs["vs"]),
            jnp.stack(outs["ss"]))
```
